```python
import jax, jax.numpy as jnp
from jax import lax
import numpy as np

D_MODEL = 2048
BATCH = 4
SEQ = 2048
DEPTH = 1

CHUNK = 64
N_LEFT_CHUNKS = 8
BAND_CHUNKS = N_LEFT_CHUNKS + 1
BAND = BAND_CHUNKS * CHUNK
HEAD_DIM = 128
D_MIX = D_MODEL
N_HEADS = D_MIX // HEAD_DIM
N_HEADS_A = N_HEADS // 2
N_HEADS_B = N_HEADS - N_HEADS_A
D_A = N_HEADS_A * HEAD_DIM
D_B = N_HEADS_B * HEAD_DIM
REL_CLIP = 256
N_REL = 2 * REL_CLIP + 1
Q_BLOCK = 128
D_FF = 256 * ((8 * D_MODEL // 3 + 255) // 256)
D_PLE = 256
EPS = 1e-6
NEG_INF = -1e30
D_IN = 3 * D_A + 3 * D_B + N_HEADS_B
SPLIT_POINTS = (D_A, 2 * D_A, 3 * D_A, 3 * D_A + D_B, 3 * D_A + 2 * D_B, 3 * D_A + 3 * D_B)

kernel_name = "hymba_chunked_relpos_fox_macaron"


def rms_norm(x, g):
    xf = x.astype(jnp.float32)
    y = xf * lax.rsqrt(jnp.mean(xf * xf, axis=-1, keepdims=True) + EPS)
    return (y * g.astype(jnp.float32)).astype(x.dtype)


def swiglu(x, w_gu, w_down):
    a, b = jnp.split(x @ w_gu, 2, axis=-1)
    return (jax.nn.silu(a) * b) @ w_down


def chunked_relpos_attention(q, k, v, rel_bias):
    B, S, H, Dh = q.shape
    nc = S // CHUNK
    pad = N_LEFT_CHUNKS * CHUNK
    qc = q.reshape(B, nc, CHUNK, H, Dh)
    kp = jnp.pad(k, ((0, 0), (pad, 0), (0, 0), (0, 0))).reshape(B, nc + N_LEFT_CHUNKS, CHUNK, H, Dh)
    vp = jnp.pad(v, ((0, 0), (pad, 0), (0, 0), (0, 0))).reshape(B, nc + N_LEFT_CHUNKS, CHUNK, H, Dh)
    kb = jnp.concatenate([kp[:, w:w + nc] for w in range(BAND_CHUNKS)], axis=2)
    vb = jnp.concatenate([vp[:, w:w + nc] for w in range(BAND_CHUNKS)], axis=2)
    s = jnp.einsum('bcihd,bcjhd->bchij', qc, kb).astype(jnp.float32) * (Dh ** -0.5)
    i_idx = jnp.arange(CHUNK)[:, None]
    j_idx = jnp.arange(BAND)[None, :]
    dist = jnp.clip(pad + i_idx - j_idx, -REL_CLIP, REL_CLIP) + REL_CLIP
    bias = rel_bias[:, dist].astype(jnp.float32)
    s = s + bias[None, None]
    c_idx = jnp.arange(nc)[:, None]
    w_idx = (jnp.arange(BAND) // CHUNK)[None, :]
    valid = (c_idx - N_LEFT_CHUNKS + w_idx) >= 0
    s = jnp.where(valid[None, :, None, None, :], s, NEG_INF)
    pr = jax.nn.softmax(s, axis=-1)
    o = jnp.einsum('bchij,bcjhd->bcihd', pr.astype(v.dtype), vb)
    return o.reshape(B, S, H, Dh)


def forgetting_attention(q, k, v, f_logit):
    B, S, H, Dh = q.shape
    log_f = jax.nn.log_sigmoid(f_logit.astype(jnp.float32))
    F = jnp.cumsum(log_f, axis=1)
    Ft = jnp.transpose(F, (0, 2, 1))
    scale = Dh ** -0.5
    outs = []
    for blk in range(S // Q_BLOCK):
        q0 = blk * Q_BLOCK
        q1 = q0 + Q_BLOCK
        s = jnp.einsum('bihd,bjhd->bhij', q[:, q0:q1], k[:, :q1]).astype(jnp.float32) * scale
        s = s + Ft[:, :, q0:q1, None] - Ft[:, :, None, :q1]
        mask = (q0 + jnp.arange(Q_BLOCK))[:, None] >= jnp.arange(q1)[None, :]
        s = jnp.where(mask[None, None], s, NEG_INF)
        pr = jax.nn.softmax(s, axis=-1)
        outs.append(jnp.einsum('bhij,bjhd->bihd', pr.astype(v.dtype), v[:, :q1]))
    return jnp.concatenate(outs, axis=1)


def setup_inputs(seed: int = 0) -> dict:
    key = jax.random.key(seed)
    ks = jax.random.split(key, 20)
    f32 = jnp.float32

    def nrm(k, shape, scale):
        return jax.random.normal(k, shape, f32) * scale

    def gain(k, shape):
        return 1.0 + 0.05 * jax.random.normal(k, shape, f32)

    return {
        "x": jax.random.normal(ks[0], (BATCH, SEQ, D_MODEL), f32),
        "p": jax.random.normal(ks[1], (DEPTH, BATCH, SEQ, D_PLE), f32),
        "g_ffn1": gain(ks[2], (DEPTH, D_MODEL)),
        "w_ffn1_gu": nrm(ks[3], (DEPTH, D_MODEL, 2 * D_FF), D_MODEL ** -0.5),
        "w_ffn1_down": nrm(ks[4], (DEPTH, D_FF, D_MODEL), D_FF ** -0.5),
        "g_mix": gain(ks[5], (DEPTH, D_MODEL)),
        "w_in": nrm(ks[6], (DEPTH, D_MODEL, D_IN), D_MODEL ** -0.5),
        "b_forget": 2.0 + 0.5 * jax.random.normal(ks[7], (DEPTH, N_HEADS_B), f32),
        "rel_bias": nrm(ks[8], (DEPTH, N_HEADS_A, N_REL), 0.5),
        "w_out": nrm(ks[9], (DEPTH, D_MIX, D_MODEL), D_MIX ** -0.5),
        "g_ffn2": gain(ks[10], (DEPTH, D_MODEL)),
        "w_ffn2_gu": nrm(ks[11], (DEPTH, D_MODEL, 2 * D_FF), D_MODEL ** -0.5),
        "w_ffn2_down": nrm(ks[12], (DEPTH, D_FF, D_MODEL), D_FF ** -0.5),
        "g_ple": gain(ks[13], (DEPTH, D_MODEL)),
        "w_ple_gate": nrm(ks[14], (DEPTH, D_MODEL, D_MODEL), D_MODEL ** -0.5),
        "w_ple_proj": nrm(ks[15], (DEPTH, D_PLE, D_MODEL), D_PLE ** -0.5),
        "g_final": gain(ks[16], (D_MODEL,)),
    }


def reference(x, p, g_ffn1, w_ffn1_gu, w_ffn1_down, g_mix, w_in, b_forget, rel_bias, w_out,
              g_ffn2, w_ffn2_gu, w_ffn2_down, g_ple, w_ple_gate, w_ple_proj, g_final):
    h = x
    B, S, _ = x.shape
    for i in range(DEPTH):
        h = h + 0.5 * swiglu(rms_norm(h, g_ffn1[i]), w_ffn1_gu[i], w_ffn1_down[i])
        u = rms_norm(h, g_mix[i])
        z = u @ w_in[i]
        qa, ka, va, qb, kb, vb, fl = jnp.split(z, SPLIT_POINTS, axis=-1)
        hd = lambda t, nh: t.reshape(B, S, nh, HEAD_DIM)
        o_a = chunked_relpos_attention(hd(qa, N_HEADS_A), hd(ka, N_HEADS_A), hd(va, N_HEADS_A), rel_bias[i])
        o_b = forgetting_attention(hd(qb, N_HEADS_B), hd(kb, N_HEADS_B), hd(vb, N_HEADS_B), fl + b_forget[i])
        o = jnp.concatenate([o_a.reshape(B, S, D_A), o_b.reshape(B, S, D_B)], axis=-1)
        h = h + o @ w_out[i]
        h = h + 0.5 * swiglu(rms_norm(h, g_ffn2[i]), w_ffn2_gu[i], w_ffn2_down[i])
        gate = jax.nn.sigmoid(rms_norm(h, g_ple[i]) @ w_ple_gate[i])
        h = h + gate * (p[i] @ w_ple_proj[i])
    return rms_norm(h, g_final)
```

```python
import functools

import jax
import jax.numpy as jnp
from jax import lax
from jax.experimental import pallas as pl
from jax.experimental.pallas import tpu as pltpu

F32 = jnp.float32
BF16 = jnp.bfloat16

D_MODEL = 2048
CHUNK = 64
N_LEFT_CHUNKS = 8
HEAD_DIM = 128
N_HEADS_A = 8
N_HEADS_B = 8
D_A = N_HEADS_A * HEAD_DIM
D_B = N_HEADS_B * HEAD_DIM
REL_CLIP = 256
D_FF = 5632
D_PLE = 256
EPS = 1e-6
NEG_INF = -1e30
D_QKV = 3 * D_A + 3 * D_B
SCALE = HEAD_DIM ** -0.5

LANES = 128
VMEM_LIMIT = 56 * 1024 * 1024

QB_A = 2 * CHUNK
BAND_A = (N_LEFT_CHUNKS + 2) * CHUNK
PAD_A = N_LEFT_CHUNKS * CHUNK

TQ_B = 256
TK_B = 256


def _cparams(n_axes):
    return pltpu.CompilerParams(
        dimension_semantics=("arbitrary",) * n_axes,
        vmem_limit_bytes=VMEM_LIMIT,
    )


def _rmsnorm_kernel(x_ref, g_ref, o_ref):
    x = x_ref[...]
    ms = jnp.mean(x * x, axis=-1, keepdims=True)
    o_ref[...] = ((x * lax.rsqrt(ms + EPS)) * g_ref[...]).astype(o_ref.dtype)


def _rmsnorm(x, g, out_dtype, tm=512):
    m, d = x.shape
    return pl.pallas_call(
        _rmsnorm_kernel,
        grid=(m // tm,),
        in_specs=[pl.BlockSpec((tm, d), lambda i: (i, 0)),
                  pl.BlockSpec((1, d), lambda i: (0, 0))],
        out_specs=pl.BlockSpec((tm, d), lambda i: (i, 0)),
        out_shape=jax.ShapeDtypeStruct((m, d), out_dtype),
        compiler_params=_cparams(1),
        name="rmsnorm",
    )(x, g.reshape(1, d))


def _dot(a, b):
    return jnp.dot(a, b, preferred_element_type=F32)


def _gateup_kernel(a_ref, wg_ref, wu_ref, o_ref, wg_s, wu_s):
    @pl.when(pl.program_id(1) == 0)
    def _():
        wg_s[...] = wg_ref[...].astype(BF16)
        wu_s[...] = wu_ref[...].astype(BF16)

    a = a_ref[...]
    g = _dot(a, wg_s[...])
    u = _dot(a, wu_s[...])
    o_ref[...] = ((g * jax.nn.sigmoid(g)) * u).astype(o_ref.dtype)


def _gateup(a, w_gu, tm=1024, tn=512):
    m, k = a.shape
    nj = D_FF // tn
    return pl.pallas_call(
        _gateup_kernel,
        grid=(nj, m // tm),
        in_specs=[pl.BlockSpec((tm, k), lambda j, i: (i, 0)),
                  pl.BlockSpec((k, tn), lambda j, i: (0, j)),
                  pl.BlockSpec((k, tn), lambda j, i: (0, j + nj))],
        out_specs=pl.BlockSpec((tm, tn), lambda j, i: (i, j)),
        out_shape=jax.ShapeDtypeStruct((m, D_FF), BF16),
        scratch_shapes=[pltpu.VMEM((k, tn), BF16), pltpu.VMEM((k, tn), BF16)],
        compiler_params=_cparams(2),
        name="ffn_gateup",
    )(a, w_gu, w_gu)


def _proj_kernel(a_ref, w_ref, o_ref, w_s):
    @pl.when(pl.program_id(1) == 0)
    def _():
        w_s[...] = w_ref[...].astype(BF16)

    o_ref[...] = _dot(a_ref[...], w_s[...]).astype(o_ref.dtype)


def _proj(a, w, n_out, tm=1024, tn=1024):
    m, k = a.shape
    return pl.pallas_call(
        _proj_kernel,
        grid=(n_out // tn, m // tm),
        in_specs=[pl.BlockSpec((tm, k), lambda j, i: (i, 0)),
                  pl.BlockSpec((k, tn), lambda j, i: (0, j))],
        out_specs=pl.BlockSpec((tm, tn), lambda j, i: (i, j)),
        out_shape=jax.ShapeDtypeStruct((m, n_out), BF16),
        scratch_shapes=[pltpu.VMEM((k, tn), BF16)],
        compiler_params=_cparams(2),
        name="in_proj",
    )(a, w)


def _resid_kernel(scale, a_ref, w_ref, r_ref, o_ref, w_s):
    @pl.when(pl.program_id(1) == 0)
    def _():
        w_s[...] = w_ref[...].astype(BF16)

    o_ref[...] = r_ref[...] + scale * _dot(a_ref[...], w_s[...])


def _resid_matmul(a, w, resid, scale, tm, tn, name):
    m, k = a.shape
    n = w.shape[1]
    return pl.pallas_call(
        functools.partial(_resid_kernel, scale),
        grid=(n // tn, m // tm),
        in_specs=[pl.BlockSpec((tm, k), lambda j, i: (i, 0)),
                  pl.BlockSpec((k, tn), lambda j, i: (0, j)),
                  pl.BlockSpec((tm, tn), lambda j, i: (i, j))],
        out_specs=pl.BlockSpec((tm, tn), lambda j, i: (i, j)),
        out_shape=jax.ShapeDtypeStruct((m, n), F32),
        scratch_shapes=[pltpu.VMEM((k, tn), BF16)],
        compiler_params=_cparams(2),
        name=name,
    )(a, w, resid)


def _outproj_kernel(oa_ref, ob_ref, wa_ref, wb_ref, r_ref, o_ref, wa_s, wb_s):
    @pl.when(pl.program_id(1) == 0)
    def _():
        wa_s[...] = wa_ref[...].astype(BF16)
        wb_s[...] = wb_ref[...].astype(BF16)

    acc = _dot(oa_ref[...], wa_s[...]) + _dot(ob_ref[...], wb_s[...])
    o_ref[...] = r_ref[...] + acc


def _outproj(o_a, o_b, w_out, resid, tm=1024, tn=1024):
    m = o_a.shape[0]
    n = w_out.shape[1]
    return pl.pallas_call(
        _outproj_kernel,
        grid=(n // tn, m // tm),
        in_specs=[pl.BlockSpec((tm, D_A), lambda j, i: (i, 0)),
                  pl.BlockSpec((tm, D_B), lambda j, i: (i, 0)),
                  pl.BlockSpec((D_A, tn), lambda j, i: (0, j)),
                  pl.BlockSpec((D_B, tn), lambda j, i: (1, j)),
                  pl.BlockSpec((tm, tn), lambda j, i: (i, j))],
        out_specs=pl.BlockSpec((tm, tn), lambda j, i: (i, j)),
        out_shape=jax.ShapeDtypeStruct((m, n), F32),
        scratch_shapes=[pltpu.VMEM((D_A, tn), BF16), pltpu.VMEM((D_B, tn), BF16)],
        compiler_params=_cparams(2),
        name="out_proj",
    )(o_a, o_b, w_out, w_out, resid)


def _ple_kernel(a_ref, p_ref, wg_ref, wp_ref, r_ref, o_ref, wg_s, wp_s):
    @pl.when(pl.program_id(1) == 0)
    def _():
        wg_s[...] = wg_ref[...].astype(BF16)
        wp_s[...] = wp_ref[...].astype(BF16)

    gate = jax.nn.sigmoid(_dot(a_ref[...], wg_s[...]))
    emb = _dot(p_ref[...].astype(BF16), wp_s[...])
    o_ref[...] = r_ref[...] + gate * emb


def _ple(a, p, w_gate, w_proj, resid, tm=1024, tn=1024):
    m, k = a.shape
    n = w_gate.shape[1]
    return pl.pallas_call(
        _ple_kernel,
        grid=(n // tn, m // tm),
        in_specs=[pl.BlockSpec((tm, k), lambda j, i: (i, 0)),
                  pl.BlockSpec((tm, D_PLE), lambda j, i: (i, 0)),
                  pl.BlockSpec((k, tn), lambda j, i: (0, j)),
                  pl.BlockSpec((D_PLE, tn), lambda j, i: (0, j)),
                  pl.BlockSpec((tm, tn), lambda j, i: (i, j))],
        out_specs=pl.BlockSpec((tm, tn), lambda j, i: (i, j)),
        out_shape=jax.ShapeDtypeStruct((m, n), F32),
        scratch_shapes=[pltpu.VMEM((k, tn), BF16), pltpu.VMEM((D_PLE, tn), BF16)],
        compiler_params=_cparams(2),
        name="ple",
    )(a, p, w_gate, w_proj, resid)


def _attn_a_block(q, k, v, bias, key_lo):
    nk = k.shape[0]
    s = lax.dot_general(q, k, (((1,), (1,)), ((), ())), preferred_element_type=F32) * SCALE
    s = s + bias
    row = lax.broadcasted_iota(jnp.int32, (QB_A, nk), 0)
    col = lax.broadcasted_iota(jnp.int32, (QB_A, nk), 1) + key_lo
    first = row < CHUNK
    valid = (first & (col < BAND_A - CHUNK)) | (jnp.logical_not(first) & (col >= CHUNK))
    s = jnp.where(valid, s, NEG_INF)
    mx = jnp.max(s, axis=-1, keepdims=True)
    e = jnp.exp(s - mx)
    den = jnp.sum(e, axis=-1, keepdims=True)
    o = _dot(e.astype(BF16), v)
    return o / den


def _attn_a_kernel(q_ref, k_ref, v_ref, bias_ref, o_ref):
    seq = q_ref.shape[0]
    n_blocks = seq // QB_A
    n_head_blocks = PAD_A // QB_A
    for m in range(n_head_blocks):
        nk = (m + 1) * QB_A
        lo = BAND_A - nk
        o = _attn_a_block(q_ref[m * QB_A:(m + 1) * QB_A, :], k_ref[0:nk, :], v_ref[0:nk, :],
                          bias_ref[:, lo:], lo)
        o_ref[m * QB_A:(m + 1) * QB_A, :] = o.astype(o_ref.dtype)

    def body(m, carry):
        q0 = pl.multiple_of(m * QB_A, QB_A)
        k0 = pl.multiple_of(m * QB_A - PAD_A, QB_A)
        o = _attn_a_block(q_ref[pl.ds(q0, QB_A), :], k_ref[pl.ds(k0, BAND_A), :],
                          v_ref[pl.ds(k0, BAND_A), :], bias_ref[...], 0)
        o_ref[pl.ds(q0, QB_A), :] = o.astype(o_ref.dtype)
        return carry

    lax.fori_loop(n_head_blocks, n_blocks, body, 0)


def _attn_a(z, bias_tab):
    b, s, _ = z.shape
    ha = N_HEADS_A
    return pl.pallas_call(
        _attn_a_kernel,
        grid=(b, ha),
        in_specs=[pl.BlockSpec((None, s, HEAD_DIM), lambda bi, h: (bi, 0, h)),
                  pl.BlockSpec((None, s, HEAD_DIM), lambda bi, h: (bi, 0, ha + h)),
                  pl.BlockSpec((None, s, HEAD_DIM), lambda bi, h: (bi, 0, 2 * ha + h)),
                  pl.BlockSpec((None, QB_A, BAND_A), lambda bi, h: (h, 0, 0))],
        out_specs=pl.BlockSpec((None, s, HEAD_DIM), lambda bi, h: (bi, 0, h)),
        out_shape=jax.ShapeDtypeStruct((b, s, D_A), BF16),
        compiler_params=_cparams(2),
        name="attn_band",
    )(z, z, z, bias_tab)


def _fox_prep_kernel(u_ref, wf_ref, bf_ref, f_ref, ft_ref):
    seq = u_ref.shape[0]
    fl = _dot(u_ref[...], wf_ref[...]) + bf_ref[...]
    x = jnp.minimum(fl, 0.0) - jnp.log1p(jnp.exp(-jnp.abs(fl)))
    row = lax.broadcasted_iota(jnp.int32, x.shape, 0)
    shift = 1
    while shift < seq:
        x = x + jnp.where(row >= shift, pltpu.roll(x, shift, axis=0), 0.0)
        shift *= 2
    f_ref[...] = x
    ft_ref[...] = jnp.transpose(x)[:N_HEADS_B, :]


def _fox_prep(u3, w_f, b_f):
    b, s, d = u3.shape
    return pl.pallas_call(
        _fox_prep_kernel,
        grid=(b,),
        in_specs=[pl.BlockSpec((None, s, d), lambda bi: (bi, 0, 0)),
                  pl.BlockSpec((d, LANES), lambda bi: (0, 0)),
                  pl.BlockSpec((1, LANES), lambda bi: (0, 0))],
        out_specs=[pl.BlockSpec((None, s, LANES), lambda bi: (bi, 0, 0)),
                   pl.BlockSpec((None, N_HEADS_B, s), lambda bi: (bi, 0, 0))],
        out_shape=[jax.ShapeDtypeStruct((b, s, LANES), F32),
                   jax.ShapeDtypeStruct((b, N_HEADS_B, s), F32)],
        compiler_params=_cparams(1),
        name="fox_prep",
    )(u3, w_f, b_f)


def _fox_kernel(q_ref, k_ref, v_ref, fq_ref, fk_ref, o_ref):
    seq = q_ref.shape[0]
    h = pl.program_id(1)
    lane = lax.broadcasted_iota(jnp.int32, (TQ_B, LANES), 1)
    row = lax.broadcasted_iota(jnp.int32, (TQ_B, TK_B), 0)
    col = lax.broadcasted_iota(jnp.int32, (TQ_B, TK_B), 1)

    def q_block(qi, carry):
        q0 = pl.multiple_of(qi * TQ_B, TQ_B)
        q = q_ref[pl.ds(q0, TQ_B), :]
        fq = jnp.sum(jnp.where(lane == h, fq_ref[pl.ds(q0, TQ_B), :], 0.0), axis=1, keepdims=True)

        def k_step(kj, state, diagonal):
            m_prev, l_prev, acc = state
            k0 = pl.multiple_of(kj * TK_B, TK_B)
            kk = k_ref[pl.ds(k0, TK_B), :]
            vv = v_ref[pl.ds(k0, TK_B), :]
            fk = fk_ref[h, pl.ds(kj, 1), :]
            s = lax.dot_general(q, kk, (((1,), (1,)), ((), ())), preferred_element_type=F32) * SCALE
            s = (s + fq) - fk
            if diagonal:
                s = jnp.where(row >= col, s, NEG_INF)
            m_new = jnp.maximum(m_prev, jnp.max(s, axis=-1, keepdims=True))
            alpha = jnp.exp(m_prev - m_new)
            e = jnp.exp(s - m_new)
            l_new = alpha * l_prev + jnp.sum(e, axis=-1, keepdims=True)
            acc = alpha * acc + _dot(e.astype(BF16), vv)
            return m_new, l_new, acc

        init = (jnp.full((TQ_B, 1), NEG_INF, F32), jnp.zeros((TQ_B, 1), F32),
                jnp.zeros((TQ_B, HEAD_DIM), F32))
        state = lax.fori_loop(0, qi, lambda kj, st: k_step(kj, st, False), init)
        _, l_fin, acc = k_step(qi, state, True)
        o_ref[pl.ds(q0, TQ_B), :] = (acc / l_fin).astype(o_ref.dtype)
        return carry

    lax.fori_loop(0, seq // TQ_B, q_block, 0)


def _fox(z, f_col, f_row):
    b, s, _ = z.shape
    hb = N_HEADS_B
    base = 3 * N_HEADS_A
    return pl.pallas_call(
        _fox_kernel,
        grid=(b, hb),
        in_specs=[pl.BlockSpec((None, s, HEAD_DIM), lambda bi, h: (bi, 0, base + h)),
                  pl.BlockSpec((None, s, HEAD_DIM), lambda bi, h: (bi, 0, base + hb + h)),
                  pl.BlockSpec((None, s, HEAD_DIM), lambda bi, h: (bi, 0, base + 2 * hb + h)),
                  pl.BlockSpec((None, s, LANES), lambda bi, h: (bi, 0, 0)),
                  pl.BlockSpec((None, hb, s // TK_B, TK_B), lambda bi, h: (bi, 0, 0, 0))],
        out_specs=pl.BlockSpec((None, s, HEAD_DIM), lambda bi, h: (bi, 0, h)),
        out_shape=jax.ShapeDtypeStruct((b, s, D_B), BF16),
        compiler_params=_cparams(2),
        name="attn_fox",
    )(z, z, z, f_col, f_row.reshape(b, hb, s // TK_B, TK_B))


def _rel_bias_table(rel_bias):
    i = jnp.arange(QB_A)[:, None]
    j = jnp.arange(BAND_A)[None, :]
    dist = jnp.clip(i + PAD_A - j, -REL_CLIP, REL_CLIP) + REL_CLIP
    return rel_bias[:, dist]


def kernel(x, p, g_ffn1, w_ffn1_gu, w_ffn1_down, g_mix, w_in, b_forget, rel_bias, w_out,
           g_ffn2, w_ffn2_gu, w_ffn2_down, g_ple, w_ple_gate, w_ple_proj, g_final):
    b, s, d = x.shape
    m = b * s
    depth = p.shape[0]
    h = x.reshape(m, d)
    for i in range(depth):
        xn = _rmsnorm(h, g_ffn1[i], BF16)
        hid = _gateup(xn, w_ffn1_gu[i])
        h = _resid_matmul(hid, w_ffn1_down[i], h, 0.5, 512, 512, "ffn_down")
        u = _rmsnorm(h, g_mix[i], BF16)
        z = _proj(u, w_in[i], D_QKV).reshape(b, s, D_QKV)
        w_f = jnp.pad(w_in[i][:, D_QKV:], ((0, 0), (0, LANES - N_HEADS_B))).astype(BF16)
        b_f = jnp.pad(b_forget[i], (0, LANES - N_HEADS_B)).reshape(1, LANES)
        f_col, f_row = _fox_prep(u.reshape(b, s, d), w_f, b_f)
        o_a = _attn_a(z, _rel_bias_table(rel_bias[i]))
        o_b = _fox(z, f_col, f_row)
        h = _outproj(o_a.reshape(m, D_A), o_b.reshape(m, D_B), w_out[i], h)
        xn = _rmsnorm(h, g_ffn2[i], BF16)
        hid = _gateup(xn, w_ffn2_gu[i])
        h = _resid_matmul(hid, w_ffn2_down[i], h, 0.5, 512, 512, "ffn_down")
        hn = _rmsnorm(h, g_ple[i], BF16)
        h = _ple(hn, p[i].reshape(m, D_PLE), w_ple_gate[i], w_ple_proj[i], h)
    return _rmsnorm(h, g_final, F32).reshape(b, s, d)
```

```python
import functools

import jax
import jax.numpy as jnp
from jax import lax
from jax.experimental import pallas as pl
from jax.experimental.pallas import tpu as pltpu

F32 = jnp.float32
BF16 = jnp.bfloat16

D_MODEL = 2048
CHUNK = 64
N_LEFT_CHUNKS = 8
HEAD_DIM = 128
N_HEADS_A = 8
N_HEADS_B = 8
D_A = N_HEADS_A * HEAD_DIM
D_B = N_HEADS_B * HEAD_DIM
REL_CLIP = 256
D_FF = 5632
D_PLE = 256
EPS = 1e-6
NEG_INF = -1e30
D_QKV = 3 * D_A + 3 * D_B
SCALE = HEAD_DIM ** -0.5

LANES = 128
VMEM_LIMIT = 56 * 1024 * 1024

QB_A = 2 * CHUNK
BAND_A = (N_LEFT_CHUNKS + 2) * CHUNK
PAD_A = N_LEFT_CHUNKS * CHUNK
REL_ROW = 1024

TQ_B = 256


def _cparams(n_axes):
    return pltpu.CompilerParams(
        dimension_semantics=("arbitrary",) * n_axes,
        vmem_limit_bytes=VMEM_LIMIT,
    )


def _rmsnorm_kernel(x_ref, g_ref, o_ref):
    x = x_ref[...]
    ms = jnp.mean(x * x, axis=-1, keepdims=True)
    o_ref[...] = ((x * lax.rsqrt(ms + EPS)) * g_ref[...]).astype(o_ref.dtype)


def _rmsnorm(x, g, out_dtype, tm=512):
    m, d = x.shape
    return pl.pallas_call(
        _rmsnorm_kernel,
        grid=(m // tm,),
        in_specs=[pl.BlockSpec((tm, d), lambda i: (i, 0)),
                  pl.BlockSpec((1, d), lambda i: (0, 0))],
        out_specs=pl.BlockSpec((tm, d), lambda i: (i, 0)),
        out_shape=jax.ShapeDtypeStruct((m, d), out_dtype),
        compiler_params=_cparams(1),
        name="rmsnorm",
    )(x, g.reshape(1, d))


def _dot(a, b):
    return jnp.dot(a, b, preferred_element_type=F32)


def _gateup_kernel(a_ref, wg_ref, wu_ref, o_ref, wg_s, wu_s):
    @pl.when(pl.program_id(1) == 0)
    def _():
        wg_s[...] = wg_ref[...].astype(BF16)
        wu_s[...] = wu_ref[...].astype(BF16)

    a = a_ref[...]
    g = _dot(a, wg_s[...])
    u = _dot(a, wu_s[...])
    o_ref[...] = ((g * jax.nn.sigmoid(g)) * u).astype(o_ref.dtype)


def _gateup(a, w_gu, tm=1024, tn=512):
    m, k = a.shape
    nj = D_FF // tn
    return pl.pallas_call(
        _gateup_kernel,
        grid=(nj, m // tm),
        in_specs=[pl.BlockSpec((tm, k), lambda j, i: (i, 0)),
                  pl.BlockSpec((k, tn), lambda j, i: (0, j)),
                  pl.BlockSpec((k, tn), lambda j, i: (0, j + nj))],
        out_specs=pl.BlockSpec((tm, tn), lambda j, i: (i, j)),
        out_shape=jax.ShapeDtypeStruct((m, D_FF), BF16),
        scratch_shapes=[pltpu.VMEM((k, tn), BF16), pltpu.VMEM((k, tn), BF16)],
        compiler_params=_cparams(2),
        name="ffn_gateup",
    )(a, w_gu, w_gu)


def _proj_kernel(a_ref, wt_ref, o_ref, w_s):
    @pl.when(pl.program_id(1) == 0)
    def _():
        w_s[...] = jnp.transpose(wt_ref[...]).astype(BF16)

    o_ref[...] = _dot(a_ref[...], w_s[...]).astype(o_ref.dtype)


def _proj(a, w_t, n_out, tm=1024, tn=1024):
    m, k = a.shape
    return pl.pallas_call(
        _proj_kernel,
        grid=(n_out // tn, m // tm),
        in_specs=[pl.BlockSpec((tm, k), lambda j, i: (i, 0)),
                  pl.BlockSpec((tn, k), lambda j, i: (j, 0))],
        out_specs=pl.BlockSpec((tm, tn), lambda j, i: (i, j)),
        out_shape=jax.ShapeDtypeStruct((m, n_out), BF16),
        scratch_shapes=[pltpu.VMEM((k, tn), BF16)],
        compiler_params=_cparams(2),
        name="in_proj",
    )(a, w_t)


def _resid_kernel(scale, a_ref, w_ref, r_ref, o_ref, w_s):
    @pl.when(pl.program_id(1) == 0)
    def _():
        w_s[...] = w_ref[...].astype(BF16)

    o_ref[...] = r_ref[...] + scale * _dot(a_ref[...], w_s[...])


def _resid_matmul(a, w, resid, scale, tm, tn, name):
    m, k = a.shape
    n = w.shape[1]
    return pl.pallas_call(
        functools.partial(_resid_kernel, scale),
        grid=(n // tn, m // tm),
        in_specs=[pl.BlockSpec((tm, k), lambda j, i: (i, 0)),
                  pl.BlockSpec((k, tn), lambda j, i: (0, j)),
                  pl.BlockSpec((tm, tn), lambda j, i: (i, j))],
        out_specs=pl.BlockSpec((tm, tn), lambda j, i: (i, j)),
        out_shape=jax.ShapeDtypeStruct((m, n), F32),
        scratch_shapes=[pltpu.VMEM((k, tn), BF16)],
        compiler_params=_cparams(2),
        name=name,
    )(a, w, resid)


def _outproj_kernel(oa_ref, ob_ref, wa_ref, wb_ref, r_ref, o_ref, wa_s, wb_s):
    @pl.when(pl.program_id(1) == 0)
    def _():
        wa_s[...] = wa_ref[...].astype(BF16)
        wb_s[...] = wb_ref[...].astype(BF16)

    acc = _dot(oa_ref[...], wa_s[...]) + _dot(ob_ref[...], wb_s[...])
    o_ref[...] = r_ref[...] + acc


def _outproj(o_a, o_b, w_out, resid, tm=1024, tn=1024):
    m = o_a.shape[0]
    n = w_out.shape[1]
    return pl.pallas_call(
        _outproj_kernel,
        grid=(n // tn, m // tm),
        in_specs=[pl.BlockSpec((tm, D_A), lambda j, i: (i, 0)),
                  pl.BlockSpec((tm, D_B), lambda j, i: (i, 0)),
                  pl.BlockSpec((D_A, tn), lambda j, i: (0, j)),
                  pl.BlockSpec((D_B, tn), lambda j, i: (1, j)),
                  pl.BlockSpec((tm, tn), lambda j, i: (i, j))],
        out_specs=pl.BlockSpec((tm, tn), lambda j, i: (i, j)),
        out_shape=jax.ShapeDtypeStruct((m, n), F32),
        scratch_shapes=[pltpu.VMEM((D_A, tn), BF16), pltpu.VMEM((D_B, tn), BF16)],
        compiler_params=_cparams(2),
        name="out_proj",
    )(o_a, o_b, w_out, w_out, resid)


def _ple_kernel(a_ref, p_ref, wg_ref, wp_ref, r_ref, o_ref, wg_s, wp_s):
    @pl.when(pl.program_id(1) == 0)
    def _():
        wg_s[...] = wg_ref[...].astype(BF16)
        wp_s[...] = wp_ref[...].astype(BF16)

    gate = jax.nn.sigmoid(_dot(a_ref[...], wg_s[...]))
    emb = _dot(p_ref[...].astype(BF16), wp_s[...])
    o_ref[...] = r_ref[...] + gate * emb


def _ple(a, p, w_gate, w_proj, resid, tm=1024, tn=1024):
    m, k = a.shape
    n = w_gate.shape[1]
    return pl.pallas_call(
        _ple_kernel,
        grid=(n // tn, m // tm),
        in_specs=[pl.BlockSpec((tm, k), lambda j, i: (i, 0)),
                  pl.BlockSpec((tm, D_PLE), lambda j, i: (i, 0)),
                  pl.BlockSpec((k, tn), lambda j, i: (0, j)),
                  pl.BlockSpec((D_PLE, tn), lambda j, i: (0, j)),
                  pl.BlockSpec((tm, tn), lambda j, i: (i, j))],
        out_specs=pl.BlockSpec((tm, tn), lambda j, i: (i, j)),
        out_shape=jax.ShapeDtypeStruct((m, n), F32),
        scratch_shapes=[pltpu.VMEM((k, tn), BF16), pltpu.VMEM((D_PLE, tn), BF16)],
        compiler_params=_cparams(2),
        name="ple",
    )(a, p, w_gate, w_proj, resid)


def _attn_a_scores(q, k):
    return lax.dot_general(q, k, (((1,), (1,)), ((), ())), preferred_element_type=F32)


def _attn_a_softmax_pv(s, v, bias, key_lo):
    nk = v.shape[0]
    s = s * SCALE + bias
    row = lax.broadcasted_iota(jnp.int32, (QB_A, nk), 0)
    col = lax.broadcasted_iota(jnp.int32, (QB_A, nk), 1) + key_lo
    first = row < CHUNK
    valid = (first & (col < BAND_A - CHUNK)) | (jnp.logical_not(first) & (col >= CHUNK))
    s = jnp.where(valid, s, NEG_INF)
    mx = jnp.max(s, axis=-1, keepdims=True)
    e = jnp.exp(s - mx)
    den = jnp.sum(e, axis=-1, keepdims=True)
    o = _dot(e.astype(BF16), v)
    return o / den


def _attn_a_kernel(g_ref, q_ref, k_ref, v_ref, o_ref, bias_s):
    @pl.when(pl.program_id(1) == 0)
    def _():
        g = jnp.broadcast_to(g_ref[...], (QB_A, REL_ROW))
        bias_s[...] = pltpu.roll(g, 0, axis=1, stride=1, stride_axis=0)[:, :BAND_A]

    n_blocks = q_ref.shape[0] // QB_A

    def window(m):
        k0 = max(m * QB_A - PAD_A, 0)
        return k0, (m + 1) * QB_A

    def scores(m):
        k0, k1 = window(m)
        return _attn_a_scores(q_ref[m * QB_A:k1, :], k_ref[k0:k1, :])

    s_next = scores(0)
    for m in range(n_blocks):
        s = s_next
        if m + 1 < n_blocks:
            s_next = scores(m + 1)
        k0, k1 = window(m)
        lo = BAND_A - (k1 - k0)
        o = _attn_a_softmax_pv(s, v_ref[k0:k1, :], bias_s[:, lo:], lo)
        o_ref[m * QB_A:k1, :] = o.astype(o_ref.dtype)


def _attn_a(z, bias_rows):
    b, s, _ = z.shape
    ha = N_HEADS_A
    return pl.pallas_call(
        _attn_a_kernel,
        grid=(ha, b),
        in_specs=[pl.BlockSpec((None, 1, REL_ROW), lambda h, bi: (h, 0, 0)),
                  pl.BlockSpec((None, s, HEAD_DIM), lambda h, bi: (bi, 0, h)),
                  pl.BlockSpec((None, s, HEAD_DIM), lambda h, bi: (bi, 0, ha + h)),
                  pl.BlockSpec((None, s, HEAD_DIM), lambda h, bi: (bi, 0, 2 * ha + h))],
        out_specs=pl.BlockSpec((None, s, HEAD_DIM), lambda h, bi: (bi, 0, h)),
        out_shape=jax.ShapeDtypeStruct((b, s, D_A), BF16),
        scratch_shapes=[pltpu.VMEM((QB_A, BAND_A), F32)],
        compiler_params=_cparams(2),
        name="attn_band",
    )(bias_rows, z, z, z)


def _fox_prep_kernel(u_ref, wft_ref, bf_ref, f_ref, ft_ref):
    seq, d = u_ref.shape
    wft = jnp.concatenate([wft_ref[...], jnp.zeros((LANES - N_HEADS_B, d), F32)], axis=0).astype(BF16)
    fl = lax.dot_general(u_ref[...], wft, (((1,), (1,)), ((), ())), preferred_element_type=F32)
    fl = fl + bf_ref[...]
    x = jnp.minimum(fl, 0.0) - jnp.log1p(jnp.exp(-jnp.abs(fl)))
    row = lax.broadcasted_iota(jnp.int32, x.shape, 0)
    shift = 1
    while shift < seq:
        x = x + jnp.where(row >= shift, pltpu.roll(x, shift, axis=0), 0.0)
        shift *= 2
    f_ref[...] = x
    ft_ref[...] = jnp.transpose(x)[:N_HEADS_B, :]


def _fox_prep(u3, w_in_t, b_f):
    b, s, d = u3.shape
    return pl.pallas_call(
        _fox_prep_kernel,
        grid=(b,),
        in_specs=[pl.BlockSpec((None, s, d), lambda bi: (bi, 0, 0)),
                  pl.BlockSpec((N_HEADS_B, d), lambda bi: (D_QKV // N_HEADS_B, 0)),
                  pl.BlockSpec((1, LANES), lambda bi: (0, 0))],
        out_specs=[pl.BlockSpec((None, s, LANES), lambda bi: (bi, 0, 0)),
                   pl.BlockSpec((None, N_HEADS_B, s), lambda bi: (bi, 0, 0))],
        out_shape=[jax.ShapeDtypeStruct((b, s, LANES), F32),
                   jax.ShapeDtypeStruct((b, N_HEADS_B, s), F32)],
        compiler_params=_cparams(1),
        name="fox_prep",
    )(u3, w_in_t, b_f)


def _fox_kernel(q_ref, k_ref, v_ref, fq_ref, fk_ref, o_ref):
    seq = q_ref.shape[0]
    h = pl.program_id(1)
    lane = lax.broadcasted_iota(jnp.int32, (TQ_B, LANES), 1)
    row = lax.broadcasted_iota(jnp.int32, (TQ_B, TQ_B), 0)
    col = lax.broadcasted_iota(jnp.int32, (TQ_B, TQ_B), 1)
    nt = (((1,), (1,)), ((), ()))

    n_blocks = seq // TQ_B

    def scores(qi):
        q0, q1 = qi * TQ_B, (qi + 1) * TQ_B
        q = q_ref[q0:q1, :]
        s_diag = lax.dot_general(q, k_ref[q0:q1, :], nt, preferred_element_type=F32)
        s_past = lax.dot_general(q, k_ref[0:q0, :], nt, preferred_element_type=F32) if qi > 0 else None
        return s_diag, s_past

    s_next = scores(0)
    for qi in range(n_blocks):
        q0, q1 = qi * TQ_B, (qi + 1) * TQ_B
        s_d, s_p = s_next
        if qi + 1 < n_blocks:
            s_next = scores(qi + 1)
        fq = jnp.sum(jnp.where(lane == h, fq_ref[q0:q1, :], 0.0), axis=1, keepdims=True)
        s_d = (s_d * SCALE + fq) - fk_ref[pl.ds(h, 1), q0:q1]
        s_d = jnp.where(row >= col, s_d, NEG_INF)
        mx = jnp.max(s_d, axis=-1, keepdims=True)
        if qi > 0:
            s_p = (s_p * SCALE + fq) - fk_ref[pl.ds(h, 1), 0:q0]
            mx = jnp.maximum(mx, jnp.max(s_p, axis=-1, keepdims=True))
        e_d = jnp.exp(s_d - mx)
        den = jnp.sum(e_d, axis=-1, keepdims=True)
        acc = _dot(e_d.astype(BF16), v_ref[q0:q1, :])
        if qi > 0:
            e_p = jnp.exp(s_p - mx)
            den = den + jnp.sum(e_p, axis=-1, keepdims=True)
            acc = acc + _dot(e_p.astype(BF16), v_ref[0:q0, :])
        o_ref[q0:q1, :] = (acc / den).astype(o_ref.dtype)


def _fox(z, f_col, f_row):
    b, s, _ = z.shape
    hb = N_HEADS_B
    base = 3 * N_HEADS_A
    return pl.pallas_call(
        _fox_kernel,
        grid=(b, hb),
        in_specs=[pl.BlockSpec((None, s, HEAD_DIM), lambda bi, h: (bi, 0, base + h)),
                  pl.BlockSpec((None, s, HEAD_DIM), lambda bi, h: (bi, 0, base + hb + h)),
                  pl.BlockSpec((None, s, HEAD_DIM), lambda bi, h: (bi, 0, base + 2 * hb + h)),
                  pl.BlockSpec((None, s, LANES), lambda bi, h: (bi, 0, 0)),
                  pl.BlockSpec((None, hb, s), lambda bi, h: (bi, 0, 0))],
        out_specs=pl.BlockSpec((None, s, HEAD_DIM), lambda bi, h: (bi, 0, h)),
        out_shape=jax.ShapeDtypeStruct((b, s, D_B), BF16),
        compiler_params=_cparams(2),
        name="attn_fox",
    )(z, z, z, f_col, f_row)


def _rel_bias_rows(rel_bias):
    far = rel_bias[:, 2 * REL_CLIP:]
    n_head = PAD_A - REL_CLIP
    n_tail = REL_ROW - n_head - (2 * REL_CLIP + 1)
    rows = jnp.concatenate([jnp.tile(far, (1, n_head)), jnp.flip(rel_bias, axis=1),
                            jnp.tile(far, (1, n_tail))], axis=1)
    return rows[:, None, :]


def kernel(x, p, g_ffn1, w_ffn1_gu, w_ffn1_down, g_mix, w_in, b_forget, rel_bias, w_out,
           g_ffn2, w_ffn2_gu, w_ffn2_down, g_ple, w_ple_gate, w_ple_proj, g_final):
    b, s, d = x.shape
    m = b * s
    depth = p.shape[0]
    h = x.reshape(m, d)
    for i in range(depth):
        xn = _rmsnorm(h, g_ffn1[i], BF16)
        hid = _gateup(xn, w_ffn1_gu[i])
        h = _resid_matmul(hid, w_ffn1_down[i], h, 0.5, 512, 512, "ffn_down")
        u = _rmsnorm(h, g_mix[i], BF16)
        w_in_t = jnp.swapaxes(w_in[i], 0, 1)
        z = _proj(u, w_in_t, D_QKV).reshape(b, s, D_QKV)
        b_f = jnp.pad(b_forget[i], (0, LANES - N_HEADS_B)).reshape(1, LANES)
        f_col, f_row = _fox_prep(u.reshape(b, s, d), w_in_t, b_f)
        o_a = _attn_a(z, _rel_bias_rows(rel_bias[i]))
        o_b = _fox(z, f_col, f_row)
        h = _outproj(o_a.reshape(m, D_A), o_b.reshape(m, D_B), w_out[i], h)
        xn = _rmsnorm(h, g_ffn2[i], BF16)
        hid = _gateup(xn, w_ffn2_gu[i])
        h = _resid_matmul(hid, w_ffn2_down[i], h, 0.5, 512, 512, "ffn_down")
        hn = _rmsnorm(h, g_ple[i], BF16)
        h = _ple(hn, p[i].reshape(m, D_PLE), w_ple_gate[i], w_ple_proj[i], h)
    return _rmsnorm(h, g_final, F32).reshape(b, s, d)
```

```python
import functools

import jax
import jax.numpy as jnp
from jax import lax
from jax.experimental import pallas as pl
from jax.experimental.pallas import tpu as pltpu

F32 = jnp.float32
BF16 = jnp.bfloat16

D_MODEL = 2048
CHUNK = 64
N_LEFT_CHUNKS = 8
HEAD_DIM = 128
N_HEADS_A = 8
N_HEADS_B = 8
D_A = N_HEADS_A * HEAD_DIM
D_B = N_HEADS_B * HEAD_DIM
REL_CLIP = 256
D_FF = 5632
D_PLE = 256
EPS = 1e-6
NEG_INF = -1e30
D_QKV = 3 * D_A + 3 * D_B
SCALE = HEAD_DIM ** -0.5

LANES = 128
VMEM_LIMIT = 56 * 1024 * 1024

QB_A = 2 * CHUNK
BAND_A = (N_LEFT_CHUNKS + 2) * CHUNK
PAD_A = N_LEFT_CHUNKS * CHUNK
REL_ROW = 1024

TQ_B = 256


def _cparams(n_axes):
    return pltpu.CompilerParams(
        dimension_semantics=("arbitrary",) * n_axes,
        vmem_limit_bytes=VMEM_LIMIT,
    )


def _rmsnorm_kernel(x_ref, g_ref, o_ref):
    x = x_ref[...]
    ms = jnp.mean(x * x, axis=-1, keepdims=True)
    o_ref[...] = ((x * lax.rsqrt(ms + EPS)) * g_ref[...]).astype(o_ref.dtype)


def _rmsnorm(x, g, out_dtype, tm=512):
    m, d = x.shape
    return pl.pallas_call(
        _rmsnorm_kernel,
        grid=(m // tm,),
        in_specs=[pl.BlockSpec((tm, d), lambda i: (i, 0)),
                  pl.BlockSpec((1, d), lambda i: (0, 0))],
        out_specs=pl.BlockSpec((tm, d), lambda i: (i, 0)),
        out_shape=jax.ShapeDtypeStruct((m, d), out_dtype),
        compiler_params=_cparams(1),
        name="rmsnorm",
    )(x, g.reshape(1, d))


def _dot(a, b):
    return jnp.dot(a, b, preferred_element_type=F32)


def _gateup_kernel(a_ref, wg_ref, wu_ref, o_ref, wg_s, wu_s):
    @pl.when(pl.program_id(1) == 0)
    def _():
        wg_s[...] = wg_ref[...].astype(BF16)
        wu_s[...] = wu_ref[...].astype(BF16)

    a = a_ref[...]
    g = _dot(a, wg_s[...])
    u = _dot(a, wu_s[...])
    o_ref[...] = ((g * jax.nn.sigmoid(g)) * u).astype(o_ref.dtype)


def _gateup(a, w_gu, tm=2048, tn=512):
    m, k = a.shape
    nj = D_FF // tn
    return pl.pallas_call(
        _gateup_kernel,
        grid=(nj, m // tm),
        in_specs=[pl.BlockSpec((tm, k), lambda j, i: (i, 0)),
                  pl.BlockSpec((k, tn), lambda j, i: (0, j)),
                  pl.BlockSpec((k, tn), lambda j, i: (0, j + nj))],
        out_specs=pl.BlockSpec((tm, tn), lambda j, i: (i, j)),
        out_shape=jax.ShapeDtypeStruct((m, D_FF), BF16),
        scratch_shapes=[pltpu.VMEM((k, tn), BF16), pltpu.VMEM((k, tn), BF16)],
        compiler_params=_cparams(2),
        name="ffn_gateup",
    )(a, w_gu, w_gu)


def _proj_kernel(a_ref, wt_ref, o_ref, w_s):
    @pl.when(pl.program_id(1) == 0)
    def _():
        w_s[...] = jnp.transpose(wt_ref[...]).astype(BF16)

    o_ref[...] = _dot(a_ref[...], w_s[...]).astype(o_ref.dtype)


def _proj(a, w_t, n_out, tm=2048, tn=1024):
    m, k = a.shape
    return pl.pallas_call(
        _proj_kernel,
        grid=(n_out // tn, m // tm),
        in_specs=[pl.BlockSpec((tm, k), lambda j, i: (i, 0)),
                  pl.BlockSpec((tn, k), lambda j, i: (j, 0))],
        out_specs=pl.BlockSpec((tm, tn), lambda j, i: (i, j)),
        out_shape=jax.ShapeDtypeStruct((m, n_out), BF16),
        scratch_shapes=[pltpu.VMEM((k, tn), BF16)],
        compiler_params=_cparams(2),
        name="in_proj",
    )(a, w_t)


def _resid_kernel(scale, a_ref, w_ref, r_ref, o_ref, w_s):
    @pl.when(pl.program_id(1) == 0)
    def _():
        w_s[...] = w_ref[...].astype(BF16)

    o_ref[...] = r_ref[...] + scale * _dot(a_ref[...], w_s[...])


def _resid_matmul(a, w, resid, scale, tm, tn, name):
    m, k = a.shape
    n = w.shape[1]
    return pl.pallas_call(
        functools.partial(_resid_kernel, scale),
        grid=(n // tn, m // tm),
        in_specs=[pl.BlockSpec((tm, k), lambda j, i: (i, 0)),
                  pl.BlockSpec((k, tn), lambda j, i: (0, j)),
                  pl.BlockSpec((tm, tn), lambda j, i: (i, j))],
        out_specs=pl.BlockSpec((tm, tn), lambda j, i: (i, j)),
        out_shape=jax.ShapeDtypeStruct((m, n), F32),
        scratch_shapes=[pltpu.VMEM((k, tn), BF16)],
        compiler_params=_cparams(2),
        name=name,
    )(a, w, resid)


def _rms(x, g):
    ms = jnp.mean(x * x, axis=-1, keepdims=True)
    return (x * lax.rsqrt(ms + EPS)) * g


def _outproj_kernel(oa_ref, ob_ref, w_ref, r_ref, g_ref, h_ref, n_ref, w_s):
    @pl.when(pl.program_id(0) == 0)
    def _():
        w_s[...] = w_ref[...].astype(BF16)

    acc = _dot(oa_ref[...], w_s[:D_A, :]) + _dot(ob_ref[...], w_s[D_A:, :])
    h = r_ref[...] + acc
    h_ref[...] = h
    n_ref[...] = _rms(h, g_ref[...]).astype(n_ref.dtype)


def _outproj(o_a, o_b, w_out, resid, g_next, tm=512):
    m = o_a.shape[0]
    k, n = w_out.shape
    return pl.pallas_call(
        _outproj_kernel,
        grid=(m // tm,),
        in_specs=[pl.BlockSpec((tm, D_A), lambda i: (i, 0)),
                  pl.BlockSpec((tm, D_B), lambda i: (i, 0)),
                  pl.BlockSpec((k, n), lambda i: (0, 0), pipeline_mode=pl.Buffered(1)),
                  pl.BlockSpec((tm, n), lambda i: (i, 0)),
                  pl.BlockSpec((1, n), lambda i: (0, 0))],
        out_specs=[pl.BlockSpec((tm, n), lambda i: (i, 0)),
                   pl.BlockSpec((tm, n), lambda i: (i, 0))],
        out_shape=[jax.ShapeDtypeStruct((m, n), F32), jax.ShapeDtypeStruct((m, n), BF16)],
        scratch_shapes=[pltpu.VMEM((k, n), BF16)],
        compiler_params=_cparams(1),
        name="out_proj",
    )(o_a, o_b, w_out, resid, g_next.reshape(1, n))


def _ple_kernel(close_block, a_ref, p_ref, wg_ref, wp_ref, r_ref, g_ref, o_ref, wg_s, wp_s):
    @pl.when(pl.program_id(0) == 0)
    def _():
        wg_s[...] = wg_ref[...].astype(BF16)
        wp_s[...] = wp_ref[...].astype(BF16)

    gate = jax.nn.sigmoid(_dot(a_ref[...], wg_s[...]))
    emb = _dot(p_ref[...].astype(BF16), wp_s[...])
    h = r_ref[...] + gate * emb
    o_ref[...] = _rms(h, g_ref[...]) if close_block else h


def _ple(a, p, w_gate, w_proj, resid, g_final, close_block, tm=512):
    m, k = a.shape
    n = w_gate.shape[1]
    return pl.pallas_call(
        functools.partial(_ple_kernel, close_block),
        grid=(m // tm,),
        in_specs=[pl.BlockSpec((tm, k), lambda i: (i, 0)),
                  pl.BlockSpec((tm, D_PLE), lambda i: (i, 0)),
                  pl.BlockSpec((k, n), lambda i: (0, 0), pipeline_mode=pl.Buffered(1)),
                  pl.BlockSpec((D_PLE, n), lambda i: (0, 0), pipeline_mode=pl.Buffered(1)),
                  pl.BlockSpec((tm, n), lambda i: (i, 0)),
                  pl.BlockSpec((1, n), lambda i: (0, 0))],
        out_specs=pl.BlockSpec((tm, n), lambda i: (i, 0)),
        out_shape=jax.ShapeDtypeStruct((m, n), F32),
        scratch_shapes=[pltpu.VMEM((k, n), BF16), pltpu.VMEM((D_PLE, n), BF16)],
        compiler_params=_cparams(1),
        name="ple",
    )(a, p, w_gate, w_proj, resid, g_final.reshape(1, n))


def _attn_a_scores(q, k):
    return lax.dot_general(q, k, (((1,), (1,)), ((), ())), preferred_element_type=F32)


def _attn_a_softmax_pv(s, v, bias, key_lo):
    nk = v.shape[0]
    s = s * SCALE + bias
    row = lax.broadcasted_iota(jnp.int32, (QB_A, nk), 0)
    col = lax.broadcasted_iota(jnp.int32, (QB_A, nk), 1) + key_lo
    first = row < CHUNK
    valid = (first & (col < BAND_A - CHUNK)) | (jnp.logical_not(first) & (col >= CHUNK))
    s = jnp.where(valid, s, NEG_INF)
    mx = jnp.max(s, axis=-1, keepdims=True)
    e = jnp.exp(s - mx)
    den = jnp.sum(e, axis=-1, keepdims=True)
    o = _dot(e.astype(BF16), v)
    return o / den


def _attn_a_kernel(g_ref, q_ref, k_ref, v_ref, o_ref, bias_s):
    @pl.when(pl.program_id(1) == 0)
    def _():
        g = jnp.broadcast_to(g_ref[...], (QB_A, REL_ROW))
        bias_s[...] = pltpu.roll(g, 0, axis=1, stride=1, stride_axis=0)[:, :BAND_A]

    n_blocks = q_ref.shape[0] // QB_A

    def window(m):
        k0 = max(m * QB_A - PAD_A, 0)
        return k0, (m + 1) * QB_A

    def scores(m):
        k0, k1 = window(m)
        return _attn_a_scores(q_ref[m * QB_A:k1, :], k_ref[k0:k1, :])

    s_next = scores(0)
    for m in range(n_blocks):
        s = s_next
        if m + 1 < n_blocks:
            s_next = scores(m + 1)
        k0, k1 = window(m)
        lo = BAND_A - (k1 - k0)
        o = _attn_a_softmax_pv(s, v_ref[k0:k1, :], bias_s[:, lo:], lo)
        o_ref[m * QB_A:k1, :] = o.astype(o_ref.dtype)


def _attn_a(z, bias_rows):
    b, s, _ = z.shape
    ha = N_HEADS_A
    return pl.pallas_call(
        _attn_a_kernel,
        grid=(ha, b),
        in_specs=[pl.BlockSpec((None, 1, REL_ROW), lambda h, bi: (h, 0, 0)),
                  pl.BlockSpec((None, s, HEAD_DIM), lambda h, bi: (bi, 0, h)),
                  pl.BlockSpec((None, s, HEAD_DIM), lambda h, bi: (bi, 0, ha + h)),
                  pl.BlockSpec((None, s, HEAD_DIM), lambda h, bi: (bi, 0, 2 * ha + h))],
        out_specs=pl.BlockSpec((None, s, HEAD_DIM), lambda h, bi: (bi, 0, h)),
        out_shape=jax.ShapeDtypeStruct((b, s, D_A), BF16),
        scratch_shapes=[pltpu.VMEM((QB_A, BAND_A), F32)],
        compiler_params=_cparams(2),
        name="attn_band",
    )(bias_rows, z, z, z)


def _fox_prep_kernel(u_ref, wft_ref, bf_ref, f_ref, ft_ref):
    seq, d = u_ref.shape
    wft = jnp.concatenate([wft_ref[...], jnp.zeros((LANES - N_HEADS_B, d), F32)], axis=0).astype(BF16)
    fl = lax.dot_general(u_ref[...], wft, (((1,), (1,)), ((), ())), preferred_element_type=F32)
    fl = fl + bf_ref[...]
    x = jnp.minimum(fl, 0.0) - jnp.log1p(jnp.exp(-jnp.abs(fl)))
    row = lax.broadcasted_iota(jnp.int32, x.shape, 0)
    shift = 1
    while shift < seq:
        x = x + jnp.where(row >= shift, pltpu.roll(x, shift, axis=0), 0.0)
        shift *= 2
    f_ref[...] = x
    ft_ref[...] = jnp.transpose(x)[:N_HEADS_B, :]


def _fox_prep(u3, w_in_t, b_f):
    b, s, d = u3.shape
    return pl.pallas_call(
        _fox_prep_kernel,
        grid=(b,),
        in_specs=[pl.BlockSpec((None, s, d), lambda bi: (bi, 0, 0)),
                  pl.BlockSpec((N_HEADS_B, d), lambda bi: (D_QKV // N_HEADS_B, 0)),
                  pl.BlockSpec((1, LANES), lambda bi: (0, 0))],
        out_specs=[pl.BlockSpec((None, s, LANES), lambda bi: (bi, 0, 0)),
                   pl.BlockSpec((None, N_HEADS_B, s), lambda bi: (bi, 0, 0))],
        out_shape=[jax.ShapeDtypeStruct((b, s, LANES), F32),
                   jax.ShapeDtypeStruct((b, N_HEADS_B, s), F32)],
        compiler_params=_cparams(1),
        name="fox_prep",
    )(u3, w_in_t, b_f)


def _fox_kernel(q_ref, k_ref, v_ref, fq_ref, fk_ref, o_ref):
    seq = q_ref.shape[0]
    h = pl.program_id(1)
    lane = lax.broadcasted_iota(jnp.int32, (TQ_B, LANES), 1)
    row = lax.broadcasted_iota(jnp.int32, (TQ_B, TQ_B), 0)
    col = lax.broadcasted_iota(jnp.int32, (TQ_B, TQ_B), 1)
    nt = (((1,), (1,)), ((), ()))

    n_blocks = seq // TQ_B

    def scores(qi):
        q0, q1 = qi * TQ_B, (qi + 1) * TQ_B
        q = q_ref[q0:q1, :]
        s_diag = lax.dot_general(q, k_ref[q0:q1, :], nt, preferred_element_type=F32)
        s_past = lax.dot_general(q, k_ref[0:q0, :], nt, preferred_element_type=F32) if qi > 0 else None
        return s_diag, s_past

    s_next = scores(0)
    for qi in range(n_blocks):
        q0, q1 = qi * TQ_B, (qi + 1) * TQ_B
        s_d, s_p = s_next
        if qi + 1 < n_blocks:
            s_next = scores(qi + 1)
        fq = jnp.sum(jnp.where(lane == h, fq_ref[q0:q1, :], 0.0), axis=1, keepdims=True)
        s_d = (s_d * SCALE + fq) - fk_ref[pl.ds(h, 1), q0:q1]
        s_d = jnp.where(row >= col, s_d, NEG_INF)
        mx = jnp.max(s_d, axis=-1, keepdims=True)
        if qi > 0:
            s_p = (s_p * SCALE + fq) - fk_ref[pl.ds(h, 1), 0:q0]
            mx = jnp.maximum(mx, jnp.max(s_p, axis=-1, keepdims=True))
        e_d = jnp.exp(s_d - mx)
        den = jnp.sum(e_d, axis=-1, keepdims=True)
        acc = _dot(e_d.astype(BF16), v_ref[q0:q1, :])
        if qi > 0:
            e_p = jnp.exp(s_p - mx)
            den = den + jnp.sum(e_p, axis=-1, keepdims=True)
            acc = acc + _dot(e_p.astype(BF16), v_ref[0:q0, :])
        o_ref[q0:q1, :] = (acc / den).astype(o_ref.dtype)


def _fox(z, f_col, f_row):
    b, s, _ = z.shape
    hb = N_HEADS_B
    base = 3 * N_HEADS_A
    return pl.pallas_call(
        _fox_kernel,
        grid=(b, hb),
        in_specs=[pl.BlockSpec((None, s, HEAD_DIM), lambda bi, h: (bi, 0, base + h)),
                  pl.BlockSpec((None, s, HEAD_DIM), lambda bi, h: (bi, 0, base + hb + h)),
                  pl.BlockSpec((None, s, HEAD_DIM), lambda bi, h: (bi, 0, base + 2 * hb + h)),
                  pl.BlockSpec((None, s, LANES), lambda bi, h: (bi, 0, 0)),
                  pl.BlockSpec((None, hb, s), lambda bi, h: (bi, 0, 0))],
        out_specs=pl.BlockSpec((None, s, HEAD_DIM), lambda bi, h: (bi, 0, h)),
        out_shape=jax.ShapeDtypeStruct((b, s, D_B), BF16),
        compiler_params=_cparams(2),
        name="attn_fox",
    )(z, z, z, f_col, f_row)


def _rel_bias_rows(rel_bias):
    far = rel_bias[:, 2 * REL_CLIP:]
    n_head = PAD_A - REL_CLIP
    n_tail = REL_ROW - n_head - (2 * REL_CLIP + 1)
    rows = jnp.concatenate([jnp.tile(far, (1, n_head)), jnp.flip(rel_bias, axis=1),
                            jnp.tile(far, (1, n_tail))], axis=1)
    return rows[:, None, :]


def kernel(x, p, g_ffn1, w_ffn1_gu, w_ffn1_down, g_mix, w_in, b_forget, rel_bias, w_out,
           g_ffn2, w_ffn2_gu, w_ffn2_down, g_ple, w_ple_gate, w_ple_proj, g_final):
    b, s, d = x.shape
    m = b * s
    depth = p.shape[0]
    h = x.reshape(m, d)
    for i in range(depth):
        xn = _rmsnorm(h, g_ffn1[i], BF16)
        hid = _gateup(xn, w_ffn1_gu[i])
        h = _resid_matmul(hid, w_ffn1_down[i], h, 0.5, 512, 512, "ffn_down")
        u = _rmsnorm(h, g_mix[i], BF16)
        w_in_t = jnp.swapaxes(w_in[i], 0, 1)
        z = _proj(u, w_in_t, D_QKV).reshape(b, s, D_QKV)
        b_f = jnp.pad(b_forget[i], (0, LANES - N_HEADS_B)).reshape(1, LANES)
        f_col, f_row = _fox_prep(u.reshape(b, s, d), w_in_t, b_f)
        o_a = _attn_a(z, _rel_bias_rows(rel_bias[i]))
        o_b = _fox(z, f_col, f_row)
        h, xn = _outproj(o_a.reshape(m, D_A), o_b.reshape(m, D_B), w_out[i], h, g_ffn2[i])
        hid = _gateup(xn, w_ffn2_gu[i])
        h = _resid_matmul(hid, w_ffn2_down[i], h, 0.5, 512, 512, "ffn_down")
        hn = _rmsnorm(h, g_ple[i], BF16)
        h = _ple(hn, p[i].reshape(m, D_PLE), w_ple_gate[i], w_ple_proj[i], h, g_final, i == depth - 1)
    return h.reshape(b, s, d)
```

```python
import functools

import jax
import jax.numpy as jnp
from jax import lax
from jax.experimental import pallas as pl
from jax.experimental.pallas import tpu as pltpu

F32 = jnp.float32
BF16 = jnp.bfloat16

D_MODEL = 2048
CHUNK = 64
N_LEFT_CHUNKS = 8
HEAD_DIM = 128
N_HEADS_A = 8
N_HEADS_B = 8
D_A = N_HEADS_A * HEAD_DIM
D_B = N_HEADS_B * HEAD_DIM
REL_CLIP = 256
D_FF = 5632
D_PLE = 256
EPS = 1e-6
NEG_INF = -1e30
D_QKV = 3 * D_A + 3 * D_B
SCALE = HEAD_DIM ** -0.5
LOG2E = 1.4426950408889634

LANES = 128
VMEM_LIMIT = 56 * 1024 * 1024

QB_A = 2 * CHUNK
BAND_A = (N_LEFT_CHUNKS + 2) * CHUNK
PAD_A = N_LEFT_CHUNKS * CHUNK
REL_ROW = 1024

TQ_B = 256


def _cparams(n_axes):
    return pltpu.CompilerParams(
        dimension_semantics=("arbitrary",) * n_axes,
        vmem_limit_bytes=VMEM_LIMIT,
    )


def _rms(x, g):
    ms = jnp.mean(x * x, axis=-1, keepdims=True)
    return (x * lax.rsqrt(ms + EPS)) * g


def _rmsnorm_kernel(x_ref, g_ref, o_ref):
    o_ref[...] = _rms(x_ref[...], g_ref[...]).astype(o_ref.dtype)


def _rmsnorm(x, g, out_dtype, tm=512):
    m, d = x.shape
    return pl.pallas_call(
        _rmsnorm_kernel,
        grid=(m // tm,),
        in_specs=[pl.BlockSpec((tm, d), lambda i: (i, 0)),
                  pl.BlockSpec((1, d), lambda i: (0, 0))],
        out_specs=pl.BlockSpec((tm, d), lambda i: (i, 0)),
        out_shape=jax.ShapeDtypeStruct((m, d), out_dtype),
        compiler_params=_cparams(1),
        name="rmsnorm",
    )(x, g.reshape(1, d))


def _dot(a, b):
    return jnp.dot(a, b, preferred_element_type=F32)


def _gateup_kernel(a_ref, wg_ref, wu_ref, wd_ref, o_ref, wd_o_ref, wg_s, wu_s):
    @pl.when(pl.program_id(1) == 0)
    def _():
        wg_s[...] = wg_ref[...].astype(BF16)
        wu_s[...] = wu_ref[...].astype(BF16)
        wd_o_ref[...] = wd_ref[...].astype(BF16)

    a = a_ref[...]
    g = _dot(a, wg_s[...])
    u = _dot(a, wu_s[...])
    o_ref[...] = ((g * jax.nn.sigmoid(g)) * u).astype(o_ref.dtype)


def _gateup(a, w_gu, w_down, tm=1024, tn=512):
    m, k = a.shape
    nj = D_FF // tn
    n = w_down.shape[1]
    return pl.pallas_call(
        _gateup_kernel,
        grid=(nj, m // tm),
        in_specs=[pl.BlockSpec((tm, k), lambda j, i: (i, 0)),
                  pl.BlockSpec((k, tn), lambda j, i: (0, j)),
                  pl.BlockSpec((k, tn), lambda j, i: (0, j + nj)),
                  pl.BlockSpec((tn, n), lambda j, i: (j, 0))],
        out_specs=[pl.BlockSpec((tm, tn), lambda j, i: (i, j)),
                   pl.BlockSpec((tn, n), lambda j, i: (j, 0))],
        out_shape=[jax.ShapeDtypeStruct((m, D_FF), BF16),
                   jax.ShapeDtypeStruct((D_FF, n), BF16)],
        scratch_shapes=[pltpu.VMEM((k, tn), BF16), pltpu.VMEM((k, tn), BF16)],
        compiler_params=_cparams(2),
        name="ffn_gateup",
    )(a, w_gu, w_gu, w_down)


def _proj_kernel(a_ref, wt_ref, o_ref, w_s):
    @pl.when(pl.program_id(1) == 0)
    def _():
        w_s[...] = jnp.transpose(wt_ref[...]).astype(BF16)

    o_ref[...] = _dot(a_ref[...], w_s[...]).astype(o_ref.dtype)


def _proj(a, w_t, n_out, tm=2048, tn=1024):
    m, k = a.shape
    return pl.pallas_call(
        _proj_kernel,
        grid=(n_out // tn, m // tm),
        in_specs=[pl.BlockSpec((tm, k), lambda j, i: (i, 0)),
                  pl.BlockSpec((tn, k), lambda j, i: (j, 0))],
        out_specs=pl.BlockSpec((tm, tn), lambda j, i: (i, j)),
        out_shape=jax.ShapeDtypeStruct((m, n_out), BF16),
        scratch_shapes=[pltpu.VMEM((k, tn), BF16)],
        compiler_params=_cparams(2),
        name="in_proj",
    )(a, w_t)


def _down_kernel(a_ref, w_ref, r_ref, g_ref, h_ref, n_ref):
    h = r_ref[...] + 0.5 * _dot(a_ref[...], w_ref[...])
    h_ref[...] = h
    n_ref[...] = _rms(h, g_ref[...]).astype(n_ref.dtype)


def _down(a, w_bf16, resid, g_next, tm=256):
    m, k = a.shape
    n = w_bf16.shape[1]
    return pl.pallas_call(
        _down_kernel,
        grid=(m // tm,),
        in_specs=[pl.BlockSpec((tm, k), lambda i: (i, 0)),
                  pl.BlockSpec((k, n), lambda i: (0, 0), pipeline_mode=pl.Buffered(1)),
                  pl.BlockSpec((tm, n), lambda i: (i, 0)),
                  pl.BlockSpec((1, n), lambda i: (0, 0))],
        out_specs=[pl.BlockSpec((tm, n), lambda i: (i, 0)),
                   pl.BlockSpec((tm, n), lambda i: (i, 0))],
        out_shape=[jax.ShapeDtypeStruct((m, n), F32), jax.ShapeDtypeStruct((m, n), BF16)],
        compiler_params=_cparams(1),
        name="ffn_down",
    )(a, w_bf16, resid, g_next.reshape(1, n))


def _outproj_kernel(oa_ref, ob_ref, w_ref, r_ref, g_ref, h_ref, n_ref, w_s):
    @pl.when(pl.program_id(0) == 0)
    def _():
        w_s[...] = w_ref[...].astype(BF16)

    acc = _dot(oa_ref[...], w_s[:D_A, :]) + _dot(ob_ref[...], w_s[D_A:, :])
    h = r_ref[...] + acc
    h_ref[...] = h
    n_ref[...] = _rms(h, g_ref[...]).astype(n_ref.dtype)


def _outproj(o_a, o_b, w_out, resid, g_next, tm=512):
    m = o_a.shape[0]
    k, n = w_out.shape
    return pl.pallas_call(
        _outproj_kernel,
        grid=(m // tm,),
        in_specs=[pl.BlockSpec((tm, D_A), lambda i: (i, 0)),
                  pl.BlockSpec((tm, D_B), lambda i: (i, 0)),
                  pl.BlockSpec((k, n), lambda i: (0, 0), pipeline_mode=pl.Buffered(1)),
                  pl.BlockSpec((tm, n), lambda i: (i, 0)),
                  pl.BlockSpec((1, n), lambda i: (0, 0))],
        out_specs=[pl.BlockSpec((tm, n), lambda i: (i, 0)),
                   pl.BlockSpec((tm, n), lambda i: (i, 0))],
        out_shape=[jax.ShapeDtypeStruct((m, n), F32), jax.ShapeDtypeStruct((m, n), BF16)],
        scratch_shapes=[pltpu.VMEM((k, n), BF16)],
        compiler_params=_cparams(1),
        name="out_proj",
    )(o_a, o_b, w_out, resid, g_next.reshape(1, n))


def _ple_kernel(close_block, a_ref, p_ref, wg_ref, wp_ref, r_ref, g_ref, o_ref, wg_s, wp_s):
    @pl.when(pl.program_id(0) == 0)
    def _():
        wg_s[...] = wg_ref[...].astype(BF16)
        wp_s[...] = wp_ref[...].astype(BF16)

    gate = jax.nn.sigmoid(_dot(a_ref[...], wg_s[...]))
    emb = _dot(p_ref[...].astype(BF16), wp_s[...])
    h = r_ref[...] + gate * emb
    o_ref[...] = _rms(h, g_ref[...]) if close_block else h


def _ple(a, p, w_gate, w_proj, resid, g_final, close_block, tm=512):
    m, k = a.shape
    n = w_gate.shape[1]
    return pl.pallas_call(
        functools.partial(_ple_kernel, close_block),
        grid=(m // tm,),
        in_specs=[pl.BlockSpec((tm, k), lambda i: (i, 0)),
                  pl.BlockSpec((tm, D_PLE), lambda i: (i, 0)),
                  pl.BlockSpec((k, n), lambda i: (0, 0), pipeline_mode=pl.Buffered(1)),
                  pl.BlockSpec((D_PLE, n), lambda i: (0, 0), pipeline_mode=pl.Buffered(1)),
                  pl.BlockSpec((tm, n), lambda i: (i, 0)),
                  pl.BlockSpec((1, n), lambda i: (0, 0))],
        out_specs=pl.BlockSpec((tm, n), lambda i: (i, 0)),
        out_shape=jax.ShapeDtypeStruct((m, n), F32),
        scratch_shapes=[pltpu.VMEM((k, n), BF16), pltpu.VMEM((D_PLE, n), BF16)],
        compiler_params=_cparams(1),
        name="ple",
    )(a, p, w_gate, w_proj, resid, g_final.reshape(1, n))


def _attn_a_scores(q, k):
    return lax.dot_general(q, k, (((1,), (1,)), ((), ())), preferred_element_type=F32)


def _attn_a_softmax_pv(s, v, bias, key_lo):
    nk = v.shape[0]
    s = s * (SCALE * LOG2E) + bias
    row = lax.broadcasted_iota(jnp.int32, (QB_A, nk), 0)
    col = lax.broadcasted_iota(jnp.int32, (QB_A, nk), 1) + key_lo
    first = row < CHUNK
    valid = (first & (col < BAND_A - CHUNK)) | (jnp.logical_not(first) & (col >= CHUNK))
    s = jnp.where(valid, s, NEG_INF)
    mx = jnp.max(s, axis=-1, keepdims=True)
    e = jnp.exp2(s - mx)
    den = jnp.sum(e, axis=-1, keepdims=True)
    o = _dot(e.astype(BF16), v)
    return o / den


def _attn_a_kernel(g_ref, q_ref, k_ref, v_ref, o_ref, bias_s):
    @pl.when(pl.program_id(1) == 0)
    def _():
        g = jnp.broadcast_to(g_ref[...] * LOG2E, (QB_A, REL_ROW))
        bias_s[...] = pltpu.roll(g, 0, axis=1, stride=1, stride_axis=0)[:, :BAND_A]

    n_blocks = q_ref.shape[0] // QB_A

    def window(m):
        k0 = max(m * QB_A - PAD_A, 0)
        return k0, (m + 1) * QB_A

    def scores(m):
        k0, k1 = window(m)
        return _attn_a_scores(q_ref[m * QB_A:k1, :], k_ref[k0:k1, :])

    s_next = scores(0)
    for m in range(n_blocks):
        s = s_next
        if m + 1 < n_blocks:
            s_next = scores(m + 1)
        k0, k1 = window(m)
        lo = BAND_A - (k1 - k0)
        o = _attn_a_softmax_pv(s, v_ref[k0:k1, :], bias_s[:, lo:], lo)
        o_ref[m * QB_A:k1, :] = o.astype(o_ref.dtype)


def _attn_a(z, bias_rows):
    b, s, _ = z.shape
    ha = N_HEADS_A
    return pl.pallas_call(
        _attn_a_kernel,
        grid=(ha, b),
        in_specs=[pl.BlockSpec((None, 1, REL_ROW), lambda h, bi: (h, 0, 0)),
                  pl.BlockSpec((None, s, HEAD_DIM), lambda h, bi: (bi, 0, h)),
                  pl.BlockSpec((None, s, HEAD_DIM), lambda h, bi: (bi, 0, ha + h)),
                  pl.BlockSpec((None, s, HEAD_DIM), lambda h, bi: (bi, 0, 2 * ha + h))],
        out_specs=pl.BlockSpec((None, s, HEAD_DIM), lambda h, bi: (bi, 0, h)),
        out_shape=jax.ShapeDtypeStruct((b, s, D_A), BF16),
        scratch_shapes=[pltpu.VMEM((QB_A, BAND_A), F32)],
        compiler_params=_cparams(2),
        name="attn_band",
    )(bias_rows, z, z, z)


def _fox_prep_kernel(u_ref, wft_ref, bf_ref, f_ref, ft_ref):
    seq, d = u_ref.shape
    wft = jnp.concatenate([wft_ref[...], jnp.zeros((LANES - N_HEADS_B, d), F32)], axis=0).astype(BF16)
    fl = lax.dot_general(u_ref[...], wft, (((1,), (1,)), ((), ())), preferred_element_type=F32)
    fl = fl + bf_ref[...]
    x = jnp.minimum(fl, 0.0) - jnp.log1p(jnp.exp(-jnp.abs(fl)))
    row = lax.broadcasted_iota(jnp.int32, x.shape, 0)
    shift = 1
    while shift < seq:
        x = x + jnp.where(row >= shift, pltpu.roll(x, shift, axis=0), 0.0)
        shift *= 2
    x = x * LOG2E
    f_ref[...] = x
    ft_ref[...] = jnp.transpose(x)[:N_HEADS_B, :]


def _fox_prep(u3, w_in_t, b_f):
    b, s, d = u3.shape
    return pl.pallas_call(
        _fox_prep_kernel,
        grid=(b,),
        in_specs=[pl.BlockSpec((None, s, d), lambda bi: (bi, 0, 0)),
                  pl.BlockSpec((N_HEADS_B, d), lambda bi: (D_QKV // N_HEADS_B, 0)),
                  pl.BlockSpec((1, LANES), lambda bi: (0, 0))],
        out_specs=[pl.BlockSpec((None, s, LANES), lambda bi: (bi, 0, 0)),
                   pl.BlockSpec((None, N_HEADS_B, s), lambda bi: (bi, 0, 0))],
        out_shape=[jax.ShapeDtypeStruct((b, s, LANES), F32),
                   jax.ShapeDtypeStruct((b, N_HEADS_B, s), F32)],
        compiler_params=_cparams(1),
        name="fox_prep",
    )(u3, w_in_t, b_f)


def _fox_kernel(q_ref, k_ref, v_ref, fq_ref, fk_ref, o_ref):
    seq = q_ref.shape[0]
    h = pl.program_id(1)
    lane = lax.broadcasted_iota(jnp.int32, (TQ_B, LANES), 1)
    row = lax.broadcasted_iota(jnp.int32, (TQ_B, TQ_B), 0)
    col = lax.broadcasted_iota(jnp.int32, (TQ_B, TQ_B), 1)
    nt = (((1,), (1,)), ((), ()))

    n_blocks = seq // TQ_B

    def scores(qi):
        q0, q1 = qi * TQ_B, (qi + 1) * TQ_B
        q = q_ref[q0:q1, :]
        s_diag = lax.dot_general(q, k_ref[q0:q1, :], nt, preferred_element_type=F32)
        s_past = lax.dot_general(q, k_ref[0:q0, :], nt, preferred_element_type=F32) if qi > 0 else None
        return s_diag, s_past

    s_next = scores(0)
    for qi in range(n_blocks):
        q0, q1 = qi * TQ_B, (qi + 1) * TQ_B
        s_d, s_p = s_next
        if qi + 1 < n_blocks:
            s_next = scores(qi + 1)
        fq = jnp.sum(jnp.where(lane == h, fq_ref[q0:q1, :], 0.0), axis=1, keepdims=True)
        s_d = (s_d * (SCALE * LOG2E) + fq) - fk_ref[pl.ds(h, 1), q0:q1]
        s_d = jnp.where(row >= col, s_d, NEG_INF)
        mx = jnp.max(s_d, axis=-1, keepdims=True)
        if qi > 0:
            s_p = (s_p * (SCALE * LOG2E) + fq) - fk_ref[pl.ds(h, 1), 0:q0]
            mx = jnp.maximum(mx, jnp.max(s_p, axis=-1, keepdims=True))
        e_d = jnp.exp2(s_d - mx)
        den = jnp.sum(e_d, axis=-1, keepdims=True)
        acc = _dot(e_d.astype(BF16), v_ref[q0:q1, :])
        if qi > 0:
            e_p = jnp.exp2(s_p - mx)
            den = den + jnp.sum(e_p, axis=-1, keepdims=True)
            acc = acc + _dot(e_p.astype(BF16), v_ref[0:q0, :])
        o_ref[q0:q1, :] = (acc / den).astype(o_ref.dtype)


def _fox(z, f_col, f_row):
    b, s, _ = z.shape
    hb = N_HEADS_B
    base = 3 * N_HEADS_A
    return pl.pallas_call(
        _fox_kernel,
        grid=(b, hb),
        in_specs=[pl.BlockSpec((None, s, HEAD_DIM), lambda bi, h: (bi, 0, base + h)),
                  pl.BlockSpec((None, s, HEAD_DIM), lambda bi, h: (bi, 0, base + hb + h)),
                  pl.BlockSpec((None, s, HEAD_DIM), lambda bi, h: (bi, 0, base + 2 * hb + h)),
                  pl.BlockSpec((None, s, LANES), lambda bi, h: (bi, 0, 0)),
                  pl.BlockSpec((None, hb, s), lambda bi, h: (bi, 0, 0))],
        out_specs=pl.BlockSpec((None, s, HEAD_DIM), lambda bi, h: (bi, 0, h)),
        out_shape=jax.ShapeDtypeStruct((b, s, D_B), BF16),
        compiler_params=_cparams(2),
        name="attn_fox",
    )(z, z, z, f_col, f_row)


def _rel_bias_rows(rel_bias):
    far = rel_bias[:, 2 * REL_CLIP:]
    n_head = PAD_A - REL_CLIP
    n_tail = REL_ROW - n_head - (2 * REL_CLIP + 1)
    rows = jnp.concatenate([jnp.tile(far, (1, n_head)), jnp.flip(rel_bias, axis=1),
                            jnp.tile(far, (1, n_tail))], axis=1)
    return rows[:, None, :]


def kernel(x, p, g_ffn1, w_ffn1_gu, w_ffn1_down, g_mix, w_in, b_forget, rel_bias, w_out,
           g_ffn2, w_ffn2_gu, w_ffn2_down, g_ple, w_ple_gate, w_ple_proj, g_final):
    b, s, d = x.shape
    m = b * s
    depth = p.shape[0]
    h = x.reshape(m, d)
    for i in range(depth):
        xn = _rmsnorm(h, g_ffn1[i], BF16)
        hid, w_down = _gateup(xn, w_ffn1_gu[i], w_ffn1_down[i])
        h, u = _down(hid, w_down, h, g_mix[i])
        w_in_t = jnp.swapaxes(w_in[i], 0, 1)
        z = _proj(u, w_in_t, D_QKV).reshape(b, s, D_QKV)
        b_f = jnp.pad(b_forget[i], (0, LANES - N_HEADS_B)).reshape(1, LANES)
        f_col, f_row = _fox_prep(u.reshape(b, s, d), w_in_t, b_f)
        o_a = _attn_a(z, _rel_bias_rows(rel_bias[i]))
        o_b = _fox(z, f_col, f_row)
        h, xn = _outproj(o_a.reshape(m, D_A), o_b.reshape(m, D_B), w_out[i], h, g_ffn2[i])
        hid, w_down = _gateup(xn, w_ffn2_gu[i], w_ffn2_down[i])
        h, hn = _down(hid, w_down, h, g_ple[i])
        h = _ple(hn, p[i].reshape(m, D_PLE), w_ple_gate[i], w_ple_proj[i], h, g_final, i == depth - 1)
    return h.reshape(b, s, d)
```

```python
import functools

import jax
import jax.numpy as jnp
from jax import lax
from jax.experimental import pallas as pl
from jax.experimental.pallas import tpu as pltpu

F32 = jnp.float32
BF16 = jnp.bfloat16

D_MODEL = 2048
CHUNK = 64
N_LEFT_CHUNKS = 8
HEAD_DIM = 128
N_HEADS_A = 8
N_HEADS_B = 8
D_A = N_HEADS_A * HEAD_DIM
D_B = N_HEADS_B * HEAD_DIM
REL_CLIP = 256
D_FF = 5632
D_PLE = 256
EPS = 1e-6
NEG_INF = -1e30
D_QKV = 3 * D_A + 3 * D_B
SCALE = HEAD_DIM ** -0.5
LOG2E = 1.4426950408889634

LANES = 128
VMEM_LIMIT = 56 * 1024 * 1024

QB_A = 2 * CHUNK
BAND_A = (N_LEFT_CHUNKS + 2) * CHUNK
PAD_A = N_LEFT_CHUNKS * CHUNK
REL_ROW = 1024

TQ_B = 256


def _cparams(n_axes):
    return pltpu.CompilerParams(
        dimension_semantics=("arbitrary",) * n_axes,
        vmem_limit_bytes=VMEM_LIMIT,
    )


def _rms(x, g):
    ms = jnp.mean(x * x, axis=-1, keepdims=True)
    return (x * lax.rsqrt(ms + EPS)) * g


def _rmsnorm_kernel(x_ref, g_ref, o_ref):
    o_ref[...] = _rms(x_ref[...], g_ref[...]).astype(o_ref.dtype)


def _rmsnorm(x, g, out_dtype, tm=512):
    m, d = x.shape
    return pl.pallas_call(
        _rmsnorm_kernel,
        grid=(m // tm,),
        in_specs=[pl.BlockSpec((tm, d), lambda i: (i, 0)),
                  pl.BlockSpec((1, d), lambda i: (0, 0))],
        out_specs=pl.BlockSpec((tm, d), lambda i: (i, 0)),
        out_shape=jax.ShapeDtypeStruct((m, d), out_dtype),
        compiler_params=_cparams(1),
        name="rmsnorm",
    )(x, g.reshape(1, d))


def _dot(a, b):
    return jnp.dot(a, b, preferred_element_type=F32)


def _gateup_kernel(a_ref, wg_ref, wu_ref, wd_ref, o_ref, wd_o_ref, wg_s, wu_s):
    @pl.when(pl.program_id(1) == 0)
    def _():
        wg_s[...] = wg_ref[...].astype(BF16)
        wu_s[...] = wu_ref[...].astype(BF16)

    wd_o_ref[...] = wd_ref[...].astype(BF16)

    a = a_ref[...]
    g = _dot(a, wg_s[...])
    u = _dot(a, wu_s[...])
    o_ref[...] = ((g * jax.nn.sigmoid(g)) * u).astype(o_ref.dtype)


def _gateup(a, w_gu, w_down, tm=1024, tn=512):
    m, k = a.shape
    nj, ni = D_FF // tn, m // tm
    n = w_down.shape[1]
    slab = D_FF // (nj * ni)
    return pl.pallas_call(
        _gateup_kernel,
        grid=(nj, ni),
        in_specs=[pl.BlockSpec((tm, k), lambda j, i: (i, 0)),
                  pl.BlockSpec((k, tn), lambda j, i: (0, j)),
                  pl.BlockSpec((k, tn), lambda j, i: (0, j + nj)),
                  pl.BlockSpec((slab, n), lambda j, i: (j * ni + i, 0))],
        out_specs=[pl.BlockSpec((tm, tn), lambda j, i: (i, j)),
                   pl.BlockSpec((slab, n), lambda j, i: (j * ni + i, 0))],
        out_shape=[jax.ShapeDtypeStruct((m, D_FF), BF16),
                   jax.ShapeDtypeStruct((D_FF, n), BF16)],
        scratch_shapes=[pltpu.VMEM((k, tn), BF16), pltpu.VMEM((k, tn), BF16)],
        compiler_params=_cparams(2),
        name="ffn_gateup",
    )(a, w_gu, w_gu, w_down)


def _proj_kernel(a_ref, wt_ref, o_ref, w_s):
    @pl.when(pl.program_id(1) == 0)
    def _():
        w_s[...] = jnp.transpose(wt_ref[...]).astype(BF16)

    o_ref[...] = _dot(a_ref[...], w_s[...]).astype(o_ref.dtype)


def _proj(a, w_t, n_out, tm=2048, tn=1024):
    m, k = a.shape
    return pl.pallas_call(
        _proj_kernel,
        grid=(n_out // tn, m // tm),
        in_specs=[pl.BlockSpec((tm, k), lambda j, i: (i, 0)),
                  pl.BlockSpec((tn, k), lambda j, i: (j, 0))],
        out_specs=pl.BlockSpec((tm, tn), lambda j, i: (i, j)),
        out_shape=jax.ShapeDtypeStruct((m, n_out), BF16),
        scratch_shapes=[pltpu.VMEM((k, tn), BF16)],
        compiler_params=_cparams(2),
        name="in_proj",
    )(a, w_t)


def _down_kernel(a_ref, w_ref, r_ref, g_ref, h_ref, n_ref):
    h = r_ref[...] + 0.5 * _dot(a_ref[...], w_ref[...])
    h_ref[...] = h
    n_ref[...] = _rms(h, g_ref[...]).astype(n_ref.dtype)


def _down(a, w_bf16, resid, g_next, tm=256):
    m, k = a.shape
    n = w_bf16.shape[1]
    return pl.pallas_call(
        _down_kernel,
        grid=(m // tm,),
        in_specs=[pl.BlockSpec((tm, k), lambda i: (i, 0)),
                  pl.BlockSpec((k, n), lambda i: (0, 0), pipeline_mode=pl.Buffered(1)),
                  pl.BlockSpec((tm, n), lambda i: (i, 0)),
                  pl.BlockSpec((1, n), lambda i: (0, 0))],
        out_specs=[pl.BlockSpec((tm, n), lambda i: (i, 0)),
                   pl.BlockSpec((tm, n), lambda i: (i, 0))],
        out_shape=[jax.ShapeDtypeStruct((m, n), F32), jax.ShapeDtypeStruct((m, n), BF16)],
        compiler_params=_cparams(1),
        name="ffn_down",
    )(a, w_bf16, resid, g_next.reshape(1, n))


def _outproj_kernel(oa_ref, ob_ref, w_ref, r_ref, g_ref, h_ref, n_ref):
    acc = _dot(oa_ref[...], w_ref[:D_A, :]) + _dot(ob_ref[...], w_ref[D_A:, :])
    h = r_ref[...] + acc
    h_ref[...] = h
    n_ref[...] = _rms(h, g_ref[...]).astype(n_ref.dtype)


def _outproj(o_a, o_b, w_out, resid, g_next, tm=512):
    m = o_a.shape[0]
    k, n = w_out.shape
    return pl.pallas_call(
        _outproj_kernel,
        grid=(m // tm,),
        in_specs=[pl.BlockSpec((tm, D_A), lambda i: (i, 0)),
                  pl.BlockSpec((tm, D_B), lambda i: (i, 0)),
                  pl.BlockSpec((k, n), lambda i: (0, 0), pipeline_mode=pl.Buffered(1)),
                  pl.BlockSpec((tm, n), lambda i: (i, 0)),
                  pl.BlockSpec((1, n), lambda i: (0, 0))],
        out_specs=[pl.BlockSpec((tm, n), lambda i: (i, 0)),
                   pl.BlockSpec((tm, n), lambda i: (i, 0))],
        out_shape=[jax.ShapeDtypeStruct((m, n), F32), jax.ShapeDtypeStruct((m, n), BF16)],
        compiler_params=_cparams(1),
        name="out_proj",
    )(o_a, o_b, w_out, resid, g_next.reshape(1, n))


def _ple_kernel(close_block, a_ref, p_ref, wg_ref, wp_ref, r_ref, g_ref, o_ref, wp_s):
    @pl.when(pl.program_id(0) == 0)
    def _():
        wp_s[...] = wp_ref[...].astype(BF16)

    gate = jax.nn.sigmoid(_dot(a_ref[...], wg_ref[...]))
    emb = _dot(p_ref[...].astype(BF16), wp_s[...])
    h = r_ref[...] + gate * emb
    o_ref[...] = _rms(h, g_ref[...]) if close_block else h


def _ple(a, p, w_gate, w_proj, resid, g_final, close_block, tm=512):
    m, k = a.shape
    n = w_gate.shape[1]
    return pl.pallas_call(
        functools.partial(_ple_kernel, close_block),
        grid=(m // tm,),
        in_specs=[pl.BlockSpec((tm, k), lambda i: (i, 0)),
                  pl.BlockSpec((tm, D_PLE), lambda i: (i, 0)),
                  pl.BlockSpec((k, n), lambda i: (0, 0), pipeline_mode=pl.Buffered(1)),
                  pl.BlockSpec((D_PLE, n), lambda i: (0, 0), pipeline_mode=pl.Buffered(1)),
                  pl.BlockSpec((tm, n), lambda i: (i, 0)),
                  pl.BlockSpec((1, n), lambda i: (0, 0))],
        out_specs=pl.BlockSpec((tm, n), lambda i: (i, 0)),
        out_shape=jax.ShapeDtypeStruct((m, n), F32),
        scratch_shapes=[pltpu.VMEM((D_PLE, n), BF16)],
        compiler_params=_cparams(1),
        name="ple",
    )(a, p, w_gate, w_proj, resid, g_final.reshape(1, n))


def _attn_a_scores(q, k):
    return lax.dot_general(q, k, (((1,), (1,)), ((), ())), preferred_element_type=F32)


def _attn_a_softmax_pv(s, v, bias, key_lo):
    nk = v.shape[0]
    s = s * (SCALE * LOG2E) + bias
    row = lax.broadcasted_iota(jnp.int32, (QB_A, nk), 0)
    col = lax.broadcasted_iota(jnp.int32, (QB_A, nk), 1) + key_lo
    first = row < CHUNK
    valid = (first & (col < BAND_A - CHUNK)) | (jnp.logical_not(first) & (col >= CHUNK))
    s = jnp.where(valid, s, NEG_INF)
    mx = jnp.max(s, axis=-1, keepdims=True)
    e = jnp.exp2(s - mx)
    den = jnp.sum(e, axis=-1, keepdims=True)
    o = _dot(e.astype(BF16), v)
    return o / den


def _attn_a_kernel(g_ref, q_ref, k_ref, v_ref, w_ref, o_ref, w_o_ref, bias_s):
    w_o_ref[...] = w_ref[...].astype(BF16)

    @pl.when(pl.program_id(1) == 0)
    def _():
        g = jnp.broadcast_to(g_ref[...] * LOG2E, (QB_A, REL_ROW))
        bias_s[...] = pltpu.roll(g, 0, axis=1, stride=1, stride_axis=0)[:, :BAND_A]

    n_blocks = q_ref.shape[0] // QB_A

    def window(m):
        k0 = max(m * QB_A - PAD_A, 0)
        return k0, (m + 1) * QB_A

    def scores(m):
        k0, k1 = window(m)
        return _attn_a_scores(q_ref[m * QB_A:k1, :], k_ref[k0:k1, :])

    s_next = scores(0)
    for m in range(n_blocks):
        s = s_next
        if m + 1 < n_blocks:
            s_next = scores(m + 1)
        k0, k1 = window(m)
        lo = BAND_A - (k1 - k0)
        o = _attn_a_softmax_pv(s, v_ref[k0:k1, :], bias_s[:, lo:], lo)
        o_ref[m * QB_A:k1, :] = o.astype(o_ref.dtype)


def _attn_a(z, bias_rows, w_cast):
    b, s, _ = z.shape
    ha = N_HEADS_A
    wk, wn = w_cast.shape
    slab = wk // (ha * b)
    return pl.pallas_call(
        _attn_a_kernel,
        grid=(ha, b),
        in_specs=[pl.BlockSpec((None, 1, REL_ROW), lambda h, bi: (h, 0, 0)),
                  pl.BlockSpec((None, s, HEAD_DIM), lambda h, bi: (bi, 0, h)),
                  pl.BlockSpec((None, s, HEAD_DIM), lambda h, bi: (bi, 0, ha + h)),
                  pl.BlockSpec((None, s, HEAD_DIM), lambda h, bi: (bi, 0, 2 * ha + h)),
                  pl.BlockSpec((slab, wn), lambda h, bi: (h * b + bi, 0))],
        out_specs=[pl.BlockSpec((None, s, HEAD_DIM), lambda h, bi: (bi, 0, h)),
                   pl.BlockSpec((slab, wn), lambda h, bi: (h * b + bi, 0))],
        out_shape=[jax.ShapeDtypeStruct((b, s, D_A), BF16),
                   jax.ShapeDtypeStruct((wk, wn), BF16)],
        scratch_shapes=[pltpu.VMEM((QB_A, BAND_A), F32)],
        compiler_params=_cparams(2),
        name="attn_band",
    )(bias_rows, z, z, z, w_cast)


def _fox_prep_kernel(u_ref, wft_ref, bf_ref, f_ref, ft_ref):
    seq, d = u_ref.shape
    wft = jnp.concatenate([wft_ref[...], jnp.zeros((LANES - N_HEADS_B, d), F32)], axis=0).astype(BF16)
    fl = lax.dot_general(u_ref[...], wft, (((1,), (1,)), ((), ())), preferred_element_type=F32)
    fl = fl + bf_ref[...]
    x = jnp.minimum(fl, 0.0) - jnp.log1p(jnp.exp(-jnp.abs(fl)))
    row = lax.broadcasted_iota(jnp.int32, x.shape, 0)
    shift = 1
    while shift < seq:
        x = x + jnp.where(row >= shift, pltpu.roll(x, shift, axis=0), 0.0)
        shift *= 2
    x = x * LOG2E
    f_ref[...] = x
    ft_ref[...] = jnp.transpose(x)[:N_HEADS_B, :]


def _fox_prep(u3, w_in_t, b_f):
    b, s, d = u3.shape
    return pl.pallas_call(
        _fox_prep_kernel,
        grid=(b,),
        in_specs=[pl.BlockSpec((None, s, d), lambda bi: (bi, 0, 0)),
                  pl.BlockSpec((N_HEADS_B, d), lambda bi: (D_QKV // N_HEADS_B, 0)),
                  pl.BlockSpec((1, LANES), lambda bi: (0, 0))],
        out_specs=[pl.BlockSpec((None, s, LANES), lambda bi: (bi, 0, 0)),
                   pl.BlockSpec((None, N_HEADS_B, s), lambda bi: (bi, 0, 0))],
        out_shape=[jax.ShapeDtypeStruct((b, s, LANES), F32),
                   jax.ShapeDtypeStruct((b, N_HEADS_B, s), F32)],
        compiler_params=_cparams(1),
        name="fox_prep",
    )(u3, w_in_t, b_f)


def _fox_kernel(q_ref, k_ref, v_ref, fq_ref, fk_ref, w_ref, o_ref, w_o_ref):
    w_o_ref[...] = w_ref[...].astype(BF16)
    seq = q_ref.shape[0]
    h = pl.program_id(1)
    lane = lax.broadcasted_iota(jnp.int32, (TQ_B, LANES), 1)
    row = lax.broadcasted_iota(jnp.int32, (TQ_B, TQ_B), 0)
    col = lax.broadcasted_iota(jnp.int32, (TQ_B, TQ_B), 1)
    nt = (((1,), (1,)), ((), ()))

    n_blocks = seq // TQ_B

    def scores(qi):
        q0, q1 = qi * TQ_B, (qi + 1) * TQ_B
        q = q_ref[q0:q1, :]
        s_diag = lax.dot_general(q, k_ref[q0:q1, :], nt, preferred_element_type=F32)
        s_past = lax.dot_general(q, k_ref[0:q0, :], nt, preferred_element_type=F32) if qi > 0 else None
        return s_diag, s_past

    s_next = scores(0)
    for qi in range(n_blocks):
        q0, q1 = qi * TQ_B, (qi + 1) * TQ_B
        s_d, s_p = s_next
        if qi + 1 < n_blocks:
            s_next = scores(qi + 1)
        fq = jnp.sum(jnp.where(lane == h, fq_ref[q0:q1, :], 0.0), axis=1, keepdims=True)
        s_d = (s_d * (SCALE * LOG2E) + fq) - fk_ref[pl.ds(h, 1), q0:q1]
        s_d = jnp.where(row >= col, s_d, NEG_INF)
        mx = jnp.max(s_d, axis=-1, keepdims=True)
        if qi > 0:
            s_p = (s_p * (SCALE * LOG2E) + fq) - fk_ref[pl.ds(h, 1), 0:q0]
            mx = jnp.maximum(mx, jnp.max(s_p, axis=-1, keepdims=True))
        e_d = jnp.exp2(s_d - mx)
        den = jnp.sum(e_d, axis=-1, keepdims=True)
        acc = _dot(e_d.astype(BF16), v_ref[q0:q1, :])
        if qi > 0:
            e_p = jnp.exp2(s_p - mx)
            den = den + jnp.sum(e_p, axis=-1, keepdims=True)
            acc = acc + _dot(e_p.astype(BF16), v_ref[0:q0, :])
        o_ref[q0:q1, :] = (acc / den).astype(o_ref.dtype)


def _fox(z, f_col, f_row, w_cast):
    b, s, _ = z.shape
    hb = N_HEADS_B
    base = 3 * N_HEADS_A
    wk, wn = w_cast.shape
    slab = wk // (b * hb)
    return pl.pallas_call(
        _fox_kernel,
        grid=(b, hb),
        in_specs=[pl.BlockSpec((None, s, HEAD_DIM), lambda bi, h: (bi, 0, base + h)),
                  pl.BlockSpec((None, s, HEAD_DIM), lambda bi, h: (bi, 0, base + hb + h)),
                  pl.BlockSpec((None, s, HEAD_DIM), lambda bi, h: (bi, 0, base + 2 * hb + h)),
                  pl.BlockSpec((None, s, LANES), lambda bi, h: (bi, 0, 0)),
                  pl.BlockSpec((None, hb, s), lambda bi, h: (bi, 0, 0)),
                  pl.BlockSpec((slab, wn), lambda bi, h: (bi * hb + h, 0))],
        out_specs=[pl.BlockSpec((None, s, HEAD_DIM), lambda bi, h: (bi, 0, h)),
                   pl.BlockSpec((slab, wn), lambda bi, h: (bi * hb + h, 0))],
        out_shape=[jax.ShapeDtypeStruct((b, s, D_B), BF16),
                   jax.ShapeDtypeStruct((wk, wn), BF16)],
        compiler_params=_cparams(2),
        name="attn_fox",
    )(z, z, z, f_col, f_row, w_cast)


def _rel_bias_rows(rel_bias):
    far = rel_bias[:, 2 * REL_CLIP:]
    n_head = PAD_A - REL_CLIP
    n_tail = REL_ROW - n_head - (2 * REL_CLIP + 1)
    rows = jnp.concatenate([jnp.tile(far, (1, n_head)), jnp.flip(rel_bias, axis=1),
                            jnp.tile(far, (1, n_tail))], axis=1)
    return rows[:, None, :]


def kernel(x, p, g_ffn1, w_ffn1_gu, w_ffn1_down, g_mix, w_in, b_forget, rel_bias, w_out,
           g_ffn2, w_ffn2_gu, w_ffn2_down, g_ple, w_ple_gate, w_ple_proj, g_final):
    b, s, d = x.shape
    m = b * s
    depth = p.shape[0]
    h = x.reshape(m, d)
    for i in range(depth):
        xn = _rmsnorm(h, g_ffn1[i], BF16)
        hid, w_down = _gateup(xn, w_ffn1_gu[i], w_ffn1_down[i])
        h, u = _down(hid, w_down, h, g_mix[i])
        w_in_t = jnp.swapaxes(w_in[i], 0, 1)
        z = _proj(u, w_in_t, D_QKV).reshape(b, s, D_QKV)
        b_f = jnp.pad(b_forget[i], (0, LANES - N_HEADS_B)).reshape(1, LANES)
        f_col, f_row = _fox_prep(u.reshape(b, s, d), w_in_t, b_f)
        o_a, w_out_bf = _attn_a(z, _rel_bias_rows(rel_bias[i]), w_out[i])
        o_b, w_gate_bf = _fox(z, f_col, f_row, w_ple_gate[i])
        h, xn = _outproj(o_a.reshape(m, D_A), o_b.reshape(m, D_B), w_out_bf, h, g_ffn2[i])
        hid, w_down = _gateup(xn, w_ffn2_gu[i], w_ffn2_down[i])
        h, hn = _down(hid, w_down, h, g_ple[i])
        h = _ple(hn, p[i].reshape(m, D_PLE), w_gate_bf, w_ple_proj[i], h, g_final, i == depth - 1)
    return h.reshape(b, s, d)
```

```python
import functools

import jax
import jax.numpy as jnp
from jax import lax
from jax.experimental import pallas as pl
from jax.experimental.pallas import tpu as pltpu

F32 = jnp.float32
BF16 = jnp.bfloat16

D_MODEL = 2048
CHUNK = 64
N_LEFT_CHUNKS = 8
HEAD_DIM = 128
N_HEADS_A = 8
N_HEADS_B = 8
D_A = N_HEADS_A * HEAD_DIM
D_B = N_HEADS_B * HEAD_DIM
REL_CLIP = 256
D_FF = 5632
D_PLE = 256
EPS = 1e-6
NEG_INF = -1e30
D_QKV = 3 * D_A + 3 * D_B
SCALE = HEAD_DIM ** -0.5
LOG2E = 1.4426950408889634

LANES = 128
VMEM_LIMIT = 56 * 1024 * 1024

QB_A = 2 * CHUNK
BAND_A = (N_LEFT_CHUNKS + 2) * CHUNK
PAD_A = N_LEFT_CHUNKS * CHUNK
REL_ROW = 1024

TQ_B = 256
HEADS_PER_STEP = 2
QK_AHEAD = 2


def _cparams(n_axes):
    return pltpu.CompilerParams(
        dimension_semantics=("arbitrary",) * n_axes,
        vmem_limit_bytes=VMEM_LIMIT,
    )


def _rms(x, g):
    ms = jnp.mean(x * x, axis=-1, keepdims=True)
    return (x * lax.rsqrt(ms + EPS)) * g


def _rmsnorm_kernel(x_ref, g_ref, o_ref):
    o_ref[...] = _rms(x_ref[...], g_ref[...]).astype(o_ref.dtype)


def _rmsnorm(x, g, out_dtype, tm=512):
    m, d = x.shape
    return pl.pallas_call(
        _rmsnorm_kernel,
        grid=(m // tm,),
        in_specs=[pl.BlockSpec((tm, d), lambda i: (i, 0)),
                  pl.BlockSpec((1, d), lambda i: (0, 0))],
        out_specs=pl.BlockSpec((tm, d), lambda i: (i, 0)),
        out_shape=jax.ShapeDtypeStruct((m, d), out_dtype),
        compiler_params=_cparams(1),
        name="rmsnorm",
    )(x, g.reshape(1, d))


def _dot(a, b):
    return jnp.dot(a, b, preferred_element_type=F32)


def _gateup_kernel(a_ref, wg_ref, wu_ref, wd_ref, o_ref, wd_o_ref, wg_s, wu_s):
    @pl.when(pl.program_id(1) == 0)
    def _():
        wg_s[...] = wg_ref[...].astype(BF16)
        wu_s[...] = wu_ref[...].astype(BF16)

    wd_o_ref[...] = wd_ref[...].astype(BF16)

    a = a_ref[...]
    g = _dot(a, wg_s[...])
    u = _dot(a, wu_s[...])
    o_ref[...] = ((g * jax.nn.sigmoid(g)) * u).astype(o_ref.dtype)


def _gateup(a, w_gu, w_down, tm=1024, tn=512):
    m, k = a.shape
    nj, ni = D_FF // tn, m // tm
    n = w_down.shape[1]
    slab = D_FF // (nj * ni)
    return pl.pallas_call(
        _gateup_kernel,
        grid=(nj, ni),
        in_specs=[pl.BlockSpec((tm, k), lambda j, i: (i, 0)),
                  pl.BlockSpec((k, tn), lambda j, i: (0, j)),
                  pl.BlockSpec((k, tn), lambda j, i: (0, j + nj)),
                  pl.BlockSpec((slab, n), lambda j, i: (j * ni + i, 0))],
        out_specs=[pl.BlockSpec((tm, tn), lambda j, i: (i, j)),
                   pl.BlockSpec((slab, n), lambda j, i: (j * ni + i, 0))],
        out_shape=[jax.ShapeDtypeStruct((m, D_FF), BF16),
                   jax.ShapeDtypeStruct((D_FF, n), BF16)],
        scratch_shapes=[pltpu.VMEM((k, tn), BF16), pltpu.VMEM((k, tn), BF16)],
        compiler_params=_cparams(2),
        name="ffn_gateup",
    )(a, w_gu, w_gu, w_down)


def _proj_kernel(a_ref, wt_ref, o_ref, w_s):
    @pl.when(pl.program_id(1) == 0)
    def _():
        w_s[...] = jnp.transpose(wt_ref[...]).astype(BF16)

    o_ref[...] = _dot(a_ref[...], w_s[...]).astype(o_ref.dtype)


def _proj(a, w_t, n_out, tm=2048, tn=1024):
    m, k = a.shape
    return pl.pallas_call(
        _proj_kernel,
        grid=(n_out // tn, m // tm),
        in_specs=[pl.BlockSpec((tm, k), lambda j, i: (i, 0)),
                  pl.BlockSpec((tn, k), lambda j, i: (j, 0))],
        out_specs=pl.BlockSpec((tm, tn), lambda j, i: (i, j)),
        out_shape=jax.ShapeDtypeStruct((m, n_out), BF16),
        scratch_shapes=[pltpu.VMEM((k, tn), BF16)],
        compiler_params=_cparams(2),
        name="in_proj",
    )(a, w_t)


def _down_kernel(a_ref, w_ref, r_ref, g_ref, h_ref, n_ref):
    h = r_ref[...] + 0.5 * _dot(a_ref[...], w_ref[...])
    h_ref[...] = h
    n_ref[...] = _rms(h, g_ref[...]).astype(n_ref.dtype)


def _down(a, w_bf16, resid, g_next, tm=256):
    m, k = a.shape
    n = w_bf16.shape[1]
    return pl.pallas_call(
        _down_kernel,
        grid=(m // tm,),
        in_specs=[pl.BlockSpec((tm, k), lambda i: (i, 0)),
                  pl.BlockSpec((k, n), lambda i: (0, 0), pipeline_mode=pl.Buffered(1)),
                  pl.BlockSpec((tm, n), lambda i: (i, 0)),
                  pl.BlockSpec((1, n), lambda i: (0, 0))],
        out_specs=[pl.BlockSpec((tm, n), lambda i: (i, 0)),
                   pl.BlockSpec((tm, n), lambda i: (i, 0))],
        out_shape=[jax.ShapeDtypeStruct((m, n), F32), jax.ShapeDtypeStruct((m, n), BF16)],
        compiler_params=_cparams(1),
        name="ffn_down",
    )(a, w_bf16, resid, g_next.reshape(1, n))


def _outproj_kernel(oa_ref, ob_ref, w_ref, r_ref, g_ref, h_ref, n_ref):
    acc = _dot(oa_ref[...], w_ref[:D_A, :]) + _dot(ob_ref[...], w_ref[D_A:, :])
    h = r_ref[...] + acc
    h_ref[...] = h
    n_ref[...] = _rms(h, g_ref[...]).astype(n_ref.dtype)


def _outproj(o_a, o_b, w_out, resid, g_next, tm=512):
    m = o_a.shape[0]
    k, n = w_out.shape
    return pl.pallas_call(
        _outproj_kernel,
        grid=(m // tm,),
        in_specs=[pl.BlockSpec((tm, D_A), lambda i: (i, 0)),
                  pl.BlockSpec((tm, D_B), lambda i: (i, 0)),
                  pl.BlockSpec((k, n), lambda i: (0, 0), pipeline_mode=pl.Buffered(1)),
                  pl.BlockSpec((tm, n), lambda i: (i, 0)),
                  pl.BlockSpec((1, n), lambda i: (0, 0))],
        out_specs=[pl.BlockSpec((tm, n), lambda i: (i, 0)),
                   pl.BlockSpec((tm, n), lambda i: (i, 0))],
        out_shape=[jax.ShapeDtypeStruct((m, n), F32), jax.ShapeDtypeStruct((m, n), BF16)],
        compiler_params=_cparams(1),
        name="out_proj",
    )(o_a, o_b, w_out, resid, g_next.reshape(1, n))


def _ple_kernel(close_block, a_ref, p_ref, wg_ref, wp_ref, r_ref, g_ref, o_ref, wp_s):
    @pl.when(pl.program_id(0) == 0)
    def _():
        wp_s[...] = wp_ref[...].astype(BF16)

    gate = jax.nn.sigmoid(_dot(a_ref[...], wg_ref[...]))
    emb = _dot(p_ref[...].astype(BF16), wp_s[...])
    h = r_ref[...] + gate * emb
    o_ref[...] = _rms(h, g_ref[...]) if close_block else h


def _ple(a, p, w_gate, w_proj, resid, g_final, close_block, tm=512):
    m, k = a.shape
    n = w_gate.shape[1]
    return pl.pallas_call(
        functools.partial(_ple_kernel, close_block),
        grid=(m // tm,),
        in_specs=[pl.BlockSpec((tm, k), lambda i: (i, 0)),
                  pl.BlockSpec((tm, D_PLE), lambda i: (i, 0)),
                  pl.BlockSpec((k, n), lambda i: (0, 0), pipeline_mode=pl.Buffered(1)),
                  pl.BlockSpec((D_PLE, n), lambda i: (0, 0), pipeline_mode=pl.Buffered(1)),
                  pl.BlockSpec((tm, n), lambda i: (i, 0)),
                  pl.BlockSpec((1, n), lambda i: (0, 0))],
        out_specs=pl.BlockSpec((tm, n), lambda i: (i, 0)),
        out_shape=jax.ShapeDtypeStruct((m, n), F32),
        scratch_shapes=[pltpu.VMEM((D_PLE, n), BF16)],
        compiler_params=_cparams(1),
        name="ple",
    )(a, p, w_gate, w_proj, resid, g_final.reshape(1, n))


def _attn_a_scores(q, k):
    return lax.dot_general(q, k, (((1,), (1,)), ((), ())), preferred_element_type=F32)


def _attn_a_softmax(s, bias, key_lo):
    nk = s.shape[1]
    s = s * (SCALE * LOG2E) + bias
    row = lax.broadcasted_iota(jnp.int32, (QB_A, nk), 0)
    col = lax.broadcasted_iota(jnp.int32, (QB_A, nk), 1) + key_lo
    first = row < CHUNK
    valid = (first & (col < BAND_A - CHUNK)) | (jnp.logical_not(first) & (col >= CHUNK))
    s = jnp.where(valid, s, NEG_INF)
    mx = jnp.max(s, axis=-1, keepdims=True)
    e = jnp.exp2(s - mx)
    den = jnp.sum(e, axis=-1, keepdims=True)
    return e.astype(BF16), den


def _attn_a_kernel(g_ref, q_ref, k_ref, v_ref, w_ref, o_ref, w_o_ref, bias_s):
    w_o_ref[...] = w_ref[...].astype(BF16)

    @pl.when(pl.program_id(1) == 0)
    def _():
        for hh in range(HEADS_PER_STEP):
            g = jnp.broadcast_to(g_ref[hh] * LOG2E, (QB_A, REL_ROW))
            bias_s[hh] = pltpu.roll(g, 0, axis=1, stride=1, stride_axis=0)[:, :BAND_A]

    n_blocks = q_ref.shape[0] // QB_A
    items = [(m, hh) for m in range(n_blocks) for hh in range(HEADS_PER_STEP)]

    def window(m):
        k0 = max(m * QB_A - PAD_A, 0)
        return k0, (m + 1) * QB_A

    def lanes(hh):
        return slice(hh * HEAD_DIM, (hh + 1) * HEAD_DIM)

    def scores(m, hh):
        k0, k1 = window(m)
        return _attn_a_scores(q_ref[m * QB_A:k1, lanes(hh)], k_ref[k0:k1, lanes(hh)])

    def finish(m, hh, e, den):
        k0, k1 = window(m)
        o = _dot(e, v_ref[k0:k1, lanes(hh)])
        o_ref[m * QB_A:k1, lanes(hh)] = (o / den).astype(o_ref.dtype)

    s_queue = [scores(*items[t]) for t in range(QK_AHEAD)]
    pending = None
    for t, (m, hh) in enumerate(items):
        s = s_queue.pop(0)
        if t + QK_AHEAD < len(items):
            s_queue.append(scores(*items[t + QK_AHEAD]))
        k0, k1 = window(m)
        lo = BAND_A - (k1 - k0)
        e, den = _attn_a_softmax(s, bias_s[hh, :, lo:], lo)
        if pending is not None:
            finish(*pending)
        pending = (m, hh, e, den)
    finish(*pending)


def _attn_a(z, bias_rows, w_cast):
    b, s, _ = z.shape
    hp = HEADS_PER_STEP
    n_groups = N_HEADS_A // hp
    width = hp * HEAD_DIM
    wk, wn = w_cast.shape
    slab = wk // (n_groups * b)
    return pl.pallas_call(
        _attn_a_kernel,
        grid=(n_groups, b),
        in_specs=[pl.BlockSpec((hp, 1, REL_ROW), lambda h, bi: (h, 0, 0)),
                  pl.BlockSpec((None, s, width), lambda h, bi: (bi, 0, h)),
                  pl.BlockSpec((None, s, width), lambda h, bi: (bi, 0, n_groups + h)),
                  pl.BlockSpec((None, s, width), lambda h, bi: (bi, 0, 2 * n_groups + h)),
                  pl.BlockSpec((slab, wn), lambda h, bi: (h * b + bi, 0))],
        out_specs=[pl.BlockSpec((None, s, width), lambda h, bi: (bi, 0, h)),
                   pl.BlockSpec((slab, wn), lambda h, bi: (h * b + bi, 0))],
        out_shape=[jax.ShapeDtypeStruct((b, s, D_A), BF16),
                   jax.ShapeDtypeStruct((wk, wn), BF16)],
        scratch_shapes=[pltpu.VMEM((hp, QB_A, BAND_A), F32)],
        compiler_params=_cparams(2),
        name="attn_band",
    )(bias_rows, z, z, z, w_cast)


def _fox_prep_kernel(u_ref, wft_ref, bf_ref, f_ref, ft_ref):
    seq, d = u_ref.shape
    wft = jnp.concatenate([wft_ref[...], jnp.zeros((LANES - N_HEADS_B, d), F32)], axis=0).astype(BF16)
    fl = lax.dot_general(u_ref[...], wft, (((1,), (1,)), ((), ())), preferred_element_type=F32)
    fl = fl + bf_ref[...]
    x = jnp.minimum(fl, 0.0) - jnp.log1p(jnp.exp(-jnp.abs(fl)))
    row = lax.broadcasted_iota(jnp.int32, x.shape, 0)
    shift = 1
    while shift < seq:
        x = x + jnp.where(row >= shift, pltpu.roll(x, shift, axis=0), 0.0)
        shift *= 2
    x = x * LOG2E
    f_ref[...] = x
    ft_ref[...] = jnp.transpose(x)[:N_HEADS_B, :]


def _fox_prep(u3, w_in_t, b_f):
    b, s, d = u3.shape
    return pl.pallas_call(
        _fox_prep_kernel,
        grid=(b,),
        in_specs=[pl.BlockSpec((None, s, d), lambda bi: (bi, 0, 0)),
                  pl.BlockSpec((N_HEADS_B, d), lambda bi: (D_QKV // N_HEADS_B, 0)),
                  pl.BlockSpec((1, LANES), lambda bi: (0, 0))],
        out_specs=[pl.BlockSpec((None, s, LANES), lambda bi: (bi, 0, 0)),
                   pl.BlockSpec((None, N_HEADS_B, s), lambda bi: (bi, 0, 0))],
        out_shape=[jax.ShapeDtypeStruct((b, s, LANES), F32),
                   jax.ShapeDtypeStruct((b, N_HEADS_B, s), F32)],
        compiler_params=_cparams(1),
        name="fox_prep",
    )(u3, w_in_t, b_f)


def _fox_kernel(q_ref, k_ref, v_ref, fq_ref, fk_ref, w_ref, o_ref, w_o_ref):
    w_o_ref[...] = w_ref[...].astype(BF16)
    seq = q_ref.shape[0]
    h = pl.program_id(1)
    lane = lax.broadcasted_iota(jnp.int32, (TQ_B, LANES), 1)
    row = lax.broadcasted_iota(jnp.int32, (TQ_B, TQ_B), 0)
    col = lax.broadcasted_iota(jnp.int32, (TQ_B, TQ_B), 1)
    nt = (((1,), (1,)), ((), ()))

    n_blocks = seq // TQ_B

    def scores(qi):
        q0, q1 = qi * TQ_B, (qi + 1) * TQ_B
        q = q_ref[q0:q1, :]
        s_diag = lax.dot_general(q, k_ref[q0:q1, :], nt, preferred_element_type=F32)
        s_past = lax.dot_general(q, k_ref[0:q0, :], nt, preferred_element_type=F32) if qi > 0 else None
        return s_diag, s_past

    s_next = scores(0)
    for qi in range(n_blocks):
        q0, q1 = qi * TQ_B, (qi + 1) * TQ_B
        s_d, s_p = s_next
        if qi + 1 < n_blocks:
            s_next = scores(qi + 1)
        fq = jnp.sum(jnp.where(lane == h, fq_ref[q0:q1, :], 0.0), axis=1, keepdims=True)
        s_d = (s_d * (SCALE * LOG2E) + fq) - fk_ref[pl.ds(h, 1), q0:q1]
        s_d = jnp.where(row >= col, s_d, NEG_INF)
        mx = jnp.max(s_d, axis=-1, keepdims=True)
        if qi > 0:
            s_p = (s_p * (SCALE * LOG2E) + fq) - fk_ref[pl.ds(h, 1), 0:q0]
            mx = jnp.maximum(mx, jnp.max(s_p, axis=-1, keepdims=True))
        e_d = jnp.exp2(s_d - mx)
        den = jnp.sum(e_d, axis=-1, keepdims=True)
        acc = _dot(e_d.astype(BF16), v_ref[q0:q1, :])
        if qi > 0:
            e_p = jnp.exp2(s_p - mx)
            den = den + jnp.sum(e_p, axis=-1, keepdims=True)
            acc = acc + _dot(e_p.astype(BF16), v_ref[0:q0, :])
        o_ref[q0:q1, :] = (acc / den).astype(o_ref.dtype)


def _fox(z, f_col, f_row, w_cast):
    b, s, _ = z.shape
    hb = N_HEADS_B
    base = 3 * N_HEADS_A
    wk, wn = w_cast.shape
    slab = wk // (b * hb)
    return pl.pallas_call(
        _fox_kernel,
        grid=(b, hb),
        in_specs=[pl.BlockSpec((None, s, HEAD_DIM), lambda bi, h: (bi, 0, base + h)),
                  pl.BlockSpec((None, s, HEAD_DIM), lambda bi, h: (bi, 0, base + hb + h)),
                  pl.BlockSpec((None, s, HEAD_DIM), lambda bi, h: (bi, 0, base + 2 * hb + h)),
                  pl.BlockSpec((None, s, LANES), lambda bi, h: (bi, 0, 0)),
                  pl.BlockSpec((None, hb, s), lambda bi, h: (bi, 0, 0)),
                  pl.BlockSpec((slab, wn), lambda bi, h: (bi * hb + h, 0))],
        out_specs=[pl.BlockSpec((None, s, HEAD_DIM), lambda bi, h: (bi, 0, h)),
                   pl.BlockSpec((slab, wn), lambda bi, h: (bi * hb + h, 0))],
        out_shape=[jax.ShapeDtypeStruct((b, s, D_B), BF16),
                   jax.ShapeDtypeStruct((wk, wn), BF16)],
        compiler_params=_cparams(2),
        name="attn_fox",
    )(z, z, z, f_col, f_row, w_cast)


def _rel_bias_rows(rel_bias):
    far = rel_bias[:, 2 * REL_CLIP:]
    n_head = PAD_A - REL_CLIP
    n_tail = REL_ROW - n_head - (2 * REL_CLIP + 1)
    rows = jnp.concatenate([jnp.tile(far, (1, n_head)), jnp.flip(rel_bias, axis=1),
                            jnp.tile(far, (1, n_tail))], axis=1)
    return rows[:, None, :]


def kernel(x, p, g_ffn1, w_ffn1_gu, w_ffn1_down, g_mix, w_in, b_forget, rel_bias, w_out,
           g_ffn2, w_ffn2_gu, w_ffn2_down, g_ple, w_ple_gate, w_ple_proj, g_final):
    b, s, d = x.shape
    m = b * s
    depth = p.shape[0]
    h = x.reshape(m, d)
    for i in range(depth):
        xn = _rmsnorm(h, g_ffn1[i], BF16)
        hid, w_down = _gateup(xn, w_ffn1_gu[i], w_ffn1_down[i])
        h, u = _down(hid, w_down, h, g_mix[i])
        w_in_t = jnp.swapaxes(w_in[i], 0, 1)
        z = _proj(u, w_in_t, D_QKV).reshape(b, s, D_QKV)
        b_f = jnp.pad(b_forget[i], (0, LANES - N_HEADS_B)).reshape(1, LANES)
        f_col, f_row = _fox_prep(u.reshape(b, s, d), w_in_t, b_f)
        o_a, w_out_bf = _attn_a(z, _rel_bias_rows(rel_bias[i]), w_out[i])
        o_b, w_gate_bf = _fox(z, f_col, f_row, w_ple_gate[i])
        h, xn = _outproj(o_a.reshape(m, D_A), o_b.reshape(m, D_B), w_out_bf, h, g_ffn2[i])
        hid, w_down = _gateup(xn, w_ffn2_gu[i], w_ffn2_down[i])
        h, hn = _down(hid, w_down, h, g_ple[i])
        h = _ple(hn, p[i].reshape(m, D_PLE), w_gate_bf, w_ple_proj[i], h, g_final, i == depth - 1)
    return h.reshape(b, s, d)
```

```python
import functools

import jax
import jax.numpy as jnp
from jax import lax
from jax.experimental import pallas as pl
from jax.experimental.pallas import tpu as pltpu

F32 = jnp.float32
BF16 = jnp.bfloat16

D_MODEL = 2048
CHUNK = 64
N_LEFT_CHUNKS = 8
HEAD_DIM = 128
N_HEADS_A = 8
N_HEADS_B = 8
D_A = N_HEADS_A * HEAD_DIM
D_B = N_HEADS_B * HEAD_DIM
REL_CLIP = 256
D_FF = 5632
D_PLE = 256
EPS = 1e-6
NEG_INF = -1e30
D_QKV = 3 * D_A + 3 * D_B
SCALE = HEAD_DIM ** -0.5
LOG2E = 1.4426950408889634

LANES = 128
VMEM_LIMIT = 56 * 1024 * 1024

QB_A = 2 * CHUNK
BAND_A = (N_LEFT_CHUNKS + 2) * CHUNK
PAD_A = N_LEFT_CHUNKS * CHUNK
REL_ROW = 1024

GATEUP_ROWS = 1024
TQ_B = 256
HEADS_PER_STEP = 2
QK_AHEAD = 2


def _cparams(n_axes):
    return pltpu.CompilerParams(
        dimension_semantics=("arbitrary",) * n_axes,
        vmem_limit_bytes=VMEM_LIMIT,
    )


def _rms(x, g):
    ms = jnp.mean(x * x, axis=-1, keepdims=True)
    return (x * lax.rsqrt(ms + EPS)) * g


def _rmsnorm_kernel(x_ref, g_ref, o_ref):
    o_ref[...] = _rms(x_ref[...], g_ref[...]).astype(o_ref.dtype)


def _rmsnorm(x, g, out_dtype, tm=512):
    m, d = x.shape
    return pl.pallas_call(
        _rmsnorm_kernel,
        grid=(m // tm,),
        in_specs=[pl.BlockSpec((tm, d), lambda i: (i, 0)),
                  pl.BlockSpec((1, d), lambda i: (0, 0))],
        out_specs=pl.BlockSpec((tm, d), lambda i: (i, 0)),
        out_shape=jax.ShapeDtypeStruct((m, d), out_dtype),
        compiler_params=_cparams(1),
        name="rmsnorm",
    )(x, g.reshape(1, d))


def _dot(a, b):
    return jnp.dot(a, b, preferred_element_type=F32)


def _gateup_kernel(a_ref, wg_ref, wu_ref, wd_ref, o_ref, wd_o_ref, w_s):
    m, tn = o_ref.shape
    w_s[:, :tn] = wg_ref[...].astype(BF16)
    w_s[:, tn:] = wu_ref[...].astype(BF16)

    wd_o_ref[...] = wd_ref[...].astype(BF16)

    w = w_s[...]
    for c in range(m // GATEUP_ROWS):
        rows = slice(c * GATEUP_ROWS, (c + 1) * GATEUP_ROWS)
        r = _dot(a_ref[rows, :], w)
        g = r[:, :tn]
        u = r[:, tn:]
        o_ref[rows, :] = ((g * jax.nn.sigmoid(g)) * u).astype(o_ref.dtype)


def _gateup(a, w_gu, w_down, tn=128):
    m, k = a.shape
    nj = D_FF // tn
    n = w_down.shape[1]
    slab = D_FF // nj
    return pl.pallas_call(
        _gateup_kernel,
        grid=(nj,),
        in_specs=[pl.BlockSpec((m, k), lambda j: (0, 0), pipeline_mode=pl.Buffered(1)),
                  pl.BlockSpec((k, tn), lambda j: (0, j)),
                  pl.BlockSpec((k, tn), lambda j: (0, j + nj)),
                  pl.BlockSpec((slab, n), lambda j: (j, 0))],
        out_specs=[pl.BlockSpec((m, tn), lambda j: (0, j)),
                   pl.BlockSpec((slab, n), lambda j: (j, 0))],
        out_shape=[jax.ShapeDtypeStruct((m, D_FF), BF16),
                   jax.ShapeDtypeStruct((D_FF, n), BF16)],
        scratch_shapes=[pltpu.VMEM((k, 2 * tn), BF16)],
        compiler_params=_cparams(1),
        name="ffn_gateup",
    )(a, w_gu, w_gu, w_down)


def _proj_kernel(a_ref, wt_ref, o_ref, w_s):
    @pl.when(pl.program_id(1) == 0)
    def _():
        w_s[...] = jnp.transpose(wt_ref[...]).astype(BF16)

    o_ref[...] = _dot(a_ref[...], w_s[...]).astype(o_ref.dtype)


def _proj(a, w_t, n_out, tm=2048, tn=1024):
    m, k = a.shape
    return pl.pallas_call(
        _proj_kernel,
        grid=(n_out // tn, m // tm),
        in_specs=[pl.BlockSpec((tm, k), lambda j, i: (i, 0)),
                  pl.BlockSpec((tn, k), lambda j, i: (j, 0))],
        out_specs=pl.BlockSpec((tm, tn), lambda j, i: (i, j)),
        out_shape=jax.ShapeDtypeStruct((m, n_out), BF16),
        scratch_shapes=[pltpu.VMEM((k, tn), BF16)],
        compiler_params=_cparams(2),
        name="in_proj",
    )(a, w_t)


def _down_kernel(a_ref, w_ref, r_ref, g_ref, h_ref, n_ref):
    h = r_ref[...] + 0.5 * _dot(a_ref[...], w_ref[...])
    h_ref[...] = h
    n_ref[...] = _rms(h, g_ref[...]).astype(n_ref.dtype)


def _down(a, w_bf16, resid, g_next, tm=256):
    m, k = a.shape
    n = w_bf16.shape[1]
    return pl.pallas_call(
        _down_kernel,
        grid=(m // tm,),
        in_specs=[pl.BlockSpec((tm, k), lambda i: (i, 0)),
                  pl.BlockSpec((k, n), lambda i: (0, 0), pipeline_mode=pl.Buffered(1)),
                  pl.BlockSpec((tm, n), lambda i: (i, 0)),
                  pl.BlockSpec((1, n), lambda i: (0, 0))],
        out_specs=[pl.BlockSpec((tm, n), lambda i: (i, 0)),
                   pl.BlockSpec((tm, n), lambda i: (i, 0))],
        out_shape=[jax.ShapeDtypeStruct((m, n), F32), jax.ShapeDtypeStruct((m, n), BF16)],
        compiler_params=_cparams(1),
        name="ffn_down",
    )(a, w_bf16, resid, g_next.reshape(1, n))


def _outproj_kernel(oa_ref, ob_ref, w_ref, r_ref, g_ref, h_ref, n_ref):
    acc = _dot(oa_ref[...], w_ref[:D_A, :]) + _dot(ob_ref[...], w_ref[D_A:, :])
    h = r_ref[...] + acc
    h_ref[...] = h
    n_ref[...] = _rms(h, g_ref[...]).astype(n_ref.dtype)


def _outproj(o_a, o_b, w_out, resid, g_next, tm=512):
    m = o_a.shape[0]
    k, n = w_out.shape
    return pl.pallas_call(
        _outproj_kernel,
        grid=(m // tm,),
        in_specs=[pl.BlockSpec((tm, D_A), lambda i: (i, 0)),
                  pl.BlockSpec((tm, D_B), lambda i: (i, 0)),
                  pl.BlockSpec((k, n), lambda i: (0, 0), pipeline_mode=pl.Buffered(1)),
                  pl.BlockSpec((tm, n), lambda i: (i, 0)),
                  pl.BlockSpec((1, n), lambda i: (0, 0))],
        out_specs=[pl.BlockSpec((tm, n), lambda i: (i, 0)),
                   pl.BlockSpec((tm, n), lambda i: (i, 0))],
        out_shape=[jax.ShapeDtypeStruct((m, n), F32), jax.ShapeDtypeStruct((m, n), BF16)],
        compiler_params=_cparams(1),
        name="out_proj",
    )(o_a, o_b, w_out, resid, g_next.reshape(1, n))


def _ple_kernel(close_block, a_ref, p_ref, wg_ref, wp_ref, r_ref, g_ref, o_ref, wp_s):
    @pl.when(pl.program_id(0) == 0)
    def _():
        wp_s[...] = wp_ref[...].astype(BF16)

    gate = jax.nn.sigmoid(_dot(a_ref[...], wg_ref[...]))
    emb = _dot(p_ref[...].astype(BF16), wp_s[...])
    h = r_ref[...] + gate * emb
    o_ref[...] = _rms(h, g_ref[...]) if close_block else h


def _ple(a, p, w_gate, w_proj, resid, g_final, close_block, tm=512):
    m, k = a.shape
    n = w_gate.shape[1]
    return pl.pallas_call(
        functools.partial(_ple_kernel, close_block),
        grid=(m // tm,),
        in_specs=[pl.BlockSpec((tm, k), lambda i: (i, 0)),
                  pl.BlockSpec((tm, D_PLE), lambda i: (i, 0)),
                  pl.BlockSpec((k, n), lambda i: (0, 0), pipeline_mode=pl.Buffered(1)),
                  pl.BlockSpec((D_PLE, n), lambda i: (0, 0), pipeline_mode=pl.Buffered(1)),
                  pl.BlockSpec((tm, n), lambda i: (i, 0)),
                  pl.BlockSpec((1, n), lambda i: (0, 0))],
        out_specs=pl.BlockSpec((tm, n), lambda i: (i, 0)),
        out_shape=jax.ShapeDtypeStruct((m, n), F32),
        scratch_shapes=[pltpu.VMEM((D_PLE, n), BF16)],
        compiler_params=_cparams(1),
        name="ple",
    )(a, p, w_gate, w_proj, resid, g_final.reshape(1, n))


def _attn_a_scores(q, k):
    return lax.dot_general(q, k, (((1,), (1,)), ((), ())), preferred_element_type=F32)


def _attn_a_softmax(s, bias, key_lo):
    nk = s.shape[1]
    s = s * (SCALE * LOG2E) + bias
    row = lax.broadcasted_iota(jnp.int32, (QB_A, nk), 0)
    col = lax.broadcasted_iota(jnp.int32, (QB_A, nk), 1) + key_lo
    first = row < CHUNK
    valid = (first & (col < BAND_A - CHUNK)) | (jnp.logical_not(first) & (col >= CHUNK))
    s = jnp.where(valid, s, NEG_INF)
    mx = jnp.max(s, axis=-1, keepdims=True)
    e = jnp.exp2(s - mx)
    den = jnp.sum(e, axis=-1, keepdims=True)
    return e.astype(BF16), den


def _attn_a_kernel(g_ref, q_ref, k_ref, v_ref, w_ref, o_ref, w_o_ref, bias_s):
    w_o_ref[...] = w_ref[...].astype(BF16)

    @pl.when(pl.program_id(1) == 0)
    def _():
        for hh in range(HEADS_PER_STEP):
            g = jnp.broadcast_to(g_ref[hh] * LOG2E, (QB_A, REL_ROW))
            bias_s[hh] = pltpu.roll(g, 0, axis=1, stride=1, stride_axis=0)[:, :BAND_A]

    n_blocks = q_ref.shape[0] // QB_A
    items = [(m, hh) for m in range(n_blocks) for hh in range(HEADS_PER_STEP)]

    def window(m):
        k0 = max(m * QB_A - PAD_A, 0)
        return k0, (m + 1) * QB_A

    def lanes(hh):
        return slice(hh * HEAD_DIM, (hh + 1) * HEAD_DIM)

    def scores(m, hh):
        k0, k1 = window(m)
        return _attn_a_scores(q_ref[m * QB_A:k1, lanes(hh)], k_ref[k0:k1, lanes(hh)])

    def finish(m, hh, e, den):
        k0, k1 = window(m)
        o = _dot(e, v_ref[k0:k1, lanes(hh)])
        o_ref[m * QB_A:k1, lanes(hh)] = (o / den).astype(o_ref.dtype)

    s_queue = [scores(*items[t]) for t in range(QK_AHEAD)]
    pending = None
    for t, (m, hh) in enumerate(items):
        s = s_queue.pop(0)
        if t + QK_AHEAD < len(items):
            s_queue.append(scores(*items[t + QK_AHEAD]))
        k0, k1 = window(m)
        lo = BAND_A - (k1 - k0)
        e, den = _attn_a_softmax(s, bias_s[hh, :, lo:], lo)
        if pending is not None:
            finish(*pending)
        pending = (m, hh, e, den)
    finish(*pending)


def _attn_a(z, bias_rows, w_cast):
    b, s, _ = z.shape
    hp = HEADS_PER_STEP
    n_groups = N_HEADS_A // hp
    width = hp * HEAD_DIM
    wk, wn = w_cast.shape
    slab = wk // (n_groups * b)
    return pl.pallas_call(
        _attn_a_kernel,
        grid=(n_groups, b),
        in_specs=[pl.BlockSpec((hp, 1, REL_ROW), lambda h, bi: (h, 0, 0)),
                  pl.BlockSpec((None, s, width), lambda h, bi: (bi, 0, h)),
                  pl.BlockSpec((None, s, width), lambda h, bi: (bi, 0, n_groups + h)),
                  pl.BlockSpec((None, s, width), lambda h, bi: (bi, 0, 2 * n_groups + h)),
                  pl.BlockSpec((slab, wn), lambda h, bi: (h * b + bi, 0))],
        out_specs=[pl.BlockSpec((None, s, width), lambda h, bi: (bi, 0, h)),
                   pl.BlockSpec((slab, wn), lambda h, bi: (h * b + bi, 0))],
        out_shape=[jax.ShapeDtypeStruct((b, s, D_A), BF16),
                   jax.ShapeDtypeStruct((wk, wn), BF16)],
        scratch_shapes=[pltpu.VMEM((hp, QB_A, BAND_A), F32)],
        compiler_params=_cparams(2),
        name="attn_band",
    )(bias_rows, z, z, z, w_cast)


def _fox_prep_kernel(u_ref, wft_ref, bf_ref, f_ref, ft_ref):
    seq, d = u_ref.shape
    wft = jnp.concatenate([wft_ref[...], jnp.zeros((LANES - N_HEADS_B, d), F32)], axis=0).astype(BF16)
    fl = lax.dot_general(u_ref[...], wft, (((1,), (1,)), ((), ())), preferred_element_type=F32)
    fl = fl + bf_ref[...]
    x = jnp.minimum(fl, 0.0) - jnp.log1p(jnp.exp(-jnp.abs(fl)))
    row = lax.broadcasted_iota(jnp.int32, x.shape, 0)
    shift = 1
    while shift < seq:
        x = x + jnp.where(row >= shift, pltpu.roll(x, shift, axis=0), 0.0)
        shift *= 2
    x = x * LOG2E
    f_ref[...] = x
    ft_ref[...] = jnp.transpose(x)[:N_HEADS_B, :]


def _fox_prep(u3, w_in_t, b_f):
    b, s, d = u3.shape
    return pl.pallas_call(
        _fox_prep_kernel,
        grid=(b,),
        in_specs=[pl.BlockSpec((None, s, d), lambda bi: (bi, 0, 0)),
                  pl.BlockSpec((N_HEADS_B, d), lambda bi: (D_QKV // N_HEADS_B, 0)),
                  pl.BlockSpec((1, LANES), lambda bi: (0, 0))],
        out_specs=[pl.BlockSpec((None, s, LANES), lambda bi: (bi, 0, 0)),
                   pl.BlockSpec((None, N_HEADS_B, s), lambda bi: (bi, 0, 0))],
        out_shape=[jax.ShapeDtypeStruct((b, s, LANES), F32),
                   jax.ShapeDtypeStruct((b, N_HEADS_B, s), F32)],
        compiler_params=_cparams(1),
        name="fox_prep",
    )(u3, w_in_t, b_f)


def _fox_kernel(q_ref, k_ref, v_ref, fq_ref, fk_ref, w_ref, o_ref, w_o_ref):
    w_o_ref[...] = w_ref[...].astype(BF16)
    seq = q_ref.shape[0]
    h = pl.program_id(1)
    lane = lax.broadcasted_iota(jnp.int32, (TQ_B, LANES), 1)
    row = lax.broadcasted_iota(jnp.int32, (TQ_B, TQ_B), 0)
    col = lax.broadcasted_iota(jnp.int32, (TQ_B, TQ_B), 1)
    nt = (((1,), (1,)), ((), ()))

    n_blocks = seq // TQ_B

    def scores(qi):
        q0, q1 = qi * TQ_B, (qi + 1) * TQ_B
        q = q_ref[q0:q1, :]
        s_diag = lax.dot_general(q, k_ref[q0:q1, :], nt, preferred_element_type=F32)
        s_past = lax.dot_general(q, k_ref[0:q0, :], nt, preferred_element_type=F32) if qi > 0 else None
        return s_diag, s_past

    s_next = scores(0)
    for qi in range(n_blocks):
        q0, q1 = qi * TQ_B, (qi + 1) * TQ_B
        s_d, s_p = s_next
        if qi + 1 < n_blocks:
            s_next = scores(qi + 1)
        fq = jnp.sum(jnp.where(lane == h, fq_ref[q0:q1, :], 0.0), axis=1, keepdims=True)
        s_d = (s_d * (SCALE * LOG2E) + fq) - fk_ref[pl.ds(h, 1), q0:q1]
        s_d = jnp.where(row >= col, s_d, NEG_INF)
        mx = jnp.max(s_d, axis=-1, keepdims=True)
        if qi > 0:
            s_p = (s_p * (SCALE * LOG2E) + fq) - fk_ref[pl.ds(h, 1), 0:q0]
            mx = jnp.maximum(mx, jnp.max(s_p, axis=-1, keepdims=True))
        e_d = jnp.exp2(s_d - mx)
        den = jnp.sum(e_d, axis=-1, keepdims=True)
        acc = _dot(e_d.astype(BF16), v_ref[q0:q1, :])
        if qi > 0:
            e_p = jnp.exp2(s_p - mx)
            den = den + jnp.sum(e_p, axis=-1, keepdims=True)
            acc = acc + _dot(e_p.astype(BF16), v_ref[0:q0, :])
        o_ref[q0:q1, :] = (acc / den).astype(o_ref.dtype)


def _fox(z, f_col, f_row, w_cast):
    b, s, _ = z.shape
    hb = N_HEADS_B
    base = 3 * N_HEADS_A
    wk, wn = w_cast.shape
    slab = wk // (b * hb)
    return pl.pallas_call(
        _fox_kernel,
        grid=(b, hb),
        in_specs=[pl.BlockSpec((None, s, HEAD_DIM), lambda bi, h: (bi, 0, base + h)),
                  pl.BlockSpec((None, s, HEAD_DIM), lambda bi, h: (bi, 0, base + hb + h)),
                  pl.BlockSpec((None, s, HEAD_DIM), lambda bi, h: (bi, 0, base + 2 * hb + h)),
                  pl.BlockSpec((None, s, LANES), lambda bi, h: (bi, 0, 0)),
                  pl.BlockSpec((None, hb, s), lambda bi, h: (bi, 0, 0)),
                  pl.BlockSpec((slab, wn), lambda bi, h: (bi * hb + h, 0))],
        out_specs=[pl.BlockSpec((None, s, HEAD_DIM), lambda bi, h: (bi, 0, h)),
                   pl.BlockSpec((slab, wn), lambda bi, h: (bi * hb + h, 0))],
        out_shape=[jax.ShapeDtypeStruct((b, s, D_B), BF16),
                   jax.ShapeDtypeStruct((wk, wn), BF16)],
        compiler_params=_cparams(2),
        name="attn_fox",
    )(z, z, z, f_col, f_row, w_cast)


def _rel_bias_rows(rel_bias):
    far = rel_bias[:, 2 * REL_CLIP:]
    n_head = PAD_A - REL_CLIP
    n_tail = REL_ROW - n_head - (2 * REL_CLIP + 1)
    rows = jnp.concatenate([jnp.tile(far, (1, n_head)), jnp.flip(rel_bias, axis=1),
                            jnp.tile(far, (1, n_tail))], axis=1)
    return rows[:, None, :]


def kernel(x, p, g_ffn1, w_ffn1_gu, w_ffn1_down, g_mix, w_in, b_forget, rel_bias, w_out,
           g_ffn2, w_ffn2_gu, w_ffn2_down, g_ple, w_ple_gate, w_ple_proj, g_final):
    b, s, d = x.shape
    m = b * s
    depth = p.shape[0]
    h = x.reshape(m, d)
    for i in range(depth):
        xn = _rmsnorm(h, g_ffn1[i], BF16)
        hid, w_down = _gateup(xn, w_ffn1_gu[i], w_ffn1_down[i])
        h, u = _down(hid, w_down, h, g_mix[i])
        w_in_t = jnp.swapaxes(w_in[i], 0, 1)
        z = _proj(u, w_in_t, D_QKV).reshape(b, s, D_QKV)
        b_f = jnp.pad(b_forget[i], (0, LANES - N_HEADS_B)).reshape(1, LANES)
        f_col, f_row = _fox_prep(u.reshape(b, s, d), w_in_t, b_f)
        o_a, w_out_bf = _attn_a(z, _rel_bias_rows(rel_bias[i]), w_out[i])
        o_b, w_gate_bf = _fox(z, f_col, f_row, w_ple_gate[i])
        h, xn = _outproj(o_a.reshape(m, D_A), o_b.reshape(m, D_B), w_out_bf, h, g_ffn2[i])
        hid, w_down = _gateup(xn, w_ffn2_gu[i], w_ffn2_down[i])
        h, hn = _down(hid, w_down, h, g_ple[i])
        h = _ple(hn, p[i].reshape(m, D_PLE), w_gate_bf, w_ple_proj[i], h, g_final, i == depth - 1)
    return h.reshape(b, s, d)
```

```python
import functools

import jax
import jax.numpy as jnp
from jax import lax
from jax.experimental import pallas as pl
from jax.experimental.pallas import tpu as pltpu

F32 = jnp.float32
BF16 = jnp.bfloat16

D_MODEL = 2048
CHUNK = 64
N_LEFT_CHUNKS = 8
HEAD_DIM = 128
N_HEADS_A = 8
N_HEADS_B = 8
D_A = N_HEADS_A * HEAD_DIM
D_B = N_HEADS_B * HEAD_DIM
REL_CLIP = 256
D_FF = 5632
D_PLE = 256
EPS = 1e-6
NEG_INF = -1e30
D_QKV = 3 * D_A + 3 * D_B
SCALE = HEAD_DIM ** -0.5
LOG2E = 1.4426950408889634

LANES = 128
VMEM_LIMIT = 56 * 1024 * 1024

QB_A = 2 * CHUNK
BAND_A = (N_LEFT_CHUNKS + 2) * CHUNK
PAD_A = N_LEFT_CHUNKS * CHUNK
REL_ROW = 1024

GATEUP_ROWS = 1024
NORM_ROWS = 512
TQ_B = 256
HEADS_PER_STEP = 2
QK_AHEAD = 2


def _cparams(n_axes):
    return pltpu.CompilerParams(
        dimension_semantics=("arbitrary",) * n_axes,
        vmem_limit_bytes=VMEM_LIMIT,
    )


def _rms(x, g):
    ms = jnp.mean(x * x, axis=-1, keepdims=True)
    return (x * lax.rsqrt(ms + EPS)) * g


def _dot(a, b):
    return jnp.dot(a, b, preferred_element_type=F32)


def _gateup_kernel(n_norm, *refs):
    if n_norm:
        x_ref, g_ref, wg_ref, wu_ref, wd_ref, o_ref, wd_o_ref, w_s, a_ref = refs
    else:
        a_ref, wg_ref, wu_ref, wd_ref, o_ref, wd_o_ref, w_s = refs
    m, tn = o_ref.shape

    def matmul_phase():
        w_s[:, :tn] = wg_ref[...].astype(BF16)
        w_s[:, tn:] = wu_ref[...].astype(BF16)

        wd_o_ref[...] = wd_ref[...].astype(BF16)

        w = w_s[...]
        for c in range(m // GATEUP_ROWS):
            rows = slice(c * GATEUP_ROWS, (c + 1) * GATEUP_ROWS)
            r = _dot(a_ref[rows, :], w)
            g = r[:, :tn]
            u = r[:, tn:]
            o_ref[rows, :] = ((g * jax.nn.sigmoid(g)) * u).astype(o_ref.dtype)

    if n_norm:
        step = pl.program_id(0)

        @pl.when(step < n_norm)
        def _():
            r0 = pl.multiple_of(step * NORM_ROWS, NORM_ROWS)
            a_ref[pl.ds(r0, NORM_ROWS), :] = _rms(x_ref[...], g_ref[...]).astype(BF16)

        pl.when(step >= n_norm)(matmul_phase)
    else:
        matmul_phase()


def _gateup(a, g_norm, w_gu, w_down, tn=128):
    m, k = a.shape
    nj = D_FF // tn
    n = w_down.shape[1]
    slab = D_FF // nj
    n_norm = 0 if g_norm is None else m // NORM_ROWS

    def col(s):
        return jnp.maximum(s - n_norm, 0)

    w_specs = [pl.BlockSpec((k, tn), lambda s: (0, col(s))),
               pl.BlockSpec((k, tn), lambda s: (0, col(s) + nj)),
               pl.BlockSpec((slab, n), lambda s: (col(s), 0))]
    if n_norm:
        operands = (a, g_norm.reshape(1, k), w_gu, w_gu, w_down)
        in_specs = [pl.BlockSpec((NORM_ROWS, k), lambda s: (jnp.minimum(s, n_norm - 1), 0)),
                    pl.BlockSpec((1, k), lambda s: (0, 0))] + w_specs
        scratch = [pltpu.VMEM((k, 2 * tn), BF16), pltpu.VMEM((m, k), BF16)]
    else:
        operands = (a, w_gu, w_gu, w_down)
        in_specs = [pl.BlockSpec((m, k), lambda s: (0, 0), pipeline_mode=pl.Buffered(1))] + w_specs
        scratch = [pltpu.VMEM((k, 2 * tn), BF16)]
    return pl.pallas_call(
        functools.partial(_gateup_kernel, n_norm),
        grid=(n_norm + nj,),
        in_specs=in_specs,
        out_specs=[pl.BlockSpec((m, tn), lambda s: (0, col(s))),
                   pl.BlockSpec((slab, n), lambda s: (col(s), 0))],
        out_shape=[jax.ShapeDtypeStruct((m, D_FF), BF16),
                   jax.ShapeDtypeStruct((D_FF, n), BF16)],
        scratch_shapes=scratch,
        compiler_params=_cparams(1),
        name="ffn_gateup",
    )(*operands)


def _proj_kernel(a_ref, wt_ref, o_ref, w_s):
    @pl.when(pl.program_id(1) == 0)
    def _():
        w_s[...] = jnp.transpose(wt_ref[...]).astype(BF16)

    o_ref[...] = _dot(a_ref[...], w_s[...]).astype(o_ref.dtype)


def _proj(a, w_t, n_out, tm=2048, tn=1024):
    m, k = a.shape
    return pl.pallas_call(
        _proj_kernel,
        grid=(n_out // tn, m // tm),
        in_specs=[pl.BlockSpec((tm, k), lambda j, i: (i, 0)),
                  pl.BlockSpec((tn, k), lambda j, i: (j, 0))],
        out_specs=pl.BlockSpec((tm, tn), lambda j, i: (i, j)),
        out_shape=jax.ShapeDtypeStruct((m, n_out), BF16),
        scratch_shapes=[pltpu.VMEM((k, tn), BF16)],
        compiler_params=_cparams(2),
        name="in_proj",
    )(a, w_t)


def _down_kernel(a_ref, w_ref, r_ref, g_ref, h_ref, n_ref):
    h = r_ref[...] + 0.5 * _dot(a_ref[...], w_ref[...])
    h_ref[...] = h
    n_ref[...] = _rms(h, g_ref[...]).astype(n_ref.dtype)


def _down(a, w_bf16, resid, g_next, tm=256):
    m, k = a.shape
    n = w_bf16.shape[1]
    return pl.pallas_call(
        _down_kernel,
        grid=(m // tm,),
        in_specs=[pl.BlockSpec((tm, k), lambda i: (i, 0)),
                  pl.BlockSpec((k, n), lambda i: (0, 0), pipeline_mode=pl.Buffered(1)),
                  pl.BlockSpec((tm, n), lambda i: (i, 0)),
                  pl.BlockSpec((1, n), lambda i: (0, 0))],
        out_specs=[pl.BlockSpec((tm, n), lambda i: (i, 0)),
                   pl.BlockSpec((tm, n), lambda i: (i, 0))],
        out_shape=[jax.ShapeDtypeStruct((m, n), F32), jax.ShapeDtypeStruct((m, n), BF16)],
        compiler_params=_cparams(1),
        name="ffn_down",
    )(a, w_bf16, resid, g_next.reshape(1, n))


def _outproj_kernel(oa_ref, ob_ref, w_ref, r_ref, g_ref, h_ref, n_ref):
    acc = _dot(oa_ref[...], w_ref[:D_A, :]) + _dot(ob_ref[...], w_ref[D_A:, :])
    h = r_ref[...] + acc
    h_ref[...] = h
    n_ref[...] = _rms(h, g_ref[...]).astype(n_ref.dtype)


def _outproj(o_a, o_b, w_out, resid, g_next, tm=512):
    m = o_a.shape[0]
    k, n = w_out.shape
    return pl.pallas_call(
        _outproj_kernel,
        grid=(m // tm,),
        in_specs=[pl.BlockSpec((tm, D_A), lambda i: (i, 0)),
                  pl.BlockSpec((tm, D_B), lambda i: (i, 0)),
                  pl.BlockSpec((k, n), lambda i: (0, 0), pipeline_mode=pl.Buffered(1)),
                  pl.BlockSpec((tm, n), lambda i: (i, 0)),
                  pl.BlockSpec((1, n), lambda i: (0, 0))],
        out_specs=[pl.BlockSpec((tm, n), lambda i: (i, 0)),
                   pl.BlockSpec((tm, n), lambda i: (i, 0))],
        out_shape=[jax.ShapeDtypeStruct((m, n), F32), jax.ShapeDtypeStruct((m, n), BF16)],
        compiler_params=_cparams(1),
        name="out_proj",
    )(o_a, o_b, w_out, resid, g_next.reshape(1, n))


def _ple_kernel(close_block, a_ref, p_ref, wg_ref, wp_ref, r_ref, g_ref, o_ref, wp_s):
    @pl.when(pl.program_id(0) == 0)
    def _():
        wp_s[...] = wp_ref[...].astype(BF16)

    gate = jax.nn.sigmoid(_dot(a_ref[...], wg_ref[...]))
    emb = _dot(p_ref[...].astype(BF16), wp_s[...])
    h = r_ref[...] + gate * emb
    o_ref[...] = _rms(h, g_ref[...]) if close_block else h


def _ple(a, p, w_gate, w_proj, resid, g_final, close_block, tm=512):
    m, k = a.shape
    n = w_gate.shape[1]
    return pl.pallas_call(
        functools.partial(_ple_kernel, close_block),
        grid=(m // tm,),
        in_specs=[pl.BlockSpec((tm, k), lambda i: (i, 0)),
                  pl.BlockSpec((tm, D_PLE), lambda i: (i, 0)),
                  pl.BlockSpec((k, n), lambda i: (0, 0), pipeline_mode=pl.Buffered(1)),
                  pl.BlockSpec((D_PLE, n), lambda i: (0, 0), pipeline_mode=pl.Buffered(1)),
                  pl.BlockSpec((tm, n), lambda i: (i, 0)),
                  pl.BlockSpec((1, n), lambda i: (0, 0))],
        out_specs=pl.BlockSpec((tm, n), lambda i: (i, 0)),
        out_shape=jax.ShapeDtypeStruct((m, n), F32),
        scratch_shapes=[pltpu.VMEM((D_PLE, n), BF16)],
        compiler_params=_cparams(1),
        name="ple",
    )(a, p, w_gate, w_proj, resid, g_final.reshape(1, n))


def _attn_a_scores(q, k):
    return lax.dot_general(q, k, (((1,), (1,)), ((), ())), preferred_element_type=F32)


def _attn_a_softmax(s, bias, key_lo):
    nk = s.shape[1]
    s = s * (SCALE * LOG2E) + bias
    row = lax.broadcasted_iota(jnp.int32, (QB_A, nk), 0)
    col = lax.broadcasted_iota(jnp.int32, (QB_A, nk), 1) + key_lo
    first = row < CHUNK
    valid = (first & (col < BAND_A - CHUNK)) | (jnp.logical_not(first) & (col >= CHUNK))
    s = jnp.where(valid, s, NEG_INF)
    mx = jnp.max(s, axis=-1, keepdims=True)
    e = jnp.exp2(s - mx)
    den = jnp.sum(e, axis=-1, keepdims=True)
    return e.astype(BF16), den


def _attn_a_kernel(g_ref, q_ref, k_ref, v_ref, w_ref, o_ref, w_o_ref, bias_s):

    @pl.when(pl.program_id(1) == 0)
    def _():
        for hh in range(HEADS_PER_STEP):
            g = jnp.broadcast_to(g_ref[hh] * LOG2E, (QB_A, REL_ROW))
            bias_s[hh] = pltpu.roll(g, 0, axis=1, stride=1, stride_axis=0)[:, :BAND_A]

    n_blocks = q_ref.shape[0] // QB_A
    items = [(m, hh) for m in range(n_blocks) for hh in range(HEADS_PER_STEP)]

    def window(m):
        k0 = max(m * QB_A - PAD_A, 0)
        return k0, (m + 1) * QB_A

    def lanes(hh):
        return slice(hh * HEAD_DIM, (hh + 1) * HEAD_DIM)

    def scores(m, hh):
        k0, k1 = window(m)
        return _attn_a_scores(q_ref[m * QB_A:k1, lanes(hh)], k_ref[k0:k1, lanes(hh)])

    def finish(m, hh, e, den):
        k0, k1 = window(m)
        o = _dot(e, v_ref[k0:k1, lanes(hh)])
        o_ref[m * QB_A:k1, lanes(hh)] = (o / den).astype(o_ref.dtype)

    s_queue = [scores(*items[t]) for t in range(QK_AHEAD)]
    pending = None
    for t, (m, hh) in enumerate(items):
        s = s_queue.pop(0)
        if t + QK_AHEAD < len(items):
            s_queue.append(scores(*items[t + QK_AHEAD]))
        k0, k1 = window(m)
        lo = BAND_A - (k1 - k0)
        e, den = _attn_a_softmax(s, bias_s[hh, :, lo:], lo)
        if pending is not None:
            finish(*pending)
        pending = (m, hh, e, den)
    finish(*pending)

    w_o_ref[...] = w_ref[...].astype(BF16)


def _attn_a(z, bias_rows, w_cast):
    b, s, _ = z.shape
    hp = HEADS_PER_STEP
    n_groups = N_HEADS_A // hp
    width = hp * HEAD_DIM
    wk, wn = w_cast.shape
    slab = wk // (n_groups * b)
    return pl.pallas_call(
        _attn_a_kernel,
        grid=(n_groups, b),
        in_specs=[pl.BlockSpec((hp, 1, REL_ROW), lambda h, bi: (h, 0, 0)),
                  pl.BlockSpec((None, s, width), lambda h, bi: (bi, 0, h)),
                  pl.BlockSpec((None, s, width), lambda h, bi: (bi, 0, n_groups + h)),
                  pl.BlockSpec((None, s, width), lambda h, bi: (bi, 0, 2 * n_groups + h)),
                  pl.BlockSpec((slab, wn), lambda h, bi: (h * b + bi, 0))],
        out_specs=[pl.BlockSpec((None, s, width), lambda h, bi: (bi, 0, h)),
                   pl.BlockSpec((slab, wn), lambda h, bi: (h * b + bi, 0))],
        out_shape=[jax.ShapeDtypeStruct((b, s, D_A), BF16),
                   jax.ShapeDtypeStruct((wk, wn), BF16)],
        scratch_shapes=[pltpu.VMEM((hp, QB_A, BAND_A), F32)],
        compiler_params=_cparams(2),
        name="attn_band",
    )(bias_rows, z, z, z, w_cast)


def _fox_prep_kernel(u_ref, wft_ref, bf_ref, f_ref, ft_ref):
    seq, d = u_ref.shape
    wft = jnp.concatenate([wft_ref[...], jnp.zeros((LANES - N_HEADS_B, d), F32)], axis=0).astype(BF16)
    fl = lax.dot_general(u_ref[...], wft, (((1,), (1,)), ((), ())), preferred_element_type=F32)
    fl = fl + bf_ref[...]
    x = jnp.minimum(fl, 0.0) - jnp.log1p(jnp.exp(-jnp.abs(fl)))
    row = lax.broadcasted_iota(jnp.int32, x.shape, 0)
    shift = 1
    while shift < seq:
        x = x + jnp.where(row >= shift, pltpu.roll(x, shift, axis=0), 0.0)
        shift *= 2
    x = x * LOG2E
    f_ref[...] = x
    ft_ref[...] = jnp.transpose(x)[:N_HEADS_B, :]


def _fox_prep(u3, w_in_t, b_f):
    b, s, d = u3.shape
    return pl.pallas_call(
        _fox_prep_kernel,
        grid=(b,),
        in_specs=[pl.BlockSpec((None, s, d), lambda bi: (bi, 0, 0)),
                  pl.BlockSpec((N_HEADS_B, d), lambda bi: (D_QKV // N_HEADS_B, 0)),
                  pl.BlockSpec((1, LANES), lambda bi: (0, 0))],
        out_specs=[pl.BlockSpec((None, s, LANES), lambda bi: (bi, 0, 0)),
                   pl.BlockSpec((None, N_HEADS_B, s), lambda bi: (bi, 0, 0))],
        out_shape=[jax.ShapeDtypeStruct((b, s, LANES), F32),
                   jax.ShapeDtypeStruct((b, N_HEADS_B, s), F32)],
        compiler_params=_cparams(1),
        name="fox_prep",
    )(u3, w_in_t, b_f)


def _fox_kernel(q_ref, k_ref, v_ref, fq_ref, fk_ref, w_ref, o_ref, w_o_ref, v_aug):
    seq = q_ref.shape[0]
    h = pl.program_id(1)
    lane = lax.broadcasted_iota(jnp.int32, (TQ_B, LANES), 1)
    row = lax.broadcasted_iota(jnp.int32, (TQ_B, TQ_B), 0)
    col = lax.broadcasted_iota(jnp.int32, (TQ_B, TQ_B), 1)
    nt = (((1,), (1,)), ((), ()))

    v_aug[:, :HEAD_DIM] = v_ref[...]
    v_aug[:, HEAD_DIM:] = jnp.ones((seq, HEAD_DIM), BF16)

    n_blocks = seq // TQ_B

    def scores(qi):
        q0, q1 = qi * TQ_B, (qi + 1) * TQ_B
        q = q_ref[q0:q1, :]
        s_diag = lax.dot_general(q, k_ref[q0:q1, :], nt, preferred_element_type=F32)
        s_past = lax.dot_general(q, k_ref[0:q0, :], nt, preferred_element_type=F32) if qi > 0 else None
        return s_diag, s_past

    s_next = scores(0)
    for qi in range(n_blocks):
        q0, q1 = qi * TQ_B, (qi + 1) * TQ_B
        s_d, s_p = s_next
        if qi + 1 < n_blocks:
            s_next = scores(qi + 1)
        fq = jnp.sum(jnp.where(lane == h, fq_ref[q0:q1, :], 0.0), axis=1, keepdims=True)
        s_d = (s_d * (SCALE * LOG2E) + fq) - fk_ref[pl.ds(h, 1), q0:q1]
        s_d = jnp.where(row >= col, s_d, NEG_INF)
        mx = jnp.max(s_d, axis=-1, keepdims=True)
        if qi > 0:
            s_p = (s_p * (SCALE * LOG2E) + fq) - fk_ref[pl.ds(h, 1), 0:q0]
            mx = jnp.maximum(mx, jnp.max(s_p, axis=-1, keepdims=True))
        acc = _dot(jnp.exp2(s_d - mx).astype(BF16), v_aug[q0:q1, :])
        if qi > 0:
            acc = acc + _dot(jnp.exp2(s_p - mx).astype(BF16), v_aug[0:q0, :])
        o_ref[q0:q1, :] = (acc[:, :HEAD_DIM] / acc[:, HEAD_DIM:]).astype(o_ref.dtype)

    w_o_ref[...] = w_ref[...].astype(BF16)


def _fox(z, f_col, f_row, w_cast):
    b, s, _ = z.shape
    hb = N_HEADS_B
    base = 3 * N_HEADS_A
    wk, wn = w_cast.shape
    slab = wk // (b * hb)
    return pl.pallas_call(
        _fox_kernel,
        grid=(b, hb),
        in_specs=[pl.BlockSpec((None, s, HEAD_DIM), lambda bi, h: (bi, 0, base + h)),
                  pl.BlockSpec((None, s, HEAD_DIM), lambda bi, h: (bi, 0, base + hb + h)),
                  pl.BlockSpec((None, s, HEAD_DIM), lambda bi, h: (bi, 0, base + 2 * hb + h)),
                  pl.BlockSpec((None, s, LANES), lambda bi, h: (bi, 0, 0)),
                  pl.BlockSpec((None, hb, s), lambda bi, h: (bi, 0, 0)),
                  pl.BlockSpec((slab, wn), lambda bi, h: (bi * hb + h, 0))],
        out_specs=[pl.BlockSpec((None, s, HEAD_DIM), lambda bi, h: (bi, 0, h)),
                   pl.BlockSpec((slab, wn), lambda bi, h: (bi * hb + h, 0))],
        out_shape=[jax.ShapeDtypeStruct((b, s, D_B), BF16),
                   jax.ShapeDtypeStruct((wk, wn), BF16)],
        scratch_shapes=[pltpu.VMEM((s, 2 * HEAD_DIM), BF16)],
        compiler_params=_cparams(2),
        name="attn_fox",
    )(z, z, z, f_col, f_row, w_cast)


def _rel_bias_rows(rel_bias):
    far = rel_bias[:, 2 * REL_CLIP:]
    n_head = PAD_A - REL_CLIP
    n_tail = REL_ROW - n_head - (2 * REL_CLIP + 1)
    rows = jnp.concatenate([jnp.tile(far, (1, n_head)), jnp.flip(rel_bias, axis=1),
                            jnp.tile(far, (1, n_tail))], axis=1)
    return rows[:, None, :]


def kernel(x, p, g_ffn1, w_ffn1_gu, w_ffn1_down, g_mix, w_in, b_forget, rel_bias, w_out,
           g_ffn2, w_ffn2_gu, w_ffn2_down, g_ple, w_ple_gate, w_ple_proj, g_final):
    b, s, d = x.shape
    m = b * s
    depth = p.shape[0]
    h = x.reshape(m, d)
    for i in range(depth):
        hid, w_down = _gateup(h, g_ffn1[i], w_ffn1_gu[i], w_ffn1_down[i])
        h, u = _down(hid, w_down, h, g_mix[i])
        w_in_t = jnp.swapaxes(w_in[i], 0, 1)
        z = _proj(u, w_in_t, D_QKV).reshape(b, s, D_QKV)
        b_f = jnp.pad(b_forget[i], (0, LANES - N_HEADS_B)).reshape(1, LANES)
        f_col, f_row = _fox_prep(u.reshape(b, s, d), w_in_t, b_f)
        o_a, w_out_bf = _attn_a(z, _rel_bias_rows(rel_bias[i]), w_out[i])
        o_b, w_gate_bf = _fox(z, f_col, f_row, w_ple_gate[i])
        h, xn = _outproj(o_a.reshape(m, D_A), o_b.reshape(m, D_B), w_out_bf, h, g_ffn2[i])
        hid, w_down = _gateup(xn, None, w_ffn2_gu[i], w_ffn2_down[i])
        h, hn = _down(hid, w_down, h, g_ple[i])
        h = _ple(hn, p[i].reshape(m, D_PLE), w_gate_bf, w_ple_proj[i], h, g_final, i == depth - 1)
    return h.reshape(b, s, d)
```

```python
import functools

import jax
import jax.numpy as jnp
from jax import lax
from jax.experimental import pallas as pl
from jax.experimental.pallas import tpu as pltpu

F32 = jnp.float32
BF16 = jnp.bfloat16

D_MODEL = 2048
CHUNK = 64
N_LEFT_CHUNKS = 8
HEAD_DIM = 128
N_HEADS_A = 8
N_HEADS_B = 8
D_A = N_HEADS_A * HEAD_DIM
D_B = N_HEADS_B * HEAD_DIM
REL_CLIP = 256
D_FF = 5632
D_PLE = 256
EPS = 1e-6
NEG_INF = -1e30
D_QKV = 3 * D_A + 3 * D_B
SCALE = HEAD_DIM ** -0.5
LOG2E = 1.4426950408889634

LANES = 128
VMEM_LIMIT = 56 * 1024 * 1024

QB_A = 2 * CHUNK
BAND_A = (N_LEFT_CHUNKS + 2) * CHUNK
PAD_A = N_LEFT_CHUNKS * CHUNK
REL_ROW = 1024

GATEUP_ROWS = 1024
NORM_ROWS = 512
TQ_B = 256
HEADS_PER_STEP = 2
QK_AHEAD = 2


def _cparams(n_axes):
    return pltpu.CompilerParams(
        dimension_semantics=("arbitrary",) * n_axes,
        vmem_limit_bytes=VMEM_LIMIT,
    )


def _rms(x, g):
    ms = jnp.mean(x * x, axis=-1, keepdims=True)
    return (x * lax.rsqrt(ms + EPS)) * g


def _dot(a, b):
    return jnp.dot(a, b, preferred_element_type=F32)


def _gateup_kernel(n_norm, *refs):
    if n_norm:
        x_ref, g_ref, wg_ref, wu_ref, wd_ref, o_ref, wd_o_ref, w_s, a_ref = refs
    else:
        a_ref, wg_ref, wu_ref, wd_ref, o_ref, wd_o_ref, w_s = refs
    m, tn = o_ref.shape

    def matmul_phase():
        w_s[:, :tn] = wg_ref[...].astype(BF16)
        w_s[:, tn:] = wu_ref[...].astype(BF16)

        wd_o_ref[...] = wd_ref[...].astype(BF16)

        w = w_s[...]
        for c in range(m // GATEUP_ROWS):
            rows = slice(c * GATEUP_ROWS, (c + 1) * GATEUP_ROWS)
            r = _dot(a_ref[rows, :], w)
            g = r[:, :tn]
            u = r[:, tn:]
            o_ref[rows, :] = ((g * jax.nn.sigmoid(g)) * u).astype(o_ref.dtype)

    if n_norm:
        step = pl.program_id(0)

        @pl.when(step < n_norm)
        def _():
            r0 = pl.multiple_of(step * NORM_ROWS, NORM_ROWS)
            a_ref[pl.ds(r0, NORM_ROWS), :] = _rms(x_ref[...], g_ref[...]).astype(BF16)

        pl.when(step >= n_norm)(matmul_phase)
    else:
        matmul_phase()


def _gateup(a, g_norm, w_gu, w_down, tn=128):
    m, k = a.shape
    nj = D_FF // tn
    n = w_down.shape[1]
    slab = D_FF // nj
    n_norm = 0 if g_norm is None else m // NORM_ROWS

    def col(s):
        return jnp.maximum(s - n_norm, 0)

    w_specs = [pl.BlockSpec((k, tn), lambda s: (0, col(s))),
               pl.BlockSpec((k, tn), lambda s: (0, col(s) + nj)),
               pl.BlockSpec((slab, n), lambda s: (col(s), 0))]
    if n_norm:
        operands = (a, g_norm.reshape(1, k), w_gu, w_gu, w_down)
        in_specs = [pl.BlockSpec((NORM_ROWS, k), lambda s: (jnp.minimum(s, n_norm - 1), 0)),
                    pl.BlockSpec((1, k), lambda s: (0, 0))] + w_specs
        scratch = [pltpu.VMEM((k, 2 * tn), BF16), pltpu.VMEM((m, k), BF16)]
    else:
        operands = (a, w_gu, w_gu, w_down)
        in_specs = [pl.BlockSpec((m, k), lambda s: (0, 0), pipeline_mode=pl.Buffered(1))] + w_specs
        scratch = [pltpu.VMEM((k, 2 * tn), BF16)]
    return pl.pallas_call(
        functools.partial(_gateup_kernel, n_norm),
        grid=(n_norm + nj,),
        in_specs=in_specs,
        out_specs=[pl.BlockSpec((m, tn), lambda s: (0, col(s))),
                   pl.BlockSpec((slab, n), lambda s: (col(s), 0))],
        out_shape=[jax.ShapeDtypeStruct((m, D_FF), BF16),
                   jax.ShapeDtypeStruct((D_FF, n), BF16)],
        scratch_shapes=scratch,
        compiler_params=_cparams(1),
        name="ffn_gateup",
    )(*operands)


def _proj_kernel(a_ref, wt_ref, o_ref, w_s):
    @pl.when(pl.program_id(1) == 0)
    def _():
        w_s[...] = jnp.transpose(wt_ref[...]).astype(BF16)

    o_ref[...] = _dot(a_ref[...], w_s[...]).astype(o_ref.dtype)


def _proj(a, w_t, n_out, tm=2048, tn=1024):
    m, k = a.shape
    return pl.pallas_call(
        _proj_kernel,
        grid=(n_out // tn, m // tm),
        in_specs=[pl.BlockSpec((tm, k), lambda j, i: (i, 0)),
                  pl.BlockSpec((tn, k), lambda j, i: (j, 0))],
        out_specs=pl.BlockSpec((tm, tn), lambda j, i: (i, j)),
        out_shape=jax.ShapeDtypeStruct((m, n_out), BF16),
        scratch_shapes=[pltpu.VMEM((k, tn), BF16)],
        compiler_params=_cparams(2),
        name="in_proj",
    )(a, w_t)


def _down_kernel(a_ref, w_ref, r_ref, g_ref, h_ref, n_ref):
    h = r_ref[...] + 0.5 * _dot(a_ref[...], w_ref[...])
    h_ref[...] = h
    n_ref[...] = _rms(h, g_ref[...]).astype(n_ref.dtype)


def _down(a, w_bf16, resid, g_next, tm=256):
    m, k = a.shape
    n = w_bf16.shape[1]
    return pl.pallas_call(
        _down_kernel,
        grid=(m // tm,),
        in_specs=[pl.BlockSpec((tm, k), lambda i: (i, 0)),
                  pl.BlockSpec((k, n), lambda i: (0, 0), pipeline_mode=pl.Buffered(1)),
                  pl.BlockSpec((tm, n), lambda i: (i, 0)),
                  pl.BlockSpec((1, n), lambda i: (0, 0))],
        out_specs=[pl.BlockSpec((tm, n), lambda i: (i, 0)),
                   pl.BlockSpec((tm, n), lambda i: (i, 0))],
        out_shape=[jax.ShapeDtypeStruct((m, n), F32), jax.ShapeDtypeStruct((m, n), BF16)],
        compiler_params=_cparams(1),
        name="ffn_down",
    )(a, w_bf16, resid, g_next.reshape(1, n))


def _outproj_kernel(oa_ref, ob_ref, w_ref, r_ref, g_ref, h_ref, n_ref):
    acc = _dot(oa_ref[...], w_ref[:D_A, :]) + _dot(ob_ref[...], w_ref[D_A:, :])
    h = r_ref[...] + acc
    h_ref[...] = h
    n_ref[...] = _rms(h, g_ref[...]).astype(n_ref.dtype)


def _outproj(o_a, o_b, w_out, resid, g_next, tm=512):
    m = o_a.shape[0]
    k, n = w_out.shape
    return pl.pallas_call(
        _outproj_kernel,
        grid=(m // tm,),
        in_specs=[pl.BlockSpec((tm, D_A), lambda i: (i, 0)),
                  pl.BlockSpec((tm, D_B), lambda i: (i, 0)),
                  pl.BlockSpec((k, n), lambda i: (0, 0), pipeline_mode=pl.Buffered(1)),
                  pl.BlockSpec((tm, n), lambda i: (i, 0)),
                  pl.BlockSpec((1, n), lambda i: (0, 0))],
        out_specs=[pl.BlockSpec((tm, n), lambda i: (i, 0)),
                   pl.BlockSpec((tm, n), lambda i: (i, 0))],
        out_shape=[jax.ShapeDtypeStruct((m, n), F32), jax.ShapeDtypeStruct((m, n), BF16)],
        compiler_params=_cparams(1),
        name="out_proj",
    )(o_a, o_b, w_out, resid, g_next.reshape(1, n))


def _ple_kernel(close_block, a_ref, p_ref, wg_ref, wp_ref, r_ref, g_ref, o_ref, wp_s):
    @pl.when(pl.program_id(0) == 0)
    def _():
        wp_s[...] = wp_ref[...].astype(BF16)

    gate = jax.nn.sigmoid(_dot(a_ref[...], wg_ref[...]))
    emb = _dot(p_ref[...].astype(BF16), wp_s[...])
    h = r_ref[...] + gate * emb
    o_ref[...] = _rms(h, g_ref[...]) if close_block else h


def _ple(a, p, w_gate, w_proj, resid, g_final, close_block, tm=512):
    m, k = a.shape
    n = w_gate.shape[1]
    return pl.pallas_call(
        functools.partial(_ple_kernel, close_block),
        grid=(m // tm,),
        in_specs=[pl.BlockSpec((tm, k), lambda i: (i, 0)),
                  pl.BlockSpec((tm, D_PLE), lambda i: (i, 0)),
                  pl.BlockSpec((k, n), lambda i: (0, 0), pipeline_mode=pl.Buffered(1)),
                  pl.BlockSpec((D_PLE, n), lambda i: (0, 0), pipeline_mode=pl.Buffered(1)),
                  pl.BlockSpec((tm, n), lambda i: (i, 0)),
                  pl.BlockSpec((1, n), lambda i: (0, 0))],
        out_specs=pl.BlockSpec((tm, n), lambda i: (i, 0)),
        out_shape=jax.ShapeDtypeStruct((m, n), F32),
        scratch_shapes=[pltpu.VMEM((D_PLE, n), BF16)],
        compiler_params=_cparams(1),
        name="ple",
    )(a, p, w_gate, w_proj, resid, g_final.reshape(1, n))


def _attn_a_scores(q, k):
    return lax.dot_general(q, k, (((1,), (1,)), ((), ())), preferred_element_type=F32)


def _attn_a_softmax(s, bias, key_lo):
    nk = s.shape[1]
    s = s * (SCALE * LOG2E) + bias
    row = lax.broadcasted_iota(jnp.int32, (QB_A, nk), 0)
    col = lax.broadcasted_iota(jnp.int32, (QB_A, nk), 1) + key_lo
    first = row < CHUNK
    valid = (first & (col < BAND_A - CHUNK)) | (jnp.logical_not(first) & (col >= CHUNK))
    s = jnp.where(valid, s, NEG_INF)
    mx = jnp.max(s, axis=-1, keepdims=True)
    e = jnp.exp2(s - mx)
    den = jnp.sum(e, axis=-1, keepdims=True)
    return e.astype(BF16), den


def _attn_a_kernel(g_ref, q_ref, k_ref, v_ref, w_ref, o_ref, w_o_ref, bias_s):

    @pl.when(pl.program_id(1) == 0)
    def _():
        for hh in range(HEADS_PER_STEP):
            g = jnp.broadcast_to(g_ref[hh] * LOG2E, (QB_A, REL_ROW))
            bias_s[hh] = pltpu.roll(g, 0, axis=1, stride=1, stride_axis=0)[:, :BAND_A]

    n_blocks = q_ref.shape[0] // QB_A
    items = [(m, hh) for m in range(n_blocks) for hh in range(HEADS_PER_STEP)]

    def window(m):
        k0 = max(m * QB_A - PAD_A, 0)
        return k0, (m + 1) * QB_A

    def lanes(hh):
        return slice(hh * HEAD_DIM, (hh + 1) * HEAD_DIM)

    def scores(m, hh):
        k0, k1 = window(m)
        return _attn_a_scores(q_ref[m * QB_A:k1, lanes(hh)], k_ref[k0:k1, lanes(hh)])

    def finish(m, hh, e, den):
        k0, k1 = window(m)
        o = _dot(e, v_ref[k0:k1, lanes(hh)])
        o_ref[m * QB_A:k1, lanes(hh)] = (o / den).astype(o_ref.dtype)

    s_queue = [scores(*items[t]) for t in range(QK_AHEAD)]
    pending = None
    for t, (m, hh) in enumerate(items):
        s = s_queue.pop(0)
        if t + QK_AHEAD < len(items):
            s_queue.append(scores(*items[t + QK_AHEAD]))
        k0, k1 = window(m)
        lo = BAND_A - (k1 - k0)
        e, den = _attn_a_softmax(s, bias_s[hh, :, lo:], lo)
        if pending is not None:
            finish(*pending)
        pending = (m, hh, e, den)
    finish(*pending)

    w_o_ref[...] = w_ref[...].astype(BF16)


def _attn_a(z, bias_rows, w_cast):
    b, s, _ = z.shape
    hp = HEADS_PER_STEP
    n_groups = N_HEADS_A // hp
    width = hp * HEAD_DIM
    wk, wn = w_cast.shape
    slab = wk // (n_groups * b)
    return pl.pallas_call(
        _attn_a_kernel,
        grid=(n_groups, b),
        in_specs=[pl.BlockSpec((hp, 1, REL_ROW), lambda h, bi: (h, 0, 0)),
                  pl.BlockSpec((None, s, width), lambda h, bi: (bi, 0, h)),
                  pl.BlockSpec((None, s, width), lambda h, bi: (bi, 0, n_groups + h)),
                  pl.BlockSpec((None, s, width), lambda h, bi: (bi, 0, 2 * n_groups + h)),
                  pl.BlockSpec((slab, wn), lambda h, bi: (h * b + bi, 0))],
        out_specs=[pl.BlockSpec((None, s, width), lambda h, bi: (bi, 0, h)),
                   pl.BlockSpec((slab, wn), lambda h, bi: (h * b + bi, 0))],
        out_shape=[jax.ShapeDtypeStruct((b, s, D_A), BF16),
                   jax.ShapeDtypeStruct((wk, wn), BF16)],
        scratch_shapes=[pltpu.VMEM((hp, QB_A, BAND_A), F32)],
        compiler_params=_cparams(2),
        name="attn_band",
    )(bias_rows, z, z, z, w_cast)


def _fox_prep_kernel(u_ref, wft_ref, bf_ref, f_ref, ft_ref):
    seq, d = u_ref.shape
    wft = jnp.concatenate([wft_ref[...], jnp.zeros((LANES - N_HEADS_B, d), F32)], axis=0).astype(BF16)
    fl = lax.dot_general(u_ref[...], wft, (((1,), (1,)), ((), ())), preferred_element_type=F32)
    fl = fl + bf_ref[...]
    x = jnp.minimum(fl, 0.0) - jnp.log1p(jnp.exp(-jnp.abs(fl)))
    row = lax.broadcasted_iota(jnp.int32, x.shape, 0)
    shift = 1
    while shift < seq:
        x = x + jnp.where(row >= shift, pltpu.roll(x, shift, axis=0), 0.0)
        shift *= 2
    x = x * LOG2E
    f_ref[...] = x
    ft_ref[...] = jnp.transpose(x)[:N_HEADS_B, :]


def _fox_prep(u3, w_in_t, b_f):
    b, s, d = u3.shape
    return pl.pallas_call(
        _fox_prep_kernel,
        grid=(b,),
        in_specs=[pl.BlockSpec((None, s, d), lambda bi: (bi, 0, 0)),
                  pl.BlockSpec((N_HEADS_B, d), lambda bi: (D_QKV // N_HEADS_B, 0)),
                  pl.BlockSpec((1, LANES), lambda bi: (0, 0))],
        out_specs=[pl.BlockSpec((None, s, LANES), lambda bi: (bi, 0, 0)),
                   pl.BlockSpec((None, N_HEADS_B, s), lambda bi: (bi, 0, 0))],
        out_shape=[jax.ShapeDtypeStruct((b, s, LANES), F32),
                   jax.ShapeDtypeStruct((b, N_HEADS_B, s), F32)],
        compiler_params=_cparams(1),
        name="fox_prep",
    )(u3, w_in_t, b_f)


def _fox_kernel(q_ref, k_ref, v_ref, fq_ref, fk_ref, w_ref, o_ref, w_o_ref, v_aug):
    seq = q_ref.shape[0]
    n_blocks = seq // TQ_B
    group = pl.program_id(1)
    lane = lax.broadcasted_iota(jnp.int32, (TQ_B, LANES), 1)
    row = lax.broadcasted_iota(jnp.int32, (TQ_B, TQ_B), 0)
    col = lax.broadcasted_iota(jnp.int32, (TQ_B, TQ_B), 1)
    nt = (((1,), (1,)), ((), ()))

    def lanes(hh):
        return slice(hh * HEAD_DIM, (hh + 1) * HEAD_DIM)

    for hh in range(HEADS_PER_STEP):
        v_aug[hh, :, :HEAD_DIM] = v_ref[:, lanes(hh)]
        v_aug[hh, :, HEAD_DIM:] = jnp.ones((seq, HEAD_DIM), BF16)

    items = []
    for t in range(n_blocks):
        for hh in range(HEADS_PER_STEP):
            items.append((hh, t if hh % 2 == 0 else n_blocks - 1 - t))

    def scores(hh, qi):
        q0, q1 = qi * TQ_B, (qi + 1) * TQ_B
        q = q_ref[q0:q1, lanes(hh)]
        s_diag = lax.dot_general(q, k_ref[q0:q1, lanes(hh)], nt, preferred_element_type=F32)
        s_past = (lax.dot_general(q, k_ref[0:q0, lanes(hh)], nt, preferred_element_type=F32)
                  if qi > 0 else None)
        return s_diag, s_past

    def softmax(hh, qi, s_d, s_p):
        q0, q1 = qi * TQ_B, (qi + 1) * TQ_B
        h = group * HEADS_PER_STEP + hh
        fq = jnp.sum(jnp.where(lane == h, fq_ref[q0:q1, :], 0.0), axis=1, keepdims=True)
        s_d = (s_d * (SCALE * LOG2E) + fq) - fk_ref[pl.ds(h, 1), q0:q1]
        s_d = jnp.where(row >= col, s_d, NEG_INF)
        mx = jnp.max(s_d, axis=-1, keepdims=True)
        if qi > 0:
            s_p = (s_p * (SCALE * LOG2E) + fq) - fk_ref[pl.ds(h, 1), 0:q0]
            mx = jnp.maximum(mx, jnp.max(s_p, axis=-1, keepdims=True))
        e_d = jnp.exp2(s_d - mx).astype(BF16)
        e_p = jnp.exp2(s_p - mx).astype(BF16) if qi > 0 else None
        return e_d, e_p

    def finish(hh, qi, e_d, e_p):
        q0, q1 = qi * TQ_B, (qi + 1) * TQ_B
        acc = _dot(e_d, v_aug[hh, q0:q1, :])
        if e_p is not None:
            acc = acc + _dot(e_p, v_aug[hh, 0:q0, :])
        o_ref[q0:q1, lanes(hh)] = (acc[:, :HEAD_DIM] / acc[:, HEAD_DIM:]).astype(o_ref.dtype)

    s_next = scores(*items[0])
    pending = None
    for t, (hh, qi) in enumerate(items):
        s_d, s_p = s_next
        if t + 1 < len(items):
            s_next = scores(*items[t + 1])
        e_d, e_p = softmax(hh, qi, s_d, s_p)
        if pending is not None:
            finish(*pending)
        pending = (hh, qi, e_d, e_p)
    finish(*pending)

    w_o_ref[...] = w_ref[...].astype(BF16)


def _fox(z, f_col, f_row, w_cast):
    b, s, _ = z.shape
    hp = HEADS_PER_STEP
    n_groups = N_HEADS_B // hp
    width = hp * HEAD_DIM
    base = 3 * N_HEADS_A // hp
    wk, wn = w_cast.shape
    slab = wk // (b * n_groups)
    return pl.pallas_call(
        _fox_kernel,
        grid=(b, n_groups),
        in_specs=[pl.BlockSpec((None, s, width), lambda bi, h: (bi, 0, base + h)),
                  pl.BlockSpec((None, s, width), lambda bi, h: (bi, 0, base + n_groups + h)),
                  pl.BlockSpec((None, s, width), lambda bi, h: (bi, 0, base + 2 * n_groups + h)),
                  pl.BlockSpec((None, s, LANES), lambda bi, h: (bi, 0, 0)),
                  pl.BlockSpec((None, N_HEADS_B, s), lambda bi, h: (bi, 0, 0)),
                  pl.BlockSpec((slab, wn), lambda bi, h: (bi * n_groups + h, 0))],
        out_specs=[pl.BlockSpec((None, s, width), lambda bi, h: (bi, 0, h)),
                   pl.BlockSpec((slab, wn), lambda bi, h: (bi * n_groups + h, 0))],
        out_shape=[jax.ShapeDtypeStruct((b, s, D_B), BF16),
                   jax.ShapeDtypeStruct((wk, wn), BF16)],
        scratch_shapes=[pltpu.VMEM((hp, s, 2 * HEAD_DIM), BF16)],
        compiler_params=_cparams(2),
        name="attn_fox",
    )(z, z, z, f_col, f_row, w_cast)


def _rel_bias_rows(rel_bias):
    far = rel_bias[:, 2 * REL_CLIP:]
    n_head = PAD_A - REL_CLIP
    n_tail = REL_ROW - n_head - (2 * REL_CLIP + 1)
    rows = jnp.concatenate([jnp.tile(far, (1, n_head)), jnp.flip(rel_bias, axis=1),
                            jnp.tile(far, (1, n_tail))], axis=1)
    return rows[:, None, :]


def kernel(x, p, g_ffn1, w_ffn1_gu, w_ffn1_down, g_mix, w_in, b_forget, rel_bias, w_out,
           g_ffn2, w_ffn2_gu, w_ffn2_down, g_ple, w_ple_gate, w_ple_proj, g_final):
    b, s, d = x.shape
    m = b * s
    depth = p.shape[0]
    h = x.reshape(m, d)
    for i in range(depth):
        hid, w_down = _gateup(h, g_ffn1[i], w_ffn1_gu[i], w_ffn1_down[i])
        h, u = _down(hid, w_down, h, g_mix[i])
        w_in_t = jnp.swapaxes(w_in[i], 0, 1)
        z = _proj(u, w_in_t, D_QKV).reshape(b, s, D_QKV)
        b_f = jnp.pad(b_forget[i], (0, LANES - N_HEADS_B)).reshape(1, LANES)
        f_col, f_row = _fox_prep(u.reshape(b, s, d), w_in_t, b_f)
        o_a, w_out_bf = _attn_a(z, _rel_bias_rows(rel_bias[i]), w_out[i])
        o_b, w_gate_bf = _fox(z, f_col, f_row, w_ple_gate[i])
        h, xn = _outproj(o_a.reshape(m, D_A), o_b.reshape(m, D_B), w_out_bf, h, g_ffn2[i])
        hid, w_down = _gateup(xn, None, w_ffn2_gu[i], w_ffn2_down[i])
        h, hn = _down(hid, w_down, h, g_ple[i])
        h = _ple(hn, p[i].reshape(m, D_PLE), w_gate_bf, w_ple_proj[i], h, g_final, i == depth - 1)
    return h.reshape(b, s, d)
```

```python
import functools

import jax
import jax.numpy as jnp
from jax import lax
from jax.experimental import pallas as pl
from jax.experimental.pallas import tpu as pltpu

F32 = jnp.float32
BF16 = jnp.bfloat16

D_MODEL = 2048
CHUNK = 64
N_LEFT_CHUNKS = 8
HEAD_DIM = 128
N_HEADS_A = 8
N_HEADS_B = 8
D_A = N_HEADS_A * HEAD_DIM
D_B = N_HEADS_B * HEAD_DIM
REL_CLIP = 256
D_FF = 5632
D_PLE = 256
EPS = 1e-6
NEG_INF = -1e30
D_QKV = 3 * D_A + 3 * D_B
SCALE = HEAD_DIM ** -0.5
LOG2E = 1.4426950408889634

LANES = 128
VMEM_LIMIT = 56 * 1024 * 1024

QB_A = 2 * CHUNK
BAND_A = (N_LEFT_CHUNKS + 2) * CHUNK
PAD_A = N_LEFT_CHUNKS * CHUNK
REL_ROW = 1024

GATEUP_ROWS = 1024
GATEUP_TN = 256
GATEUP_TN_FUSED = 128
NORM_ROWS = 512
TQ_B = 256
HEADS_PER_STEP = 2
QK_AHEAD = 2


def _cparams(n_axes):
    return pltpu.CompilerParams(
        dimension_semantics=("arbitrary",) * n_axes,
        vmem_limit_bytes=VMEM_LIMIT,
    )


def _rms(x, g):
    ms = jnp.mean(x * x, axis=-1, keepdims=True)
    return (x * lax.rsqrt(ms + EPS)) * g


def _dot(a, b):
    return jnp.dot(a, b, preferred_element_type=F32)


def _gateup_kernel(n_norm, has_rider, *refs):
    refs = list(refs)
    x_ref, g_ref = (refs.pop(0), refs.pop(0)) if n_norm else (None, None)
    a_ref = None if n_norm else refs.pop(0)
    wg_ref, wu_ref = refs.pop(0), refs.pop(0)
    wc_ref = refs.pop(0) if has_rider else None
    o_ref = refs.pop(0)
    wc_o_ref = refs.pop(0) if has_rider else None
    w_s = refs.pop(0)
    if n_norm:
        a_ref = refs.pop(0)
    m, tn = o_ref.shape

    def matmul_phase():
        w_s[:, :tn] = wg_ref[...].astype(BF16)
        w_s[:, tn:] = wu_ref[...].astype(BF16)

        if has_rider:
            wc_o_ref[...] = wc_ref[...].astype(BF16)

        w = w_s[...]
        for c in range(m // GATEUP_ROWS):
            rows = slice(c * GATEUP_ROWS, (c + 1) * GATEUP_ROWS)
            r = _dot(a_ref[rows, :], w)
            g = r[:, :tn]
            u = r[:, tn:]
            o_ref[rows, :] = ((g * jax.nn.sigmoid(g)) * u).astype(o_ref.dtype)

    if n_norm:
        step = pl.program_id(0)

        @pl.when(step < n_norm)
        def _():
            r0 = pl.multiple_of(step * NORM_ROWS, NORM_ROWS)
            a_ref[pl.ds(r0, NORM_ROWS), :] = _rms(x_ref[...], g_ref[...]).astype(BF16)

        pl.when(step >= n_norm)(matmul_phase)
    else:
        matmul_phase()


def _gateup(a, g_norm, w_gu, w_cast, tn):
    m, k = a.shape
    nj = D_FF // tn
    n_norm = 0 if g_norm is None else m // NORM_ROWS
    has_rider = w_cast is not None

    def col(s):
        return jnp.maximum(s - n_norm, 0)

    operands, in_specs = [], []
    scratch = [pltpu.VMEM((k, 2 * tn), BF16)]
    if n_norm:
        operands += [a, g_norm.reshape(1, k)]
        in_specs += [pl.BlockSpec((NORM_ROWS, k), lambda s: (jnp.minimum(s, n_norm - 1), 0)),
                     pl.BlockSpec((1, k), lambda s: (0, 0))]
        scratch.append(pltpu.VMEM((m, k), BF16))
    else:
        operands.append(a)
        in_specs.append(pl.BlockSpec((m, k), lambda s: (0, 0), pipeline_mode=pl.Buffered(1)))
    operands += [w_gu, w_gu]
    in_specs += [pl.BlockSpec((k, tn), lambda s: (0, col(s))),
                 pl.BlockSpec((k, tn), lambda s: (0, col(s) + nj))]
    out_specs = [pl.BlockSpec((m, tn), lambda s: (0, col(s)))]
    out_shape = [jax.ShapeDtypeStruct((m, D_FF), BF16)]
    if has_rider:
        wk, wn = w_cast.shape
        slab = wk // nj
        operands.append(w_cast)
        in_specs.append(pl.BlockSpec((slab, wn), lambda s: (col(s), 0)))
        out_specs.append(pl.BlockSpec((slab, wn), lambda s: (col(s), 0)))
        out_shape.append(jax.ShapeDtypeStruct((wk, wn), BF16))
    return pl.pallas_call(
        functools.partial(_gateup_kernel, n_norm, has_rider),
        grid=(n_norm + nj,),
        in_specs=in_specs,
        out_specs=out_specs,
        out_shape=out_shape,
        scratch_shapes=scratch,
        compiler_params=_cparams(1),
        name="ffn_gateup",
    )(*operands)


def _proj_kernel(a_ref, wt_ref, o_ref, w_s):
    @pl.when(pl.program_id(1) == 0)
    def _():
        w_s[...] = jnp.transpose(wt_ref[...]).astype(BF16)

    o_ref[...] = _dot(a_ref[...], w_s[...]).astype(o_ref.dtype)


def _proj(a, w_t, n_out, tm=2048, tn=1024):
    m, k = a.shape
    return pl.pallas_call(
        _proj_kernel,
        grid=(n_out // tn, m // tm),
        in_specs=[pl.BlockSpec((tm, k), lambda j, i: (i, 0)),
                  pl.BlockSpec((tn, k), lambda j, i: (j, 0))],
        out_specs=pl.BlockSpec((tm, tn), lambda j, i: (i, j)),
        out_shape=jax.ShapeDtypeStruct((m, n_out), BF16),
        scratch_shapes=[pltpu.VMEM((k, tn), BF16)],
        compiler_params=_cparams(2),
        name="in_proj",
    )(a, w_t)


def _down_kernel(has_rider, a_ref, w_ref, r_ref, g_ref, *refs):
    if has_rider:
        wc_ref, h_ref, n_ref, wc_o_ref = refs
    else:
        h_ref, n_ref = refs
    h = r_ref[...] + 0.5 * _dot(a_ref[...], w_ref[...])
    h_ref[...] = h
    n_ref[...] = _rms(h, g_ref[...]).astype(n_ref.dtype)
    if has_rider:
        wc_o_ref[...] = wc_ref[...].astype(BF16)


def _down(a, w_bf16, resid, g_next, w_cast, tm=256):
    m, k = a.shape
    n = w_bf16.shape[1]
    steps = m // tm
    has_rider = w_cast is not None
    operands = [a, w_bf16, resid, g_next.reshape(1, n)]
    in_specs = [pl.BlockSpec((tm, k), lambda i: (i, 0)),
                pl.BlockSpec((k, n), lambda i: (0, 0), pipeline_mode=pl.Buffered(1)),
                pl.BlockSpec((tm, n), lambda i: (i, 0)),
                pl.BlockSpec((1, n), lambda i: (0, 0))]
    out_specs = [pl.BlockSpec((tm, n), lambda i: (i, 0)),
                 pl.BlockSpec((tm, n), lambda i: (i, 0))]
    out_shape = [jax.ShapeDtypeStruct((m, n), F32), jax.ShapeDtypeStruct((m, n), BF16)]
    if has_rider:
        wk, wn = w_cast.shape
        slab = wk // steps
        operands.append(w_cast)
        in_specs.append(pl.BlockSpec((slab, wn), lambda i: (i, 0)))
        out_specs.append(pl.BlockSpec((slab, wn), lambda i: (i, 0)))
        out_shape.append(jax.ShapeDtypeStruct((wk, wn), BF16))
    return pl.pallas_call(
        functools.partial(_down_kernel, has_rider),
        grid=(steps,),
        in_specs=in_specs,
        out_specs=out_specs,
        out_shape=out_shape,
        compiler_params=_cparams(1),
        name="ffn_down",
    )(*operands)


def _outproj_kernel(oa_ref, ob_ref, w_ref, r_ref, g_ref, h_ref, n_ref):
    acc = _dot(oa_ref[...], w_ref[:D_A, :]) + _dot(ob_ref[...], w_ref[D_A:, :])
    h = r_ref[...] + acc
    h_ref[...] = h
    n_ref[...] = _rms(h, g_ref[...]).astype(n_ref.dtype)


def _outproj(o_a, o_b, w_out, resid, g_next, tm=512):
    m = o_a.shape[0]
    k, n = w_out.shape
    return pl.pallas_call(
        _outproj_kernel,
        grid=(m // tm,),
        in_specs=[pl.BlockSpec((tm, D_A), lambda i: (i, 0)),
                  pl.BlockSpec((tm, D_B), lambda i: (i, 0)),
                  pl.BlockSpec((k, n), lambda i: (0, 0), pipeline_mode=pl.Buffered(1)),
                  pl.BlockSpec((tm, n), lambda i: (i, 0)),
                  pl.BlockSpec((1, n), lambda i: (0, 0))],
        out_specs=[pl.BlockSpec((tm, n), lambda i: (i, 0)),
                   pl.BlockSpec((tm, n), lambda i: (i, 0))],
        out_shape=[jax.ShapeDtypeStruct((m, n), F32), jax.ShapeDtypeStruct((m, n), BF16)],
        compiler_params=_cparams(1),
        name="out_proj",
    )(o_a, o_b, w_out, resid, g_next.reshape(1, n))


def _ple_kernel(close_block, a_ref, p_ref, wg_ref, wp_ref, r_ref, g_ref, o_ref, wp_s):
    @pl.when(pl.program_id(0) == 0)
    def _():
        wp_s[...] = wp_ref[...].astype(BF16)

    gate = jax.nn.sigmoid(_dot(a_ref[...], wg_ref[...]))
    emb = _dot(p_ref[...].astype(BF16), wp_s[...])
    h = r_ref[...] + gate * emb
    o_ref[...] = _rms(h, g_ref[...]) if close_block else h


def _ple(a, p, w_gate, w_proj, resid, g_final, close_block, tm=512):
    m, k = a.shape
    n = w_gate.shape[1]
    return pl.pallas_call(
        functools.partial(_ple_kernel, close_block),
        grid=(m // tm,),
        in_specs=[pl.BlockSpec((tm, k), lambda i: (i, 0)),
                  pl.BlockSpec((tm, D_PLE), lambda i: (i, 0)),
                  pl.BlockSpec((k, n), lambda i: (0, 0), pipeline_mode=pl.Buffered(1)),
                  pl.BlockSpec((D_PLE, n), lambda i: (0, 0), pipeline_mode=pl.Buffered(1)),
                  pl.BlockSpec((tm, n), lambda i: (i, 0)),
                  pl.BlockSpec((1, n), lambda i: (0, 0))],
        out_specs=pl.BlockSpec((tm, n), lambda i: (i, 0)),
        out_shape=jax.ShapeDtypeStruct((m, n), F32),
        scratch_shapes=[pltpu.VMEM((D_PLE, n), BF16)],
        compiler_params=_cparams(1),
        name="ple",
    )(a, p, w_gate, w_proj, resid, g_final.reshape(1, n))


def _attn_a_scores(q, k):
    return lax.dot_general(q, k, (((1,), (1,)), ((), ())), preferred_element_type=F32)


def _attn_a_softmax(s, bias, key_lo):
    nk = s.shape[1]
    s = s * (SCALE * LOG2E) + bias
    row = lax.broadcasted_iota(jnp.int32, (QB_A, nk), 0)
    col = lax.broadcasted_iota(jnp.int32, (QB_A, nk), 1) + key_lo
    first = row < CHUNK
    valid = (first & (col < BAND_A - CHUNK)) | (jnp.logical_not(first) & (col >= CHUNK))
    s = jnp.where(valid, s, NEG_INF)
    mx = jnp.max(s, axis=-1, keepdims=True)
    e = jnp.exp2(s - mx)
    den = jnp.sum(e, axis=-1, keepdims=True)
    return e.astype(BF16), den


def _attn_a_kernel(g_ref, q_ref, k_ref, v_ref, w_ref, o_ref, w_o_ref, bias_s):

    @pl.when(pl.program_id(1) == 0)
    def _():
        for hh in range(HEADS_PER_STEP):
            g = jnp.broadcast_to(g_ref[hh] * LOG2E, (QB_A, REL_ROW))
            bias_s[hh] = pltpu.roll(g, 0, axis=1, stride=1, stride_axis=0)[:, :BAND_A]

    n_blocks = q_ref.shape[0] // QB_A
    items = [(m, hh) for m in range(n_blocks) for hh in range(HEADS_PER_STEP)]

    def window(m):
        k0 = max(m * QB_A - PAD_A, 0)
        return k0, (m + 1) * QB_A

    def lanes(hh):
        return slice(hh * HEAD_DIM, (hh + 1) * HEAD_DIM)

    def scores(m, hh):
        k0, k1 = window(m)
        return _attn_a_scores(q_ref[m * QB_A:k1, lanes(hh)], k_ref[k0:k1, lanes(hh)])

    def finish(m, hh, e, den):
        k0, k1 = window(m)
        o = _dot(e, v_ref[k0:k1, lanes(hh)])
        o_ref[m * QB_A:k1, lanes(hh)] = (o / den).astype(o_ref.dtype)

    s_queue = [scores(*items[t]) for t in range(QK_AHEAD)]
    pending = None
    for t, (m, hh) in enumerate(items):
        s = s_queue.pop(0)
        if t + QK_AHEAD < len(items):
            s_queue.append(scores(*items[t + QK_AHEAD]))
        k0, k1 = window(m)
        lo = BAND_A - (k1 - k0)
        e, den = _attn_a_softmax(s, bias_s[hh, :, lo:], lo)
        if pending is not None:
            finish(*pending)
        pending = (m, hh, e, den)
    finish(*pending)

    w_o_ref[...] = w_ref[...].astype(BF16)


def _attn_a(z, bias_rows, w_cast):
    b, s, _ = z.shape
    hp = HEADS_PER_STEP
    n_groups = N_HEADS_A // hp
    width = hp * HEAD_DIM
    wk, wn = w_cast.shape
    slab = wk // (n_groups * b)
    return pl.pallas_call(
        _attn_a_kernel,
        grid=(n_groups, b),
        in_specs=[pl.BlockSpec((hp, 1, REL_ROW), lambda h, bi: (h, 0, 0)),
                  pl.BlockSpec((None, s, width), lambda h, bi: (bi, 0, h)),
                  pl.BlockSpec((None, s, width), lambda h, bi: (bi, 0, n_groups + h)),
                  pl.BlockSpec((None, s, width), lambda h, bi: (bi, 0, 2 * n_groups + h)),
                  pl.BlockSpec((slab, wn), lambda h, bi: (h * b + bi, 0))],
        out_specs=[pl.BlockSpec((None, s, width), lambda h, bi: (bi, 0, h)),
                   pl.BlockSpec((slab, wn), lambda h, bi: (h * b + bi, 0))],
        out_shape=[jax.ShapeDtypeStruct((b, s, D_A), BF16),
                   jax.ShapeDtypeStruct((wk, wn), BF16)],
        scratch_shapes=[pltpu.VMEM((hp, QB_A, BAND_A), F32)],
        compiler_params=_cparams(2),
        name="attn_band",
    )(bias_rows, z, z, z, w_cast)


def _fox_prep_kernel(u_ref, wft_ref, bf_ref, f_ref, ft_ref):
    seq, d = u_ref.shape
    wft = jnp.concatenate([wft_ref[...], jnp.zeros((LANES - N_HEADS_B, d), F32)], axis=0).astype(BF16)
    fl = lax.dot_general(u_ref[...], wft, (((1,), (1,)), ((), ())), preferred_element_type=F32)
    fl = fl + bf_ref[...]
    x = jnp.minimum(fl, 0.0) - jnp.log1p(jnp.exp(-jnp.abs(fl)))
    row = lax.broadcasted_iota(jnp.int32, x.shape, 0)
    shift = 1
    while shift < seq:
        x = x + jnp.where(row >= shift, pltpu.roll(x, shift, axis=0), 0.0)
        shift *= 2
    x = x * LOG2E
    f_ref[...] = x
    ft_ref[...] = jnp.transpose(x)[:N_HEADS_B, :]


def _fox_prep(u3, w_in_t, b_f):
    b, s, d = u3.shape
    return pl.pallas_call(
        _fox_prep_kernel,
        grid=(b,),
        in_specs=[pl.BlockSpec((None, s, d), lambda bi: (bi, 0, 0)),
                  pl.BlockSpec((N_HEADS_B, d), lambda bi: (D_QKV // N_HEADS_B, 0)),
                  pl.BlockSpec((1, LANES), lambda bi: (0, 0))],
        out_specs=[pl.BlockSpec((None, s, LANES), lambda bi: (bi, 0, 0)),
                   pl.BlockSpec((None, N_HEADS_B, s), lambda bi: (bi, 0, 0))],
        out_shape=[jax.ShapeDtypeStruct((b, s, LANES), F32),
                   jax.ShapeDtypeStruct((b, N_HEADS_B, s), F32)],
        compiler_params=_cparams(1),
        name="fox_prep",
    )(u3, w_in_t, b_f)


def _fox_kernel(q_ref, k_ref, v_ref, fq_ref, fk_ref, w_ref, o_ref, w_o_ref, v_aug):
    seq = q_ref.shape[0]
    n_blocks = seq // TQ_B
    group = pl.program_id(1)
    lane = lax.broadcasted_iota(jnp.int32, (TQ_B, LANES), 1)
    row = lax.broadcasted_iota(jnp.int32, (TQ_B, TQ_B), 0)
    col = lax.broadcasted_iota(jnp.int32, (TQ_B, TQ_B), 1)
    nt = (((1,), (1,)), ((), ()))

    def lanes(hh):
        return slice(hh * HEAD_DIM, (hh + 1) * HEAD_DIM)

    for hh in range(HEADS_PER_STEP):
        v_aug[hh, :, :HEAD_DIM] = v_ref[:, lanes(hh)]
        v_aug[hh, :, HEAD_DIM:] = jnp.ones((seq, HEAD_DIM), BF16)

    items = []
    for t in range(n_blocks):
        for hh in range(HEADS_PER_STEP):
            items.append((hh, t if hh % 2 == 0 else n_blocks - 1 - t))

    def scores(hh, qi):
        q0, q1 = qi * TQ_B, (qi + 1) * TQ_B
        q = q_ref[q0:q1, lanes(hh)]
        s_diag = lax.dot_general(q, k_ref[q0:q1, lanes(hh)], nt, preferred_element_type=F32)
        s_past = (lax.dot_general(q, k_ref[0:q0, lanes(hh)], nt, preferred_element_type=F32)
                  if qi > 0 else None)
        return s_diag, s_past

    def softmax(hh, qi, s_d, s_p):
        q0, q1 = qi * TQ_B, (qi + 1) * TQ_B
        h = group * HEADS_PER_STEP + hh
        fq = jnp.sum(jnp.where(lane == h, fq_ref[q0:q1, :], 0.0), axis=1, keepdims=True)
        s_d = (s_d * (SCALE * LOG2E) + fq) - fk_ref[pl.ds(h, 1), q0:q1]
        s_d = jnp.where(row >= col, s_d, NEG_INF)
        mx = jnp.max(s_d, axis=-1, keepdims=True)
        if qi > 0:
            s_p = (s_p * (SCALE * LOG2E) + fq) - fk_ref[pl.ds(h, 1), 0:q0]
            mx = jnp.maximum(mx, jnp.max(s_p, axis=-1, keepdims=True))
        e_d = jnp.exp2(s_d - mx).astype(BF16)
        e_p = jnp.exp2(s_p - mx).astype(BF16) if qi > 0 else None
        return e_d, e_p

    def finish(hh, qi, e_d, e_p):
        q0, q1 = qi * TQ_B, (qi + 1) * TQ_B
        acc = _dot(e_d, v_aug[hh, q0:q1, :])
        if e_p is not None:
            acc = acc + _dot(e_p, v_aug[hh, 0:q0, :])
        o_ref[q0:q1, lanes(hh)] = (acc[:, :HEAD_DIM] / acc[:, HEAD_DIM:]).astype(o_ref.dtype)

    s_next = scores(*items[0])
    pending = None
    for t, (hh, qi) in enumerate(items):
        s_d, s_p = s_next
        if t + 1 < len(items):
            s_next = scores(*items[t + 1])
        e_d, e_p = softmax(hh, qi, s_d, s_p)
        if pending is not None:
            finish(*pending)
        pending = (hh, qi, e_d, e_p)
    finish(*pending)

    w_o_ref[...] = w_ref[...].astype(BF16)


def _fox(z, f_col, f_row, w_cast):
    b, s, _ = z.shape
    hp = HEADS_PER_STEP
    n_groups = N_HEADS_B // hp
    width = hp * HEAD_DIM
    base = 3 * N_HEADS_A // hp
    wk, wn = w_cast.shape
    slab = wk // (b * n_groups)
    return pl.pallas_call(
        _fox_kernel,
        grid=(b, n_groups),
        in_specs=[pl.BlockSpec((None, s, width), lambda bi, h: (bi, 0, base + h)),
                  pl.BlockSpec((None, s, width), lambda bi, h: (bi, 0, base + n_groups + h)),
                  pl.BlockSpec((None, s, width), lambda bi, h: (bi, 0, base + 2 * n_groups + h)),
                  pl.BlockSpec((None, s, LANES), lambda bi, h: (bi, 0, 0)),
                  pl.BlockSpec((None, N_HEADS_B, s), lambda bi, h: (bi, 0, 0)),
                  pl.BlockSpec((slab, wn), lambda bi, h: (bi * n_groups + h, 0))],
        out_specs=[pl.BlockSpec((None, s, width), lambda bi, h: (bi, 0, h)),
                   pl.BlockSpec((slab, wn), lambda bi, h: (bi * n_groups + h, 0))],
        out_shape=[jax.ShapeDtypeStruct((b, s, D_B), BF16),
                   jax.ShapeDtypeStruct((wk, wn), BF16)],
        scratch_shapes=[pltpu.VMEM((hp, s, 2 * HEAD_DIM), BF16)],
        compiler_params=_cparams(2),
        name="attn_fox",
    )(z, z, z, f_col, f_row, w_cast)


def _rel_bias_rows(rel_bias):
    far = rel_bias[:, 2 * REL_CLIP:]
    n_head = PAD_A - REL_CLIP
    n_tail = REL_ROW - n_head - (2 * REL_CLIP + 1)
    rows = jnp.concatenate([jnp.tile(far, (1, n_head)), jnp.flip(rel_bias, axis=1),
                            jnp.tile(far, (1, n_tail))], axis=1)
    return rows[:, None, :]


def kernel(x, p, g_ffn1, w_ffn1_gu, w_ffn1_down, g_mix, w_in, b_forget, rel_bias, w_out,
           g_ffn2, w_ffn2_gu, w_ffn2_down, g_ple, w_ple_gate, w_ple_proj, g_final):
    b, s, d = x.shape
    m = b * s
    depth = p.shape[0]
    h = x.reshape(m, d)
    for i in range(depth):
        hid, w_down1 = _gateup(h, g_ffn1[i], w_ffn1_gu[i], w_ffn1_down[i], GATEUP_TN_FUSED)
        h, u, w_down2 = _down(hid, w_down1, h, g_mix[i], w_ffn2_down[i])
        w_in_t = jnp.swapaxes(w_in[i], 0, 1)
        z = _proj(u, w_in_t, D_QKV).reshape(b, s, D_QKV)
        b_f = jnp.pad(b_forget[i], (0, LANES - N_HEADS_B)).reshape(1, LANES)
        f_col, f_row = _fox_prep(u.reshape(b, s, d), w_in_t, b_f)
        o_a, w_out_bf = _attn_a(z, _rel_bias_rows(rel_bias[i]), w_out[i])
        o_b, w_gate_bf = _fox(z, f_col, f_row, w_ple_gate[i])
        h, xn = _outproj(o_a.reshape(m, D_A), o_b.reshape(m, D_B), w_out_bf, h, g_ffn2[i])
        hid, = _gateup(xn, None, w_ffn2_gu[i], None, GATEUP_TN)
        h, hn = _down(hid, w_down2, h, g_ple[i], None)
        h = _ple(hn, p[i].reshape(m, D_PLE), w_gate_bf, w_ple_proj[i], h, g_final, i == depth - 1)
    return h.reshape(b, s, d)
```

```python
import functools

import jax
import jax.numpy as jnp
from jax import lax
from jax.experimental import pallas as pl
from jax.experimental.pallas import tpu as pltpu

F32 = jnp.float32
BF16 = jnp.bfloat16

D_MODEL = 2048
CHUNK = 64
N_LEFT_CHUNKS = 8
HEAD_DIM = 128
N_HEADS_A = 8
N_HEADS_B = 8
D_A = N_HEADS_A * HEAD_DIM
D_B = N_HEADS_B * HEAD_DIM
REL_CLIP = 256
D_FF = 5632
D_PLE = 256
EPS = 1e-6
NEG_INF = -1e30
D_QKV = 3 * D_A + 3 * D_B
SCALE = HEAD_DIM ** -0.5
LOG2E = 1.4426950408889634

LANES = 128
VMEM_LIMIT = 56 * 1024 * 1024

QB_A = 2 * CHUNK
BAND_A = (N_LEFT_CHUNKS + 2) * CHUNK
PAD_A = N_LEFT_CHUNKS * CHUNK
REL_ROW = 1024

GATEUP_ROWS = 1024
GATEUP_TN = 256
GATEUP_TN_FUSED = 128
NORM_ROWS = 512
TQ_B = 256
HEADS_PER_STEP = 4
QK_AHEAD = 2


def _cparams(n_axes):
    return pltpu.CompilerParams(
        dimension_semantics=("arbitrary",) * n_axes,
        vmem_limit_bytes=VMEM_LIMIT,
    )


def _rms(x, g):
    ms = jnp.mean(x * x, axis=-1, keepdims=True)
    return (x * lax.rsqrt(ms + EPS)) * g


def _dot(a, b):
    return jnp.dot(a, b, preferred_element_type=F32)


def _gateup_kernel(n_norm, has_rider, *refs):
    refs = list(refs)
    x_ref, g_ref = (refs.pop(0), refs.pop(0)) if n_norm else (None, None)
    a_ref = None if n_norm else refs.pop(0)
    wg_ref, wu_ref = refs.pop(0), refs.pop(0)
    wc_ref = refs.pop(0) if has_rider else None
    o_ref = refs.pop(0)
    wc_o_ref = refs.pop(0) if has_rider else None
    w_s = refs.pop(0)
    if n_norm:
        a_ref = refs.pop(0)
    m, tn = o_ref.shape

    def matmul_phase():
        w_s[:, :tn] = wg_ref[...].astype(BF16)
        w_s[:, tn:] = wu_ref[...].astype(BF16)

        if has_rider:
            wc_o_ref[...] = wc_ref[...].astype(BF16)

        w = w_s[...]
        for c in range(m // GATEUP_ROWS):
            rows = slice(c * GATEUP_ROWS, (c + 1) * GATEUP_ROWS)
            r = _dot(a_ref[rows, :], w)
            g = r[:, :tn]
            u = r[:, tn:]
            o_ref[rows, :] = ((g * jax.nn.sigmoid(g)) * u).astype(o_ref.dtype)

    if n_norm:
        step = pl.program_id(0)

        @pl.when(step < n_norm)
        def _():
            r0 = pl.multiple_of(step * NORM_ROWS, NORM_ROWS)
            a_ref[pl.ds(r0, NORM_ROWS), :] = _rms(x_ref[...], g_ref[...]).astype(BF16)

        pl.when(step >= n_norm)(matmul_phase)
    else:
        matmul_phase()


def _gateup(a, g_norm, w_gu, w_cast, tn):
    m, k = a.shape
    nj = D_FF // tn
    n_norm = 0 if g_norm is None else m // NORM_ROWS
    has_rider = w_cast is not None

    def col(s):
        return jnp.maximum(s - n_norm, 0)

    operands, in_specs = [], []
    scratch = [pltpu.VMEM((k, 2 * tn), BF16)]
    if n_norm:
        operands += [a, g_norm.reshape(1, k)]
        in_specs += [pl.BlockSpec((NORM_ROWS, k), lambda s: (jnp.minimum(s, n_norm - 1), 0)),
                     pl.BlockSpec((1, k), lambda s: (0, 0))]
        scratch.append(pltpu.VMEM((m, k), BF16))
    else:
        operands.append(a)
        in_specs.append(pl.BlockSpec((m, k), lambda s: (0, 0), pipeline_mode=pl.Buffered(1)))
    operands += [w_gu, w_gu]
    in_specs += [pl.BlockSpec((k, tn), lambda s: (0, col(s))),
                 pl.BlockSpec((k, tn), lambda s: (0, col(s) + nj))]
    out_specs = [pl.BlockSpec((m, tn), lambda s: (0, col(s)))]
    out_shape = [jax.ShapeDtypeStruct((m, D_FF), BF16)]
    if has_rider:
        wk, wn = w_cast.shape
        slab = wk // nj
        operands.append(w_cast)
        in_specs.append(pl.BlockSpec((slab, wn), lambda s: (col(s), 0)))
        out_specs.append(pl.BlockSpec((slab, wn), lambda s: (col(s), 0)))
        out_shape.append(jax.ShapeDtypeStruct((wk, wn), BF16))
    return pl.pallas_call(
        functools.partial(_gateup_kernel, n_norm, has_rider),
        grid=(n_norm + nj,),
        in_specs=in_specs,
        out_specs=out_specs,
        out_shape=out_shape,
        scratch_shapes=scratch,
        compiler_params=_cparams(1),
        name="ffn_gateup",
    )(*operands)


def _proj_kernel(a_ref, wt_ref, o_ref, w_s):
    @pl.when(pl.program_id(1) == 0)
    def _():
        w_s[...] = jnp.transpose(wt_ref[...]).astype(BF16)

    o_ref[...] = _dot(a_ref[...], w_s[...]).astype(o_ref.dtype)


def _proj(a, w_t, n_out, tm=2048, tn=1024):
    m, k = a.shape
    return pl.pallas_call(
        _proj_kernel,
        grid=(n_out // tn, m // tm),
        in_specs=[pl.BlockSpec((tm, k), lambda j, i: (i, 0)),
                  pl.BlockSpec((tn, k), lambda j, i: (j, 0))],
        out_specs=pl.BlockSpec((tm, tn), lambda j, i: (i, j)),
        out_shape=jax.ShapeDtypeStruct((m, n_out), BF16),
        scratch_shapes=[pltpu.VMEM((k, tn), BF16)],
        compiler_params=_cparams(2),
        name="in_proj",
    )(a, w_t)


def _down_kernel(has_rider, a_ref, w_ref, r_ref, g_ref, *refs):
    if has_rider:
        wc_ref, h_ref, n_ref, wc_o_ref = refs
    else:
        h_ref, n_ref = refs
    h = r_ref[...] + 0.5 * _dot(a_ref[...], w_ref[...])
    h_ref[...] = h
    n_ref[...] = _rms(h, g_ref[...]).astype(n_ref.dtype)
    if has_rider:
        wc_o_ref[...] = wc_ref[...].astype(BF16)


def _down(a, w_bf16, resid, g_next, w_cast, tm=256):
    m, k = a.shape
    n = w_bf16.shape[1]
    steps = m // tm
    has_rider = w_cast is not None
    operands = [a, w_bf16, resid, g_next.reshape(1, n)]
    in_specs = [pl.BlockSpec((tm, k), lambda i: (i, 0)),
                pl.BlockSpec((k, n), lambda i: (0, 0), pipeline_mode=pl.Buffered(1)),
                pl.BlockSpec((tm, n), lambda i: (i, 0)),
                pl.BlockSpec((1, n), lambda i: (0, 0))]
    out_specs = [pl.BlockSpec((tm, n), lambda i: (i, 0)),
                 pl.BlockSpec((tm, n), lambda i: (i, 0))]
    out_shape = [jax.ShapeDtypeStruct((m, n), F32), jax.ShapeDtypeStruct((m, n), BF16)]
    if has_rider:
        wk, wn = w_cast.shape
        slab = wk // steps
        operands.append(w_cast)
        in_specs.append(pl.BlockSpec((slab, wn), lambda i: (i, 0)))
        out_specs.append(pl.BlockSpec((slab, wn), lambda i: (i, 0)))
        out_shape.append(jax.ShapeDtypeStruct((wk, wn), BF16))
    return pl.pallas_call(
        functools.partial(_down_kernel, has_rider),
        grid=(steps,),
        in_specs=in_specs,
        out_specs=out_specs,
        out_shape=out_shape,
        compiler_params=_cparams(1),
        name="ffn_down",
    )(*operands)


def _outproj_kernel(oa_ref, ob_ref, w_ref, r_ref, g_ref, h_ref, n_ref):
    acc = _dot(oa_ref[...], w_ref[:D_A, :]) + _dot(ob_ref[...], w_ref[D_A:, :])
    h = r_ref[...] + acc
    h_ref[...] = h
    n_ref[...] = _rms(h, g_ref[...]).astype(n_ref.dtype)


def _outproj(o_a, o_b, w_out, resid, g_next, tm=512):
    m = o_a.shape[0]
    k, n = w_out.shape
    return pl.pallas_call(
        _outproj_kernel,
        grid=(m // tm,),
        in_specs=[pl.BlockSpec((tm, D_A), lambda i: (i, 0)),
                  pl.BlockSpec((tm, D_B), lambda i: (i, 0)),
                  pl.BlockSpec((k, n), lambda i: (0, 0), pipeline_mode=pl.Buffered(1)),
                  pl.BlockSpec((tm, n), lambda i: (i, 0)),
                  pl.BlockSpec((1, n), lambda i: (0, 0))],
        out_specs=[pl.BlockSpec((tm, n), lambda i: (i, 0)),
                   pl.BlockSpec((tm, n), lambda i: (i, 0))],
        out_shape=[jax.ShapeDtypeStruct((m, n), F32), jax.ShapeDtypeStruct((m, n), BF16)],
        compiler_params=_cparams(1),
        name="out_proj",
    )(o_a, o_b, w_out, resid, g_next.reshape(1, n))


def _ple_kernel(close_block, a_ref, p_ref, wg_ref, wp_ref, r_ref, g_ref, o_ref, wp_s):
    @pl.when(pl.program_id(0) == 0)
    def _():
        wp_s[...] = wp_ref[...].astype(BF16)

    gate = jax.nn.sigmoid(_dot(a_ref[...], wg_ref[...]))
    emb = _dot(p_ref[...].astype(BF16), wp_s[...])
    h = r_ref[...] + gate * emb
    o_ref[...] = _rms(h, g_ref[...]) if close_block else h


def _ple(a, p, w_gate, w_proj, resid, g_final, close_block, tm=512):
    m, k = a.shape
    n = w_gate.shape[1]
    return pl.pallas_call(
        functools.partial(_ple_kernel, close_block),
        grid=(m // tm,),
        in_specs=[pl.BlockSpec((tm, k), lambda i: (i, 0)),
                  pl.BlockSpec((tm, D_PLE), lambda i: (i, 0)),
                  pl.BlockSpec((k, n), lambda i: (0, 0), pipeline_mode=pl.Buffered(1)),
                  pl.BlockSpec((D_PLE, n), lambda i: (0, 0), pipeline_mode=pl.Buffered(1)),
                  pl.BlockSpec((tm, n), lambda i: (i, 0)),
                  pl.BlockSpec((1, n), lambda i: (0, 0))],
        out_specs=pl.BlockSpec((tm, n), lambda i: (i, 0)),
        out_shape=jax.ShapeDtypeStruct((m, n), F32),
        scratch_shapes=[pltpu.VMEM((D_PLE, n), BF16)],
        compiler_params=_cparams(1),
        name="ple",
    )(a, p, w_gate, w_proj, resid, g_final.reshape(1, n))


def _attn_a_scores(q, k):
    return lax.dot_general(q, k, (((1,), (1,)), ((), ())), preferred_element_type=F32)


def _attn_a_softmax(s, bias, key_lo):
    nk = s.shape[1]
    s = s * (SCALE * LOG2E) + bias
    row = lax.broadcasted_iota(jnp.int32, (QB_A, nk), 0)
    col = lax.broadcasted_iota(jnp.int32, (QB_A, nk), 1) + key_lo
    first = row < CHUNK
    valid = (first & (col < BAND_A - CHUNK)) | (jnp.logical_not(first) & (col >= CHUNK))
    s = jnp.where(valid, s, NEG_INF)
    mx = jnp.max(s, axis=-1, keepdims=True)
    e = jnp.exp2(s - mx)
    den = jnp.sum(e, axis=-1, keepdims=True)
    return e.astype(BF16), den


def _attn_a_kernel(g_ref, q_ref, k_ref, v_ref, w_ref, o_ref, w_o_ref, bias_s):

    @pl.when(pl.program_id(1) == 0)
    def _():
        for hh in range(HEADS_PER_STEP):
            g = jnp.broadcast_to(g_ref[hh] * LOG2E, (QB_A, REL_ROW))
            bias_s[hh] = pltpu.roll(g, 0, axis=1, stride=1, stride_axis=0)[:, :BAND_A]

    n_blocks = q_ref.shape[0] // QB_A
    items = [(m, hh) for m in range(n_blocks) for hh in range(HEADS_PER_STEP)]

    def window(m):
        k0 = max(m * QB_A - PAD_A, 0)
        return k0, (m + 1) * QB_A

    def lanes(hh):
        return slice(hh * HEAD_DIM, (hh + 1) * HEAD_DIM)

    def scores(m, hh):
        k0, k1 = window(m)
        return _attn_a_scores(q_ref[m * QB_A:k1, lanes(hh)], k_ref[k0:k1, lanes(hh)])

    def finish(m, hh, e, den):
        k0, k1 = window(m)
        o = _dot(e, v_ref[k0:k1, lanes(hh)])
        o_ref[m * QB_A:k1, lanes(hh)] = (o / den).astype(o_ref.dtype)

    s_queue = [scores(*items[t]) for t in range(QK_AHEAD)]
    pending = None
    for t, (m, hh) in enumerate(items):
        s = s_queue.pop(0)
        if t + QK_AHEAD < len(items):
            s_queue.append(scores(*items[t + QK_AHEAD]))
        k0, k1 = window(m)
        lo = BAND_A - (k1 - k0)
        e, den = _attn_a_softmax(s, bias_s[hh, :, lo:], lo)
        if pending is not None:
            finish(*pending)
        pending = (m, hh, e, den)
    finish(*pending)

    w_o_ref[...] = w_ref[...].astype(BF16)


def _attn_a(z, bias_rows, w_cast):
    b, s, _ = z.shape
    hp = HEADS_PER_STEP
    n_groups = N_HEADS_A // hp
    width = hp * HEAD_DIM
    wk, wn = w_cast.shape
    slab = wk // (n_groups * b)
    return pl.pallas_call(
        _attn_a_kernel,
        grid=(n_groups, b),
        in_specs=[pl.BlockSpec((hp, 1, REL_ROW), lambda h, bi: (h, 0, 0)),
                  pl.BlockSpec((None, s, width), lambda h, bi: (bi, 0, h)),
                  pl.BlockSpec((None, s, width), lambda h, bi: (bi, 0, n_groups + h)),
                  pl.BlockSpec((None, s, width), lambda h, bi: (bi, 0, 2 * n_groups + h)),
                  pl.BlockSpec((slab, wn), lambda h, bi: (h * b + bi, 0))],
        out_specs=[pl.BlockSpec((None, s, width), lambda h, bi: (bi, 0, h)),
                   pl.BlockSpec((slab, wn), lambda h, bi: (h * b + bi, 0))],
        out_shape=[jax.ShapeDtypeStruct((b, s, D_A), BF16),
                   jax.ShapeDtypeStruct((wk, wn), BF16)],
        scratch_shapes=[pltpu.VMEM((hp, QB_A, BAND_A), F32)],
        compiler_params=_cparams(2),
        name="attn_band",
    )(bias_rows, z, z, z, w_cast)


def _fox_prep_kernel(u_ref, wft_ref, bf_ref, f_ref, ft_ref):
    seq, d = u_ref.shape
    wft = jnp.concatenate([wft_ref[...], jnp.zeros((LANES - N_HEADS_B, d), F32)], axis=0).astype(BF16)
    fl = lax.dot_general(u_ref[...], wft, (((1,), (1,)), ((), ())), preferred_element_type=F32)
    fl = fl + bf_ref[...]
    x = jnp.minimum(fl, 0.0) - jnp.log1p(jnp.exp(-jnp.abs(fl)))
    row = lax.broadcasted_iota(jnp.int32, x.shape, 0)
    shift = 1
    while shift < seq:
        x = x + jnp.where(row >= shift, pltpu.roll(x, shift, axis=0), 0.0)
        shift *= 2
    x = x * LOG2E
    f_ref[...] = x
    ft_ref[...] = jnp.transpose(x)[:N_HEADS_B, :]


def _fox_prep(u3, w_in_t, b_f):
    b, s, d = u3.shape
    return pl.pallas_call(
        _fox_prep_kernel,
        grid=(b,),
        in_specs=[pl.BlockSpec((None, s, d), lambda bi: (bi, 0, 0)),
                  pl.BlockSpec((N_HEADS_B, d), lambda bi: (D_QKV // N_HEADS_B, 0)),
                  pl.BlockSpec((1, LANES), lambda bi: (0, 0))],
        out_specs=[pl.BlockSpec((None, s, LANES), lambda bi: (bi, 0, 0)),
                   pl.BlockSpec((None, N_HEADS_B, s), lambda bi: (bi, 0, 0))],
        out_shape=[jax.ShapeDtypeStruct((b, s, LANES), F32),
                   jax.ShapeDtypeStruct((b, N_HEADS_B, s), F32)],
        compiler_params=_cparams(1),
        name="fox_prep",
    )(u3, w_in_t, b_f)


def _fox_kernel(q_ref, k_ref, v_ref, fq_ref, fk_ref, w_ref, o_ref, w_o_ref, v_aug):
    seq = q_ref.shape[0]
    n_blocks = seq // TQ_B
    group = pl.program_id(1)
    lane = lax.broadcasted_iota(jnp.int32, (TQ_B, LANES), 1)
    row = lax.broadcasted_iota(jnp.int32, (TQ_B, TQ_B), 0)
    col = lax.broadcasted_iota(jnp.int32, (TQ_B, TQ_B), 1)
    nt = (((1,), (1,)), ((), ()))

    def lanes(hh):
        return slice(hh * HEAD_DIM, (hh + 1) * HEAD_DIM)

    for hh in range(HEADS_PER_STEP):
        v_aug[hh, :, :HEAD_DIM] = v_ref[:, lanes(hh)]
        v_aug[hh, :, HEAD_DIM:] = jnp.ones((seq, HEAD_DIM), BF16)

    items = []
    for t in range(n_blocks):
        for hh in range(HEADS_PER_STEP):
            items.append((hh, t if hh % 2 == 0 else n_blocks - 1 - t))

    def scores(hh, qi):
        q0, q1 = qi * TQ_B, (qi + 1) * TQ_B
        q = q_ref[q0:q1, lanes(hh)]
        s_diag = lax.dot_general(q, k_ref[q0:q1, lanes(hh)], nt, preferred_element_type=F32)
        s_past = (lax.dot_general(q, k_ref[0:q0, lanes(hh)], nt, preferred_element_type=F32)
                  if qi > 0 else None)
        return s_diag, s_past

    def softmax(hh, qi, s_d, s_p):
        q0, q1 = qi * TQ_B, (qi + 1) * TQ_B
        h = group * HEADS_PER_STEP + hh
        fq = jnp.sum(jnp.where(lane == h, fq_ref[q0:q1, :], 0.0), axis=1, keepdims=True)
        s_d = (s_d * (SCALE * LOG2E) + fq) - fk_ref[pl.ds(h, 1), q0:q1]
        s_d = jnp.where(row >= col, s_d, NEG_INF)
        mx = jnp.max(s_d, axis=-1, keepdims=True)
        if qi > 0:
            s_p = (s_p * (SCALE * LOG2E) + fq) - fk_ref[pl.ds(h, 1), 0:q0]
            mx = jnp.maximum(mx, jnp.max(s_p, axis=-1, keepdims=True))
        e_d = jnp.exp2(s_d - mx).astype(BF16)
        e_p = jnp.exp2(s_p - mx).astype(BF16) if qi > 0 else None
        return e_d, e_p

    def finish(hh, qi, e_d, e_p):
        q0, q1 = qi * TQ_B, (qi + 1) * TQ_B
        acc = _dot(e_d, v_aug[hh, q0:q1, :])
        if e_p is not None:
            acc = acc + _dot(e_p, v_aug[hh, 0:q0, :])
        o_ref[q0:q1, lanes(hh)] = (acc[:, :HEAD_DIM] / acc[:, HEAD_DIM:]).astype(o_ref.dtype)

    s_queue = [scores(*items[t]) for t in range(QK_AHEAD)]
    pending = None
    for t, (hh, qi) in enumerate(items):
        s_d, s_p = s_queue.pop(0)
        if t + QK_AHEAD < len(items):
            s_queue.append(scores(*items[t + QK_AHEAD]))
        e_d, e_p = softmax(hh, qi, s_d, s_p)
        if pending is not None:
            finish(*pending)
        pending = (hh, qi, e_d, e_p)
    finish(*pending)

    w_o_ref[...] = w_ref[...].astype(BF16)


def _fox(z, f_col, f_row, w_cast):
    b, s, _ = z.shape
    hp = HEADS_PER_STEP
    n_groups = N_HEADS_B // hp
    width = hp * HEAD_DIM
    base = 3 * N_HEADS_A // hp
    wk, wn = w_cast.shape
    slab = wk // (b * n_groups)
    return pl.pallas_call(
        _fox_kernel,
        grid=(b, n_groups),
        in_specs=[pl.BlockSpec((None, s, width), lambda bi, h: (bi, 0, base + h)),
                  pl.BlockSpec((None, s, width), lambda bi, h: (bi, 0, base + n_groups + h)),
                  pl.BlockSpec((None, s, width), lambda bi, h: (bi, 0, base + 2 * n_groups + h)),
                  pl.BlockSpec((None, s, LANES), lambda bi, h: (bi, 0, 0)),
                  pl.BlockSpec((None, N_HEADS_B, s), lambda bi, h: (bi, 0, 0)),
                  pl.BlockSpec((slab, wn), lambda bi, h: (bi * n_groups + h, 0))],
        out_specs=[pl.BlockSpec((None, s, width), lambda bi, h: (bi, 0, h)),
                   pl.BlockSpec((slab, wn), lambda bi, h: (bi * n_groups + h, 0))],
        out_shape=[jax.ShapeDtypeStruct((b, s, D_B), BF16),
                   jax.ShapeDtypeStruct((wk, wn), BF16)],
        scratch_shapes=[pltpu.VMEM((hp, s, 2 * HEAD_DIM), BF16)],
        compiler_params=_cparams(2),
        name="attn_fox",
    )(z, z, z, f_col, f_row, w_cast)


def _rel_bias_rows(rel_bias):
    far = rel_bias[:, 2 * REL_CLIP:]
    n_head = PAD_A - REL_CLIP
    n_tail = REL_ROW - n_head - (2 * REL_CLIP + 1)
    rows = jnp.concatenate([jnp.tile(far, (1, n_head)), jnp.flip(rel_bias, axis=1),
                            jnp.tile(far, (1, n_tail))], axis=1)
    return rows[:, None, :]


def kernel(x, p, g_ffn1, w_ffn1_gu, w_ffn1_down, g_mix, w_in, b_forget, rel_bias, w_out,
           g_ffn2, w_ffn2_gu, w_ffn2_down, g_ple, w_ple_gate, w_ple_proj, g_final):
    b, s, d = x.shape
    m = b * s
    depth = p.shape[0]
    h = x.reshape(m, d)
    for i in range(depth):
        hid, w_down1 = _gateup(h, g_ffn1[i], w_ffn1_gu[i], w_ffn1_down[i], GATEUP_TN_FUSED)
        h, u, w_down2 = _down(hid, w_down1, h, g_mix[i], w_ffn2_down[i])
        w_in_t = jnp.swapaxes(w_in[i], 0, 1)
        z = _proj(u, w_in_t, D_QKV).reshape(b, s, D_QKV)
        b_f = jnp.pad(b_forget[i], (0, LANES - N_HEADS_B)).reshape(1, LANES)
        f_col, f_row = _fox_prep(u.reshape(b, s, d), w_in_t, b_f)
        o_a, w_out_bf = _attn_a(z, _rel_bias_rows(rel_bias[i]), w_out[i])
        o_b, w_gate_bf = _fox(z, f_col, f_row, w_ple_gate[i])
        h, xn = _outproj(o_a.reshape(m, D_A), o_b.reshape(m, D_B), w_out_bf, h, g_ffn2[i])
        hid, = _gateup(xn, None, w_ffn2_gu[i], None, GATEUP_TN)
        h, hn = _down(hid, w_down2, h, g_ple[i], None)
        h = _ple(hn, p[i].reshape(m, D_PLE), w_gate_bf, w_ple_proj[i], h, g_final, i == depth - 1)
    return h.reshape(b, s, d)
```

```python
import functools

import jax
import jax.numpy as jnp
from jax import lax
from jax.experimental import pallas as pl
from jax.experimental.pallas import tpu as pltpu

F32 = jnp.float32
BF16 = jnp.bfloat16

D_MODEL = 2048
CHUNK = 64
N_LEFT_CHUNKS = 8
HEAD_DIM = 128
N_HEADS_A = 8
N_HEADS_B = 8
D_A = N_HEADS_A * HEAD_DIM
D_B = N_HEADS_B * HEAD_DIM
REL_CLIP = 256
D_FF = 5632
D_PLE = 256
EPS = 1e-6
NEG_INF = -1e30
D_QKV = 3 * D_A + 3 * D_B
SCALE = HEAD_DIM ** -0.5
LOG2E = 1.4426950408889634

LANES = 128
VMEM_LIMIT = 56 * 1024 * 1024

QB_A = 2 * CHUNK
BAND_A = (N_LEFT_CHUNKS + 2) * CHUNK
PAD_A = N_LEFT_CHUNKS * CHUNK
REL_ROW = 1024

GATEUP_ROWS = 1024
GATEUP_TN = 256
GATEUP_TN_FUSED = 128
NORM_ROWS = 512
TQ_B = 256
HEADS_PER_STEP_A = 2
HEADS_PER_STEP_B = 4
QK_AHEAD = 2


def _cparams(n_axes):
    return pltpu.CompilerParams(
        dimension_semantics=("arbitrary",) * n_axes,
        vmem_limit_bytes=VMEM_LIMIT,
    )


def _rms(x, g):
    ms = jnp.mean(x * x, axis=-1, keepdims=True)
    return (x * lax.rsqrt(ms + EPS)) * g


def _dot(a, b):
    return jnp.dot(a, b, preferred_element_type=F32)


def _gateup_kernel(n_norm, has_rider, *refs):
    refs = list(refs)
    x_ref, g_ref = (refs.pop(0), refs.pop(0)) if n_norm else (None, None)
    a_ref = None if n_norm else refs.pop(0)
    wg_ref, wu_ref = refs.pop(0), refs.pop(0)
    wc_ref = refs.pop(0) if has_rider else None
    o_ref = refs.pop(0)
    wc_o_ref = refs.pop(0) if has_rider else None
    w_s = refs.pop(0)
    if n_norm:
        a_ref = refs.pop(0)
    m, tn = o_ref.shape

    def matmul_phase():
        w_s[:, :tn] = wg_ref[...].astype(BF16)
        w_s[:, tn:] = wu_ref[...].astype(BF16)

        if has_rider:
            wc_o_ref[...] = wc_ref[...].astype(BF16)

        w = w_s[...]
        for c in range(m // GATEUP_ROWS):
            rows = slice(c * GATEUP_ROWS, (c + 1) * GATEUP_ROWS)
            r = _dot(a_ref[rows, :], w)
            g = r[:, :tn]
            u = r[:, tn:]
            o_ref[rows, :] = ((g * jax.nn.sigmoid(g)) * u).astype(o_ref.dtype)

    if n_norm:
        step = pl.program_id(0)

        @pl.when(step < n_norm)
        def _():
            r0 = pl.multiple_of(step * NORM_ROWS, NORM_ROWS)
            a_ref[pl.ds(r0, NORM_ROWS), :] = _rms(x_ref[...], g_ref[...]).astype(BF16)

        pl.when(step >= n_norm)(matmul_phase)
    else:
        matmul_phase()


def _gateup(a, g_norm, w_gu, w_cast, tn):
    m, k = a.shape
    nj = D_FF // tn
    n_norm = 0 if g_norm is None else m // NORM_ROWS
    has_rider = w_cast is not None

    def col(s):
        return jnp.maximum(s - n_norm, 0)

    operands, in_specs = [], []
    scratch = [pltpu.VMEM((k, 2 * tn), BF16)]
    if n_norm:
        operands += [a, g_norm.reshape(1, k)]
        in_specs += [pl.BlockSpec((NORM_ROWS, k), lambda s: (jnp.minimum(s, n_norm - 1), 0)),
                     pl.BlockSpec((1, k), lambda s: (0, 0))]
        scratch.append(pltpu.VMEM((m, k), BF16))
    else:
        operands.append(a)
        in_specs.append(pl.BlockSpec((m, k), lambda s: (0, 0), pipeline_mode=pl.Buffered(1)))
    operands += [w_gu, w_gu]
    in_specs += [pl.BlockSpec((k, tn), lambda s: (0, col(s))),
                 pl.BlockSpec((k, tn), lambda s: (0, col(s) + nj))]
    out_specs = [pl.BlockSpec((m, tn), lambda s: (0, col(s)))]
    out_shape = [jax.ShapeDtypeStruct((m, D_FF), BF16)]
    if has_rider:
        wk, wn = w_cast.shape
        slab = wk // nj
        operands.append(w_cast)
        in_specs.append(pl.BlockSpec((slab, wn), lambda s: (col(s), 0)))
        out_specs.append(pl.BlockSpec((slab, wn), lambda s: (col(s), 0)))
        out_shape.append(jax.ShapeDtypeStruct((wk, wn), BF16))
    return pl.pallas_call(
        functools.partial(_gateup_kernel, n_norm, has_rider),
        grid=(n_norm + nj,),
        in_specs=in_specs,
        out_specs=out_specs,
        out_shape=out_shape,
        scratch_shapes=scratch,
        compiler_params=_cparams(1),
        name="ffn_gateup",
    )(*operands)


def _proj_kernel(a_ref, wt_ref, o_ref, w_s):
    @pl.when(pl.program_id(1) == 0)
    def _():
        w_s[...] = jnp.transpose(wt_ref[...]).astype(BF16)

    o_ref[...] = _dot(a_ref[...], w_s[...]).astype(o_ref.dtype)


def _proj(a, w_t, n_out, tm=2048, tn=1024):
    m, k = a.shape
    return pl.pallas_call(
        _proj_kernel,
        grid=(n_out // tn, m // tm),
        in_specs=[pl.BlockSpec((tm, k), lambda j, i: (i, 0)),
                  pl.BlockSpec((tn, k), lambda j, i: (j, 0))],
        out_specs=pl.BlockSpec((tm, tn), lambda j, i: (i, j)),
        out_shape=jax.ShapeDtypeStruct((m, n_out), BF16),
        scratch_shapes=[pltpu.VMEM((k, tn), BF16)],
        compiler_params=_cparams(2),
        name="in_proj",
    )(a, w_t)


def _down_kernel(a_ref, w_ref, r_ref, g_ref, wc_ref, h_ref, n_ref, wc_o_ref):
    h = r_ref[...] + 0.5 * _dot(a_ref[...], w_ref[...])
    h_ref[...] = h
    n_ref[...] = _rms(h, g_ref[...]).astype(n_ref.dtype)
    wc_o_ref[...] = wc_ref[...].astype(BF16)


def _down(a, w_bf16, resid, g_next, w_cast, tm=256):
    m, k = a.shape
    n = w_bf16.shape[1]
    steps = m // tm
    wk, wn = w_cast.shape
    slab = wk // steps
    return pl.pallas_call(
        _down_kernel,
        grid=(steps,),
        in_specs=[pl.BlockSpec((tm, k), lambda i: (i, 0)),
                  pl.BlockSpec((k, n), lambda i: (0, 0), pipeline_mode=pl.Buffered(1)),
                  pl.BlockSpec((tm, n), lambda i: (i, 0)),
                  pl.BlockSpec((1, n), lambda i: (0, 0)),
                  pl.BlockSpec((slab, wn), lambda i: (i, 0))],
        out_specs=[pl.BlockSpec((tm, n), lambda i: (i, 0)),
                   pl.BlockSpec((tm, n), lambda i: (i, 0)),
                   pl.BlockSpec((slab, wn), lambda i: (i, 0))],
        out_shape=[jax.ShapeDtypeStruct((m, n), F32), jax.ShapeDtypeStruct((m, n), BF16),
                   jax.ShapeDtypeStruct((wk, wn), BF16)],
        compiler_params=_cparams(1),
        name="ffn_down",
    )(a, w_bf16, resid, g_next.reshape(1, n), w_cast)


def _outproj_kernel(oa_ref, ob_ref, w_ref, r_ref, g_ref, h_ref, n_ref):
    acc = _dot(oa_ref[...], w_ref[:D_A, :]) + _dot(ob_ref[...], w_ref[D_A:, :])
    h = r_ref[...] + acc
    h_ref[...] = h
    n_ref[...] = _rms(h, g_ref[...]).astype(n_ref.dtype)


def _outproj(o_a, o_b, w_out, resid, g_next, tm=512):
    m = o_a.shape[0]
    k, n = w_out.shape
    return pl.pallas_call(
        _outproj_kernel,
        grid=(m // tm,),
        in_specs=[pl.BlockSpec((tm, D_A), lambda i: (i, 0)),
                  pl.BlockSpec((tm, D_B), lambda i: (i, 0)),
                  pl.BlockSpec((k, n), lambda i: (0, 0), pipeline_mode=pl.Buffered(1)),
                  pl.BlockSpec((tm, n), lambda i: (i, 0)),
                  pl.BlockSpec((1, n), lambda i: (0, 0))],
        out_specs=[pl.BlockSpec((tm, n), lambda i: (i, 0)),
                   pl.BlockSpec((tm, n), lambda i: (i, 0))],
        out_shape=[jax.ShapeDtypeStruct((m, n), F32), jax.ShapeDtypeStruct((m, n), BF16)],
        compiler_params=_cparams(1),
        name="out_proj",
    )(o_a, o_b, w_out, resid, g_next.reshape(1, n))


def _down_ple_kernel(close_block, a_ref, w_ref, r_ref, gp_ref, p_ref, wg_ref, wp_ref, gf_ref, o_ref, wp_s):
    @pl.when(pl.program_id(0) == 0)
    def _():
        wp_s[...] = wp_ref[...].astype(BF16)

    h = r_ref[...] + 0.5 * _dot(a_ref[...], w_ref[...])
    hn = _rms(h, gp_ref[...]).astype(BF16)
    gate = jax.nn.sigmoid(_dot(hn, wg_ref[...]))
    emb = _dot(p_ref[...].astype(BF16), wp_s[...])
    h = h + gate * emb
    o_ref[...] = _rms(h, gf_ref[...]) if close_block else h


def _down_ple(a, w_down, resid, g_ple, p, w_gate, w_proj, g_final, close_block, tm=256):
    m, k = a.shape
    n = w_down.shape[1]
    const = dict(pipeline_mode=pl.Buffered(1))
    return pl.pallas_call(
        functools.partial(_down_ple_kernel, close_block),
        grid=(m // tm,),
        in_specs=[pl.BlockSpec((tm, k), lambda i: (i, 0)),
                  pl.BlockSpec((k, n), lambda i: (0, 0), **const),
                  pl.BlockSpec((tm, n), lambda i: (i, 0)),
                  pl.BlockSpec((1, n), lambda i: (0, 0)),
                  pl.BlockSpec((tm, D_PLE), lambda i: (i, 0)),
                  pl.BlockSpec((n, n), lambda i: (0, 0), **const),
                  pl.BlockSpec((D_PLE, n), lambda i: (0, 0), **const),
                  pl.BlockSpec((1, n), lambda i: (0, 0))],
        out_specs=pl.BlockSpec((tm, n), lambda i: (i, 0)),
        out_shape=jax.ShapeDtypeStruct((m, n), F32),
        scratch_shapes=[pltpu.VMEM((D_PLE, n), BF16)],
        compiler_params=_cparams(1),
        name="ffn_down_ple",
    )(a, w_down, resid, g_ple.reshape(1, n), p, w_gate, w_proj, g_final.reshape(1, n))


def _attn_a_scores(q, k):
    return lax.dot_general(q, k, (((1,), (1,)), ((), ())), preferred_element_type=F32)


def _attn_a_softmax(s, bias, key_lo):
    nk = s.shape[1]
    s = s * (SCALE * LOG2E) + bias
    row = lax.broadcasted_iota(jnp.int32, (QB_A, nk), 0)
    col = lax.broadcasted_iota(jnp.int32, (QB_A, nk), 1) + key_lo
    first = row < CHUNK
    valid = (first & (col < BAND_A - CHUNK)) | (jnp.logical_not(first) & (col >= CHUNK))
    s = jnp.where(valid, s, NEG_INF)
    mx = jnp.max(s, axis=-1, keepdims=True)
    e = jnp.exp2(s - mx)
    den = jnp.sum(e, axis=-1, keepdims=True)
    return e.astype(BF16), den


def _attn_a_kernel(g_ref, q_ref, k_ref, v_ref, w_ref, o_ref, w_o_ref, bias_s):

    @pl.when(pl.program_id(1) == 0)
    def _():
        for hh in range(HEADS_PER_STEP_A):
            g = jnp.broadcast_to(g_ref[hh] * LOG2E, (QB_A, REL_ROW))
            bias_s[hh] = pltpu.roll(g, 0, axis=1, stride=1, stride_axis=0)[:, :BAND_A]

    n_blocks = q_ref.shape[0] // QB_A
    items = [(m, hh) for m in range(n_blocks) for hh in range(HEADS_PER_STEP_A)]

    def window(m):
        k0 = max(m * QB_A - PAD_A, 0)
        return k0, (m + 1) * QB_A

    def lanes(hh):
        return slice(hh * HEAD_DIM, (hh + 1) * HEAD_DIM)

    def scores(m, hh):
        k0, k1 = window(m)
        return _attn_a_scores(q_ref[m * QB_A:k1, lanes(hh)], k_ref[k0:k1, lanes(hh)])

    def finish(m, hh, e, den):
        k0, k1 = window(m)
        o = _dot(e, v_ref[k0:k1, lanes(hh)])
        o_ref[m * QB_A:k1, lanes(hh)] = (o / den).astype(o_ref.dtype)

    s_queue = [scores(*items[t]) for t in range(QK_AHEAD)]
    pending = None
    for t, (m, hh) in enumerate(items):
        s = s_queue.pop(0)
        if t + QK_AHEAD < len(items):
            s_queue.append(scores(*items[t + QK_AHEAD]))
        k0, k1 = window(m)
        lo = BAND_A - (k1 - k0)
        e, den = _attn_a_softmax(s, bias_s[hh, :, lo:], lo)
        if pending is not None:
            finish(*pending)
        pending = (m, hh, e, den)
    finish(*pending)

    w_o_ref[...] = w_ref[...].astype(BF16)


def _attn_a(z, bias_rows, w_cast):
    b, s, _ = z.shape
    hp = HEADS_PER_STEP_A
    n_groups = N_HEADS_A // hp
    width = hp * HEAD_DIM
    wk, wn = w_cast.shape
    slab = wk // (n_groups * b)
    return pl.pallas_call(
        _attn_a_kernel,
        grid=(n_groups, b),
        in_specs=[pl.BlockSpec((hp, 1, REL_ROW), lambda h, bi: (h, 0, 0)),
                  pl.BlockSpec((None, s, width), lambda h, bi: (bi, 0, h)),
                  pl.BlockSpec((None, s, width), lambda h, bi: (bi, 0, n_groups + h)),
                  pl.BlockSpec((None, s, width), lambda h, bi: (bi, 0, 2 * n_groups + h)),
                  pl.BlockSpec((slab, wn), lambda h, bi: (h * b + bi, 0))],
        out_specs=[pl.BlockSpec((None, s, width), lambda h, bi: (bi, 0, h)),
                   pl.BlockSpec((slab, wn), lambda h, bi: (h * b + bi, 0))],
        out_shape=[jax.ShapeDtypeStruct((b, s, D_A), BF16),
                   jax.ShapeDtypeStruct((wk, wn), BF16)],
        scratch_shapes=[pltpu.VMEM((hp, QB_A, BAND_A), F32)],
        compiler_params=_cparams(2),
        name="attn_band",
    )(bias_rows, z, z, z, w_cast)


def _fox_prep_kernel(u_ref, wft_ref, bf_ref, f_ref, ft_ref):
    seq, d = u_ref.shape
    wft = jnp.concatenate([wft_ref[...], jnp.zeros((LANES - N_HEADS_B, d), F32)], axis=0).astype(BF16)
    fl = lax.dot_general(u_ref[...], wft, (((1,), (1,)), ((), ())), preferred_element_type=F32)
    fl = fl + bf_ref[...]
    x = jnp.minimum(fl, 0.0) - jnp.log1p(jnp.exp(-jnp.abs(fl)))
    row = lax.broadcasted_iota(jnp.int32, x.shape, 0)
    shift = 1
    while shift < seq:
        x = x + jnp.where(row >= shift, pltpu.roll(x, shift, axis=0), 0.0)
        shift *= 2
    x = x * LOG2E
    f_ref[...] = x
    ft_ref[...] = jnp.transpose(x)[:N_HEADS_B, :]


def _fox_prep(u3, w_in_t, b_f):
    b, s, d = u3.shape
    return pl.pallas_call(
        _fox_prep_kernel,
        grid=(b,),
        in_specs=[pl.BlockSpec((None, s, d), lambda bi: (bi, 0, 0)),
                  pl.BlockSpec((N_HEADS_B, d), lambda bi: (D_QKV // N_HEADS_B, 0)),
                  pl.BlockSpec((1, LANES), lambda bi: (0, 0))],
        out_specs=[pl.BlockSpec((None, s, LANES), lambda bi: (bi, 0, 0)),
                   pl.BlockSpec((None, N_HEADS_B, s), lambda bi: (bi, 0, 0))],
        out_shape=[jax.ShapeDtypeStruct((b, s, LANES), F32),
                   jax.ShapeDtypeStruct((b, N_HEADS_B, s), F32)],
        compiler_params=_cparams(1),
        name="fox_prep",
    )(u3, w_in_t, b_f)


def _fox_kernel(q_ref, k_ref, v_ref, fq_ref, fk_ref, w_ref, o_ref, w_o_ref, v_aug):
    seq = q_ref.shape[0]
    n_blocks = seq // TQ_B
    group = pl.program_id(1)
    lane = lax.broadcasted_iota(jnp.int32, (TQ_B, LANES), 1)
    row = lax.broadcasted_iota(jnp.int32, (TQ_B, TQ_B), 0)
    col = lax.broadcasted_iota(jnp.int32, (TQ_B, TQ_B), 1)
    nt = (((1,), (1,)), ((), ()))

    def lanes(hh):
        return slice(hh * HEAD_DIM, (hh + 1) * HEAD_DIM)

    for hh in range(HEADS_PER_STEP_B):
        v_aug[hh, :, :HEAD_DIM] = v_ref[:, lanes(hh)]
        v_aug[hh, :, HEAD_DIM:] = jnp.ones((seq, HEAD_DIM), BF16)

    items = []
    for t in range(n_blocks):
        for hh in range(HEADS_PER_STEP_B):
            items.append((hh, t if hh % 2 == 0 else n_blocks - 1 - t))

    def scores(hh, qi):
        q0, q1 = qi * TQ_B, (qi + 1) * TQ_B
        q = q_ref[q0:q1, lanes(hh)]
        s_diag = lax.dot_general(q, k_ref[q0:q1, lanes(hh)], nt, preferred_element_type=F32)
        s_past = (lax.dot_general(q, k_ref[0:q0, lanes(hh)], nt, preferred_element_type=F32)
                  if qi > 0 else None)
        return s_diag, s_past

    def softmax(hh, qi, s_d, s_p):
        q0, q1 = qi * TQ_B, (qi + 1) * TQ_B
        h = group * HEADS_PER_STEP_B + hh
        fq = jnp.sum(jnp.where(lane == h, fq_ref[q0:q1, :], 0.0), axis=1, keepdims=True)
        s_d = (s_d * (SCALE * LOG2E) + fq) - fk_ref[pl.ds(h, 1), q0:q1]
        s_d = jnp.where(row >= col, s_d, NEG_INF)
        mx = jnp.max(s_d, axis=-1, keepdims=True)
        if qi > 0:
            s_p = (s_p * (SCALE * LOG2E) + fq) - fk_ref[pl.ds(h, 1), 0:q0]
            mx = jnp.maximum(mx, jnp.max(s_p, axis=-1, keepdims=True))
        e_d = jnp.exp2(s_d - mx).astype(BF16)
        e_p = jnp.exp2(s_p - mx).astype(BF16) if qi > 0 else None
        return e_d, e_p

    def finish(hh, qi, e_d, e_p):
        q0, q1 = qi * TQ_B, (qi + 1) * TQ_B
        acc = _dot(e_d, v_aug[hh, q0:q1, :])
        if e_p is not None:
            acc = acc + _dot(e_p, v_aug[hh, 0:q0, :])
        o_ref[q0:q1, lanes(hh)] = (acc[:, :HEAD_DIM] / acc[:, HEAD_DIM:]).astype(o_ref.dtype)

    s_queue = [scores(*items[t]) for t in range(QK_AHEAD)]
    pending = None
    for t, (hh, qi) in enumerate(items):
        s_d, s_p = s_queue.pop(0)
        if t + QK_AHEAD < len(items):
            s_queue.append(scores(*items[t + QK_AHEAD]))
        e_d, e_p = softmax(hh, qi, s_d, s_p)
        if pending is not None:
            finish(*pending)
        pending = (hh, qi, e_d, e_p)
    finish(*pending)

    w_o_ref[...] = w_ref[...].astype(BF16)


def _fox(z, f_col, f_row, w_cast):
    b, s, _ = z.shape
    hp = HEADS_PER_STEP_B
    n_groups = N_HEADS_B // hp
    width = hp * HEAD_DIM
    base = 3 * N_HEADS_A // hp
    wk, wn = w_cast.shape
    slab = wk // (b * n_groups)
    return pl.pallas_call(
        _fox_kernel,
        grid=(b, n_groups),
        in_specs=[pl.BlockSpec((None, s, width), lambda bi, h: (bi, 0, base + h)),
                  pl.BlockSpec((None, s, width), lambda bi, h: (bi, 0, base + n_groups + h)),
                  pl.BlockSpec((None, s, width), lambda bi, h: (bi, 0, base + 2 * n_groups + h)),
                  pl.BlockSpec((None, s, LANES), lambda bi, h: (bi, 0, 0)),
                  pl.BlockSpec((None, N_HEADS_B, s), lambda bi, h: (bi, 0, 0)),
                  pl.BlockSpec((slab, wn), lambda bi, h: (bi * n_groups + h, 0))],
        out_specs=[pl.BlockSpec((None, s, width), lambda bi, h: (bi, 0, h)),
                   pl.BlockSpec((slab, wn), lambda bi, h: (bi * n_groups + h, 0))],
        out_shape=[jax.ShapeDtypeStruct((b, s, D_B), BF16),
                   jax.ShapeDtypeStruct((wk, wn), BF16)],
        scratch_shapes=[pltpu.VMEM((hp, s, 2 * HEAD_DIM), BF16)],
        compiler_params=_cparams(2),
        name="attn_fox",
    )(z, z, z, f_col, f_row, w_cast)


def _rel_bias_rows(rel_bias):
    far = rel_bias[:, 2 * REL_CLIP:]
    n_head = PAD_A - REL_CLIP
    n_tail = REL_ROW - n_head - (2 * REL_CLIP + 1)
    rows = jnp.concatenate([jnp.tile(far, (1, n_head)), jnp.flip(rel_bias, axis=1),
                            jnp.tile(far, (1, n_tail))], axis=1)
    return rows[:, None, :]


def kernel(x, p, g_ffn1, w_ffn1_gu, w_ffn1_down, g_mix, w_in, b_forget, rel_bias, w_out,
           g_ffn2, w_ffn2_gu, w_ffn2_down, g_ple, w_ple_gate, w_ple_proj, g_final):
    b, s, d = x.shape
    m = b * s
    depth = p.shape[0]
    h = x.reshape(m, d)
    for i in range(depth):
        hid, w_down1 = _gateup(h, g_ffn1[i], w_ffn1_gu[i], w_ffn1_down[i], GATEUP_TN_FUSED)
        h, u, w_down2 = _down(hid, w_down1, h, g_mix[i], w_ffn2_down[i])
        w_in_t = jnp.swapaxes(w_in[i], 0, 1)
        z = _proj(u, w_in_t, D_QKV).reshape(b, s, D_QKV)
        b_f = jnp.pad(b_forget[i], (0, LANES - N_HEADS_B)).reshape(1, LANES)
        f_col, f_row = _fox_prep(u.reshape(b, s, d), w_in_t, b_f)
        o_a, w_out_bf = _attn_a(z, _rel_bias_rows(rel_bias[i]), w_out[i])
        o_b, w_gate_bf = _fox(z, f_col, f_row, w_ple_gate[i])
        h, xn = _outproj(o_a.reshape(m, D_A), o_b.reshape(m, D_B), w_out_bf, h, g_ffn2[i])
        hid, = _gateup(xn, None, w_ffn2_gu[i], None, GATEUP_TN)
        h = _down_ple(hid, w_down2, h, g_ple[i], p[i].reshape(m, D_PLE), w_gate_bf, w_ple_proj[i],
                      g_final, i == depth - 1)
    return h.reshape(b, s, d)
```

```python
import functools

import jax
import jax.numpy as jnp
from jax import lax
from jax.experimental import pallas as pl
from jax.experimental.pallas import tpu as pltpu

F32 = jnp.float32
BF16 = jnp.bfloat16

D_MODEL = 2048
CHUNK = 64
N_LEFT_CHUNKS = 8
HEAD_DIM = 128
N_HEADS_A = 8
N_HEADS_B = 8
D_A = N_HEADS_A * HEAD_DIM
D_B = N_HEADS_B * HEAD_DIM
REL_CLIP = 256
D_FF = 5632
D_PLE = 256
EPS = 1e-6
NEG_INF = -1e30
D_QKV = 3 * D_A + 3 * D_B
SCALE = HEAD_DIM ** -0.5
LOG2E = 1.4426950408889634

LANES = 128
VMEM_LIMIT = 56 * 1024 * 1024

QB_A = 2 * CHUNK
BAND_A = (N_LEFT_CHUNKS + 2) * CHUNK
PAD_A = N_LEFT_CHUNKS * CHUNK
REL_ROW = 1024

GATEUP_ROWS = 1024
GATEUP_TN = 256
GATEUP_TN_FUSED = 128
NORM_ROWS = 512
TQ_B = 256
HEADS_PER_STEP_A = 2
HEADS_PER_STEP_B = 4
QK_AHEAD = 2


def _cparams(n_axes):
    return pltpu.CompilerParams(
        dimension_semantics=("arbitrary",) * n_axes,
        vmem_limit_bytes=VMEM_LIMIT,
    )


def _rms(x, g):
    ms = jnp.mean(x * x, axis=-1, keepdims=True)
    return (x * lax.rsqrt(ms + EPS)) * g


def _dot(a, b):
    return jnp.dot(a, b, preferred_element_type=F32)


def _gateup_kernel(n_norm, has_rider, *refs):
    refs = list(refs)
    x_ref, g_ref = (refs.pop(0), refs.pop(0)) if n_norm else (None, None)
    a_ref = None if n_norm else refs.pop(0)
    wg_ref, wu_ref = refs.pop(0), refs.pop(0)
    wc_ref = refs.pop(0) if has_rider else None
    o_ref = refs.pop(0)
    wc_o_ref = refs.pop(0) if has_rider else None
    w_s = refs.pop(0)
    if n_norm:
        a_ref = refs.pop(0)
    m, tn = o_ref.shape

    def matmul_phase():
        w_s[:, :tn] = wg_ref[...].astype(BF16)
        w_s[:, tn:] = wu_ref[...].astype(BF16)

        if has_rider:
            wc_o_ref[...] = wc_ref[...].astype(BF16)

        w = w_s[...]
        for c in range(m // GATEUP_ROWS):
            rows = slice(c * GATEUP_ROWS, (c + 1) * GATEUP_ROWS)
            r = _dot(a_ref[rows, :], w)
            g = r[:, :tn]
            u = r[:, tn:]
            o_ref[rows, :] = ((g * jax.nn.sigmoid(g)) * u).astype(o_ref.dtype)

    if n_norm:
        step = pl.program_id(0)

        @pl.when(step < n_norm)
        def _():
            r0 = pl.multiple_of(step * NORM_ROWS, NORM_ROWS)
            a_ref[pl.ds(r0, NORM_ROWS), :] = _rms(x_ref[...], g_ref[...]).astype(BF16)

        pl.when(step >= n_norm)(matmul_phase)
    else:
        matmul_phase()


def _gateup(a, g_norm, w_gu, w_cast, tn):
    m, k = a.shape
    nj = D_FF // tn
    n_norm = 0 if g_norm is None else m // NORM_ROWS
    has_rider = w_cast is not None

    def col(s):
        return jnp.maximum(s - n_norm, 0)

    operands, in_specs = [], []
    scratch = [pltpu.VMEM((k, 2 * tn), BF16)]
    if n_norm:
        operands += [a, g_norm.reshape(1, k)]
        in_specs += [pl.BlockSpec((NORM_ROWS, k), lambda s: (jnp.minimum(s, n_norm - 1), 0)),
                     pl.BlockSpec((1, k), lambda s: (0, 0))]
        scratch.append(pltpu.VMEM((m, k), BF16))
    else:
        operands.append(a)
        in_specs.append(pl.BlockSpec((m, k), lambda s: (0, 0), pipeline_mode=pl.Buffered(1)))
    operands += [w_gu, w_gu]
    in_specs += [pl.BlockSpec((k, tn), lambda s: (0, col(s))),
                 pl.BlockSpec((k, tn), lambda s: (0, col(s) + nj))]
    out_specs = [pl.BlockSpec((m, tn), lambda s: (0, col(s)))]
    out_shape = [jax.ShapeDtypeStruct((m, D_FF), BF16)]
    if has_rider:
        wk, wn = w_cast.shape
        slab = wk // nj
        operands.append(w_cast)
        in_specs.append(pl.BlockSpec((slab, wn), lambda s: (col(s), 0)))
        out_specs.append(pl.BlockSpec((slab, wn), lambda s: (col(s), 0)))
        out_shape.append(jax.ShapeDtypeStruct((wk, wn), BF16))
    return pl.pallas_call(
        functools.partial(_gateup_kernel, n_norm, has_rider),
        grid=(n_norm + nj,),
        in_specs=in_specs,
        out_specs=out_specs,
        out_shape=out_shape,
        scratch_shapes=scratch,
        compiler_params=_cparams(1),
        name="ffn_gateup",
    )(*operands)


def _proj_kernel(a_ref, wt_ref, o_ref, w_s):
    @pl.when(pl.program_id(1) == 0)
    def _():
        w_s[...] = wt_ref[...].astype(BF16)

    o_ref[...] = lax.dot_general(a_ref[...], w_s[...], (((1,), (1,)), ((), ())),
                                 preferred_element_type=F32).astype(o_ref.dtype)


def _proj(a, w_t, n_out, tm=2048, tn=1024):
    m, k = a.shape
    return pl.pallas_call(
        _proj_kernel,
        grid=(n_out // tn, m // tm),
        in_specs=[pl.BlockSpec((tm, k), lambda j, i: (i, 0)),
                  pl.BlockSpec((tn, k), lambda j, i: (j, 0))],
        out_specs=pl.BlockSpec((tm, tn), lambda j, i: (i, j)),
        out_shape=jax.ShapeDtypeStruct((m, n_out), BF16),
        scratch_shapes=[pltpu.VMEM((tn, k), BF16)],
        compiler_params=_cparams(2),
        name="in_proj",
    )(a, w_t)


def _down_kernel(a_ref, w_ref, r_ref, g_ref, wc_ref, h_ref, n_ref, wc_o_ref):
    h = r_ref[...] + 0.5 * _dot(a_ref[...], w_ref[...])
    h_ref[...] = h
    n_ref[...] = _rms(h, g_ref[...]).astype(n_ref.dtype)
    wc_o_ref[...] = wc_ref[...].astype(BF16)


def _down(a, w_bf16, resid, g_next, w_cast, tm=256):
    m, k = a.shape
    n = w_bf16.shape[1]
    steps = m // tm
    wk, wn = w_cast.shape
    slab = wk // steps
    return pl.pallas_call(
        _down_kernel,
        grid=(steps,),
        in_specs=[pl.BlockSpec((tm, k), lambda i: (i, 0)),
                  pl.BlockSpec((k, n), lambda i: (0, 0), pipeline_mode=pl.Buffered(1)),
                  pl.BlockSpec((tm, n), lambda i: (i, 0)),
                  pl.BlockSpec((1, n), lambda i: (0, 0)),
                  pl.BlockSpec((slab, wn), lambda i: (i, 0))],
        out_specs=[pl.BlockSpec((tm, n), lambda i: (i, 0)),
                   pl.BlockSpec((tm, n), lambda i: (i, 0)),
                   pl.BlockSpec((slab, wn), lambda i: (i, 0))],
        out_shape=[jax.ShapeDtypeStruct((m, n), F32), jax.ShapeDtypeStruct((m, n), BF16),
                   jax.ShapeDtypeStruct((wk, wn), BF16)],
        compiler_params=_cparams(1),
        name="ffn_down",
    )(a, w_bf16, resid, g_next.reshape(1, n), w_cast)


def _outproj_kernel(oa_ref, ob_ref, w_ref, r_ref, g_ref, h_ref, n_ref):
    acc = _dot(oa_ref[...], w_ref[:D_A, :]) + _dot(ob_ref[...], w_ref[D_A:, :])
    h = r_ref[...] + acc
    h_ref[...] = h
    n_ref[...] = _rms(h, g_ref[...]).astype(n_ref.dtype)


def _outproj(o_a, o_b, w_out, resid, g_next, tm=512):
    m = o_a.shape[0]
    k, n = w_out.shape
    return pl.pallas_call(
        _outproj_kernel,
        grid=(m // tm,),
        in_specs=[pl.BlockSpec((tm, D_A), lambda i: (i, 0)),
                  pl.BlockSpec((tm, D_B), lambda i: (i, 0)),
                  pl.BlockSpec((k, n), lambda i: (0, 0), pipeline_mode=pl.Buffered(1)),
                  pl.BlockSpec((tm, n), lambda i: (i, 0)),
                  pl.BlockSpec((1, n), lambda i: (0, 0))],
        out_specs=[pl.BlockSpec((tm, n), lambda i: (i, 0)),
                   pl.BlockSpec((tm, n), lambda i: (i, 0))],
        out_shape=[jax.ShapeDtypeStruct((m, n), F32), jax.ShapeDtypeStruct((m, n), BF16)],
        compiler_params=_cparams(1),
        name="out_proj",
    )(o_a, o_b, w_out, resid, g_next.reshape(1, n))


def _down_ple_kernel(close_block, a_ref, w_ref, r_ref, gp_ref, p_ref, wg_ref, wp_ref, gf_ref, o_ref, wp_s):
    @pl.when(pl.program_id(0) == 0)
    def _():
        wp_s[...] = wp_ref[...].astype(BF16)

    h = r_ref[...] + 0.5 * _dot(a_ref[...], w_ref[...])
    hn = _rms(h, gp_ref[...]).astype(BF16)
    gate = jax.nn.sigmoid(_dot(hn, wg_ref[...]))
    emb = _dot(p_ref[...].astype(BF16), wp_s[...])
    h = h + gate * emb
    o_ref[...] = _rms(h, gf_ref[...]) if close_block else h


def _down_ple(a, w_down, resid, g_ple, p, w_gate, w_proj, g_final, close_block, tm=256):
    m, k = a.shape
    n = w_down.shape[1]
    const = dict(pipeline_mode=pl.Buffered(1))
    return pl.pallas_call(
        functools.partial(_down_ple_kernel, close_block),
        grid=(m // tm,),
        in_specs=[pl.BlockSpec((tm, k), lambda i: (i, 0)),
                  pl.BlockSpec((k, n), lambda i: (0, 0), **const),
                  pl.BlockSpec((tm, n), lambda i: (i, 0)),
                  pl.BlockSpec((1, n), lambda i: (0, 0)),
                  pl.BlockSpec((tm, D_PLE), lambda i: (i, 0)),
                  pl.BlockSpec((n, n), lambda i: (0, 0), **const),
                  pl.BlockSpec((D_PLE, n), lambda i: (0, 0), **const),
                  pl.BlockSpec((1, n), lambda i: (0, 0))],
        out_specs=pl.BlockSpec((tm, n), lambda i: (i, 0)),
        out_shape=jax.ShapeDtypeStruct((m, n), F32),
        scratch_shapes=[pltpu.VMEM((D_PLE, n), BF16)],
        compiler_params=_cparams(1),
        name="ffn_down_ple",
    )(a, w_down, resid, g_ple.reshape(1, n), p, w_gate, w_proj, g_final.reshape(1, n))


def _attn_a_scores(q, k):
    return lax.dot_general(q, k, (((1,), (1,)), ((), ())), preferred_element_type=F32)


def _attn_a_softmax(s, bias, key_lo):
    nk = s.shape[1]
    s = s * (SCALE * LOG2E) + bias
    row = lax.broadcasted_iota(jnp.int32, (QB_A, nk), 0)
    col = lax.broadcasted_iota(jnp.int32, (QB_A, nk), 1) + key_lo
    first = row < CHUNK
    valid = (first & (col < BAND_A - CHUNK)) | (jnp.logical_not(first) & (col >= CHUNK))
    s = jnp.where(valid, s, NEG_INF)
    mx = jnp.max(s, axis=-1, keepdims=True)
    e = jnp.exp2(s - mx)
    den = jnp.sum(e, axis=-1, keepdims=True)
    return e.astype(BF16), den


def _attn_a_kernel(g_ref, q_ref, k_ref, v_ref, w_ref, o_ref, w_o_ref, bias_s):

    @pl.when(pl.program_id(1) == 0)
    def _():
        for hh in range(HEADS_PER_STEP_A):
            g = jnp.broadcast_to(g_ref[hh] * LOG2E, (QB_A, REL_ROW))
            bias_s[hh] = pltpu.roll(g, 0, axis=1, stride=1, stride_axis=0)[:, :BAND_A]

    n_blocks = q_ref.shape[0] // QB_A
    items = [(m, hh) for m in range(n_blocks) for hh in range(HEADS_PER_STEP_A)]

    def window(m):
        k0 = max(m * QB_A - PAD_A, 0)
        return k0, (m + 1) * QB_A

    def lanes(hh):
        return slice(hh * HEAD_DIM, (hh + 1) * HEAD_DIM)

    def scores(m, hh):
        k0, k1 = window(m)
        return _attn_a_scores(q_ref[m * QB_A:k1, lanes(hh)], k_ref[k0:k1, lanes(hh)])

    def finish(m, hh, e, den):
        k0, k1 = window(m)
        o = _dot(e, v_ref[k0:k1, lanes(hh)])
        o_ref[m * QB_A:k1, lanes(hh)] = (o / den).astype(o_ref.dtype)

    s_queue = [scores(*items[t]) for t in range(QK_AHEAD)]
    pending = None
    for t, (m, hh) in enumerate(items):
        s = s_queue.pop(0)
        if t + QK_AHEAD < len(items):
            s_queue.append(scores(*items[t + QK_AHEAD]))
        k0, k1 = window(m)
        lo = BAND_A - (k1 - k0)
        e, den = _attn_a_softmax(s, bias_s[hh, :, lo:], lo)
        if pending is not None:
            finish(*pending)
        pending = (m, hh, e, den)
    finish(*pending)

    w_o_ref[...] = w_ref[...].astype(BF16)


def _attn_a(z, bias_rows, w_cast):
    b, s, _ = z.shape
    hp = HEADS_PER_STEP_A
    n_groups = N_HEADS_A // hp
    width = hp * HEAD_DIM
    wk, wn = w_cast.shape
    slab = wk // (n_groups * b)
    return pl.pallas_call(
        _attn_a_kernel,
        grid=(n_groups, b),
        in_specs=[pl.BlockSpec((hp, 1, REL_ROW), lambda h, bi: (h, 0, 0)),
                  pl.BlockSpec((None, s, width), lambda h, bi: (bi, 0, h)),
                  pl.BlockSpec((None, s, width), lambda h, bi: (bi, 0, n_groups + h)),
                  pl.BlockSpec((None, s, width), lambda h, bi: (bi, 0, 2 * n_groups + h)),
                  pl.BlockSpec((slab, wn), lambda h, bi: (h * b + bi, 0))],
        out_specs=[pl.BlockSpec((None, s, width), lambda h, bi: (bi, 0, h)),
                   pl.BlockSpec((slab, wn), lambda h, bi: (h * b + bi, 0))],
        out_shape=[jax.ShapeDtypeStruct((b, s, D_A), BF16),
                   jax.ShapeDtypeStruct((wk, wn), BF16)],
        scratch_shapes=[pltpu.VMEM((hp, QB_A, BAND_A), F32)],
        compiler_params=_cparams(2),
        name="attn_band",
    )(bias_rows, z, z, z, w_cast)


def _fox_prep_kernel(u_ref, wft_ref, bf_ref, f_ref, ft_ref):
    seq, d = u_ref.shape
    wft = jnp.concatenate([wft_ref[...], jnp.zeros((LANES - N_HEADS_B, d), F32)], axis=0).astype(BF16)
    fl = lax.dot_general(u_ref[...], wft, (((1,), (1,)), ((), ())), preferred_element_type=F32)
    fl = fl + bf_ref[...]
    x = jnp.minimum(fl, 0.0) - jnp.log1p(jnp.exp(-jnp.abs(fl)))
    row = lax.broadcasted_iota(jnp.int32, x.shape, 0)
    shift = 1
    while shift < seq:
        x = x + jnp.where(row >= shift, pltpu.roll(x, shift, axis=0), 0.0)
        shift *= 2
    x = x * LOG2E
    f_ref[...] = x
    ft_ref[...] = jnp.transpose(x)[:N_HEADS_B, :]


def _fox_prep(u3, w_in_t, b_f):
    b, s, d = u3.shape
    return pl.pallas_call(
        _fox_prep_kernel,
        grid=(b,),
        in_specs=[pl.BlockSpec((None, s, d), lambda bi: (bi, 0, 0)),
                  pl.BlockSpec((N_HEADS_B, d), lambda bi: (D_QKV // N_HEADS_B, 0)),
                  pl.BlockSpec((1, LANES), lambda bi: (0, 0))],
        out_specs=[pl.BlockSpec((None, s, LANES), lambda bi: (bi, 0, 0)),
                   pl.BlockSpec((None, N_HEADS_B, s), lambda bi: (bi, 0, 0))],
        out_shape=[jax.ShapeDtypeStruct((b, s, LANES), F32),
                   jax.ShapeDtypeStruct((b, N_HEADS_B, s), F32)],
        compiler_params=_cparams(1),
        name="fox_prep",
    )(u3, w_in_t, b_f)


def _fox_kernel(q_ref, k_ref, v_ref, fq_ref, fk_ref, w_ref, o_ref, w_o_ref, v_aug):
    seq = q_ref.shape[0]
    n_blocks = seq // TQ_B
    group = pl.program_id(1)
    lane = lax.broadcasted_iota(jnp.int32, (TQ_B, LANES), 1)
    row = lax.broadcasted_iota(jnp.int32, (TQ_B, TQ_B), 0)
    col = lax.broadcasted_iota(jnp.int32, (TQ_B, TQ_B), 1)
    nt = (((1,), (1,)), ((), ()))

    def lanes(hh):
        return slice(hh * HEAD_DIM, (hh + 1) * HEAD_DIM)

    for hh in range(HEADS_PER_STEP_B):
        v_aug[hh, :, :HEAD_DIM] = v_ref[:, lanes(hh)]
        v_aug[hh, :, HEAD_DIM:] = jnp.ones((seq, HEAD_DIM), BF16)

    items = []
    for t in range(n_blocks):
        for hh in range(HEADS_PER_STEP_B):
            items.append((hh, t if hh % 2 == 0 else n_blocks - 1 - t))

    def scores(hh, qi):
        q0, q1 = qi * TQ_B, (qi + 1) * TQ_B
        q = q_ref[q0:q1, lanes(hh)]
        s_diag = lax.dot_general(q, k_ref[q0:q1, lanes(hh)], nt, preferred_element_type=F32)
        s_past = (lax.dot_general(q, k_ref[0:q0, lanes(hh)], nt, preferred_element_type=F32)
                  if qi > 0 else None)
        return s_diag, s_past

    def softmax(hh, qi, s_d, s_p):
        q0, q1 = qi * TQ_B, (qi + 1) * TQ_B
        h = group * HEADS_PER_STEP_B + hh
        fq = jnp.sum(jnp.where(lane == h, fq_ref[q0:q1, :], 0.0), axis=1, keepdims=True)
        s_d = (s_d * (SCALE * LOG2E) + fq) - fk_ref[pl.ds(h, 1), q0:q1]
        s_d = jnp.where(row >= col, s_d, NEG_INF)
        mx = jnp.max(s_d, axis=-1, keepdims=True)
        if qi > 0:
            s_p = (s_p * (SCALE * LOG2E) + fq) - fk_ref[pl.ds(h, 1), 0:q0]
            mx = jnp.maximum(mx, jnp.max(s_p, axis=-1, keepdims=True))
        e_d = jnp.exp2(s_d - mx).astype(BF16)
        e_p = jnp.exp2(s_p - mx).astype(BF16) if qi > 0 else None
        return e_d, e_p

    def finish(hh, qi, e_d, e_p):
        q0, q1 = qi * TQ_B, (qi + 1) * TQ_B
        acc = _dot(e_d, v_aug[hh, q0:q1, :])
        if e_p is not None:
            acc = acc + _dot(e_p, v_aug[hh, 0:q0, :])
        o_ref[q0:q1, lanes(hh)] = (acc[:, :HEAD_DIM] / acc[:, HEAD_DIM:]).astype(o_ref.dtype)

    s_queue = [scores(*items[t]) for t in range(QK_AHEAD)]
    pending = None
    for t, (hh, qi) in enumerate(items):
        s_d, s_p = s_queue.pop(0)
        if t + QK_AHEAD < len(items):
            s_queue.append(scores(*items[t + QK_AHEAD]))
        e_d, e_p = softmax(hh, qi, s_d, s_p)
        if pending is not None:
            finish(*pending)
        pending = (hh, qi, e_d, e_p)
    finish(*pending)

    w_o_ref[...] = w_ref[...].astype(BF16)


def _fox(z, f_col, f_row, w_cast):
    b, s, _ = z.shape
    hp = HEADS_PER_STEP_B
    n_groups = N_HEADS_B // hp
    width = hp * HEAD_DIM
    base = 3 * N_HEADS_A // hp
    wk, wn = w_cast.shape
    slab = wk // (b * n_groups)
    return pl.pallas_call(
        _fox_kernel,
        grid=(b, n_groups),
        in_specs=[pl.BlockSpec((None, s, width), lambda bi, h: (bi, 0, base + h)),
                  pl.BlockSpec((None, s, width), lambda bi, h: (bi, 0, base + n_groups + h)),
                  pl.BlockSpec((None, s, width), lambda bi, h: (bi, 0, base + 2 * n_groups + h)),
                  pl.BlockSpec((None, s, LANES), lambda bi, h: (bi, 0, 0)),
                  pl.BlockSpec((None, N_HEADS_B, s), lambda bi, h: (bi, 0, 0)),
                  pl.BlockSpec((slab, wn), lambda bi, h: (bi * n_groups + h, 0))],
        out_specs=[pl.BlockSpec((None, s, width), lambda bi, h: (bi, 0, h)),
                   pl.BlockSpec((slab, wn), lambda bi, h: (bi * n_groups + h, 0))],
        out_shape=[jax.ShapeDtypeStruct((b, s, D_B), BF16),
                   jax.ShapeDtypeStruct((wk, wn), BF16)],
        scratch_shapes=[pltpu.VMEM((hp, s, 2 * HEAD_DIM), BF16)],
        compiler_params=_cparams(2),
        name="attn_fox",
    )(z, z, z, f_col, f_row, w_cast)


def _rel_bias_rows(rel_bias):
    far = rel_bias[:, 2 * REL_CLIP:]
    n_head = PAD_A - REL_CLIP
    n_tail = REL_ROW - n_head - (2 * REL_CLIP + 1)
    rows = jnp.concatenate([jnp.tile(far, (1, n_head)), jnp.flip(rel_bias, axis=1),
                            jnp.tile(far, (1, n_tail))], axis=1)
    return rows[:, None, :]


def kernel(x, p, g_ffn1, w_ffn1_gu, w_ffn1_down, g_mix, w_in, b_forget, rel_bias, w_out,
           g_ffn2, w_ffn2_gu, w_ffn2_down, g_ple, w_ple_gate, w_ple_proj, g_final):
    b, s, d = x.shape
    m = b * s
    depth = p.shape[0]
    h = x.reshape(m, d)
    for i in range(depth):
        hid, w_down1 = _gateup(h, g_ffn1[i], w_ffn1_gu[i], w_ffn1_down[i], GATEUP_TN_FUSED)
        h, u, w_down2 = _down(hid, w_down1, h, g_mix[i], w_ffn2_down[i])
        w_in_t = jnp.swapaxes(w_in[i], 0, 1)
        z = _proj(u, w_in_t, D_QKV).reshape(b, s, D_QKV)
        b_f = jnp.pad(b_forget[i], (0, LANES - N_HEADS_B)).reshape(1, LANES)
        f_col, f_row = _fox_prep(u.reshape(b, s, d), w_in_t, b_f)
        o_a, w_out_bf = _attn_a(z, _rel_bias_rows(rel_bias[i]), w_out[i])
        o_b, w_gate_bf = _fox(z, f_col, f_row, w_ple_gate[i])
        h, xn = _outproj(o_a.reshape(m, D_A), o_b.reshape(m, D_B), w_out_bf, h, g_ffn2[i])
        hid, = _gateup(xn, None, w_ffn2_gu[i], None, GATEUP_TN)
        h = _down_ple(hid, w_down2, h, g_ple[i], p[i].reshape(m, D_PLE), w_gate_bf, w_ple_proj[i],
                      g_final, i == depth - 1)
    return h.reshape(b, s, d)
```

```python
import functools

import jax
import jax.numpy as jnp
from jax import lax
from jax.experimental import pallas as pl
from jax.experimental.pallas import tpu as pltpu

F32 = jnp.float32
BF16 = jnp.bfloat16

D_MODEL = 2048
CHUNK = 64
N_LEFT_CHUNKS = 8
HEAD_DIM = 128
N_HEADS_A = 8
N_HEADS_B = 8
D_A = N_HEADS_A * HEAD_DIM
D_B = N_HEADS_B * HEAD_DIM
REL_CLIP = 256
D_FF = 5632
D_PLE = 256
EPS = 1e-6
NEG_INF = -1e30
D_QKV = 3 * D_A + 3 * D_B
SCALE = HEAD_DIM ** -0.5
LOG2E = 1.4426950408889634

LANES = 128
VMEM_LIMIT = 56 * 1024 * 1024

QB_A = 2 * CHUNK
BAND_A = (N_LEFT_CHUNKS + 2) * CHUNK
PAD_A = N_LEFT_CHUNKS * CHUNK
REL_ROW = 1024

GATEUP_ROWS = 1024
GATEUP_TN = 256
GATEUP_TN_FUSED = 128
NORM_ROWS = 512
TQ_B = 256
HEADS_PER_STEP_A = 2
HEADS_PER_STEP_B = 4
QK_AHEAD = 2


def _cparams(n_axes):
    return pltpu.CompilerParams(
        dimension_semantics=("arbitrary",) * n_axes,
        vmem_limit_bytes=VMEM_LIMIT,
    )


def _rms(x, g):
    ms = jnp.mean(x * x, axis=-1, keepdims=True)
    return (x * lax.rsqrt(ms + EPS)) * g


def _dot(a, b):
    return jnp.dot(a, b, preferred_element_type=F32)


def _gateup_kernel(n_norm, has_rider, *refs):
    refs = list(refs)
    x_ref, g_ref = (refs.pop(0), refs.pop(0)) if n_norm else (None, None)
    a_ref = None if n_norm else refs.pop(0)
    wg_ref, wu_ref = refs.pop(0), refs.pop(0)
    wc_ref = refs.pop(0) if has_rider else None
    o_ref = refs.pop(0)
    wc_o_ref = refs.pop(0) if has_rider else None
    w_s = refs.pop(0)
    if n_norm:
        a_ref = refs.pop(0)
    m, tn = o_ref.shape

    def matmul_phase():
        w_s[:, :tn] = wg_ref[...].astype(BF16)
        w_s[:, tn:] = wu_ref[...].astype(BF16)

        if has_rider:
            wc_o_ref[...] = wc_ref[...].astype(BF16)

        w = w_s[...]
        for c in range(m // GATEUP_ROWS):
            rows = slice(c * GATEUP_ROWS, (c + 1) * GATEUP_ROWS)
            r = _dot(a_ref[rows, :], w)
            g = r[:, :tn]
            u = r[:, tn:]
            o_ref[rows, :] = ((g * jax.nn.sigmoid(g)) * u).astype(o_ref.dtype)

    if n_norm:
        step = pl.program_id(0)

        @pl.when(step < n_norm)
        def _():
            r0 = pl.multiple_of(step * NORM_ROWS, NORM_ROWS)
            a_ref[pl.ds(r0, NORM_ROWS), :] = _rms(x_ref[...], g_ref[...]).astype(BF16)

        pl.when(step >= n_norm)(matmul_phase)
    else:
        matmul_phase()


def _gateup(a, g_norm, w_gu, w_cast, tn):
    m, k = a.shape
    nj = D_FF // tn
    n_norm = 0 if g_norm is None else m // NORM_ROWS
    has_rider = w_cast is not None

    def col(s):
        return jnp.maximum(s - n_norm, 0)

    operands, in_specs = [], []
    scratch = [pltpu.VMEM((k, 2 * tn), BF16)]
    if n_norm:
        operands += [a, g_norm.reshape(1, k)]
        in_specs += [pl.BlockSpec((NORM_ROWS, k), lambda s: (jnp.minimum(s, n_norm - 1), 0)),
                     pl.BlockSpec((1, k), lambda s: (0, 0))]
        scratch.append(pltpu.VMEM((m, k), BF16))
    else:
        operands.append(a)
        in_specs.append(pl.BlockSpec((m, k), lambda s: (0, 0), pipeline_mode=pl.Buffered(1)))
    operands += [w_gu, w_gu]
    in_specs += [pl.BlockSpec((k, tn), lambda s: (0, col(s))),
                 pl.BlockSpec((k, tn), lambda s: (0, col(s) + nj))]
    out_specs = [pl.BlockSpec((m, tn), lambda s: (0, col(s)))]
    out_shape = [jax.ShapeDtypeStruct((m, D_FF), BF16)]
    if has_rider:
        wk, wn = w_cast.shape
        slab = wk // nj
        operands.append(w_cast)
        in_specs.append(pl.BlockSpec((slab, wn), lambda s: (col(s), 0)))
        out_specs.append(pl.BlockSpec((slab, wn), lambda s: (col(s), 0)))
        out_shape.append(jax.ShapeDtypeStruct((wk, wn), BF16))
    return pl.pallas_call(
        functools.partial(_gateup_kernel, n_norm, has_rider),
        grid=(n_norm + nj,),
        in_specs=in_specs,
        out_specs=out_specs,
        out_shape=out_shape,
        scratch_shapes=scratch,
        compiler_params=_cparams(1),
        name="ffn_gateup",
    )(*operands)


def _proj_kernel(a_ref, wt_ref, o_ref, w_s):
    @pl.when(pl.program_id(1) == 0)
    def _():
        w_s[...] = wt_ref[...].astype(BF16)

    o_ref[...] = lax.dot_general(a_ref[...], w_s[...], (((1,), (1,)), ((), ())),
                                 preferred_element_type=F32).astype(o_ref.dtype)


def _proj(a, w_t, n_out, tm=2048, tn=1024):
    m, k = a.shape
    return pl.pallas_call(
        _proj_kernel,
        grid=(n_out // tn, m // tm),
        in_specs=[pl.BlockSpec((tm, k), lambda j, i: (i, 0)),
                  pl.BlockSpec((tn, k), lambda j, i: (j, 0))],
        out_specs=pl.BlockSpec((tm, tn), lambda j, i: (i, j)),
        out_shape=jax.ShapeDtypeStruct((m, n_out), BF16),
        scratch_shapes=[pltpu.VMEM((tn, k), BF16)],
        compiler_params=_cparams(2),
        name="in_proj",
    )(a, w_t)


def _down_kernel(a_ref, w_ref, r_ref, g_ref, wc_ref, h_ref, n_ref, wc_o_ref):
    h = r_ref[...] + 0.5 * _dot(a_ref[...], w_ref[...])
    h_ref[...] = h
    n_ref[...] = _rms(h, g_ref[...]).astype(n_ref.dtype)
    wc_o_ref[...] = wc_ref[...].astype(BF16)


def _down(a, w_bf16, resid, g_next, w_cast, tm=256):
    m, k = a.shape
    n = w_bf16.shape[1]
    steps = m // tm
    wk, wn = w_cast.shape
    slab = wk // steps
    return pl.pallas_call(
        _down_kernel,
        grid=(steps,),
        in_specs=[pl.BlockSpec((tm, k), lambda i: (i, 0)),
                  pl.BlockSpec((k, n), lambda i: (0, 0), pipeline_mode=pl.Buffered(1)),
                  pl.BlockSpec((tm, n), lambda i: (i, 0)),
                  pl.BlockSpec((1, n), lambda i: (0, 0)),
                  pl.BlockSpec((slab, wn), lambda i: (i, 0))],
        out_specs=[pl.BlockSpec((tm, n), lambda i: (i, 0)),
                   pl.BlockSpec((tm, n), lambda i: (i, 0)),
                   pl.BlockSpec((slab, wn), lambda i: (i, 0))],
        out_shape=[jax.ShapeDtypeStruct((m, n), F32), jax.ShapeDtypeStruct((m, n), BF16),
                   jax.ShapeDtypeStruct((wk, wn), BF16)],
        compiler_params=_cparams(1),
        name="ffn_down",
    )(a, w_bf16, resid, g_next.reshape(1, n), w_cast)


def _outproj_kernel(oa_ref, ob_ref, w_ref, r_ref, g_ref, h_ref, n_ref):
    acc = _dot(oa_ref[...], w_ref[:D_A, :]) + _dot(ob_ref[...], w_ref[D_A:, :])
    h = r_ref[...] + acc
    h_ref[...] = h
    n_ref[...] = _rms(h, g_ref[...]).astype(n_ref.dtype)


def _outproj(o_a, o_b, w_out, resid, g_next, tm=512):
    m = o_a.shape[0]
    k, n = w_out.shape
    return pl.pallas_call(
        _outproj_kernel,
        grid=(m // tm,),
        in_specs=[pl.BlockSpec((tm, D_A), lambda i: (i, 0)),
                  pl.BlockSpec((tm, D_B), lambda i: (i, 0)),
                  pl.BlockSpec((k, n), lambda i: (0, 0), pipeline_mode=pl.Buffered(1)),
                  pl.BlockSpec((tm, n), lambda i: (i, 0)),
                  pl.BlockSpec((1, n), lambda i: (0, 0))],
        out_specs=[pl.BlockSpec((tm, n), lambda i: (i, 0)),
                   pl.BlockSpec((tm, n), lambda i: (i, 0))],
        out_shape=[jax.ShapeDtypeStruct((m, n), F32), jax.ShapeDtypeStruct((m, n), BF16)],
        compiler_params=_cparams(1),
        name="out_proj",
    )(o_a, o_b, w_out, resid, g_next.reshape(1, n))


def _down_ple_kernel(close_block, a_ref, w_ref, r_ref, gp_ref, p_ref, wg_ref, wp_ref, gf_ref, o_ref, wp_s):
    @pl.when(pl.program_id(0) == 0)
    def _():
        wp_s[...] = wp_ref[...].astype(BF16)

    h = r_ref[...] + 0.5 * _dot(a_ref[...], w_ref[...])
    hn = _rms(h, gp_ref[...]).astype(BF16)
    gate = jax.nn.sigmoid(_dot(hn, wg_ref[...]))
    emb = _dot(p_ref[...].astype(BF16), wp_s[...])
    h = h + gate * emb
    o_ref[...] = _rms(h, gf_ref[...]) if close_block else h


def _down_ple(a, w_down, resid, g_ple, p, w_gate, w_proj, g_final, close_block, tm=256):
    m, k = a.shape
    n = w_down.shape[1]
    const = dict(pipeline_mode=pl.Buffered(1))
    return pl.pallas_call(
        functools.partial(_down_ple_kernel, close_block),
        grid=(m // tm,),
        in_specs=[pl.BlockSpec((tm, k), lambda i: (i, 0)),
                  pl.BlockSpec((k, n), lambda i: (0, 0), **const),
                  pl.BlockSpec((tm, n), lambda i: (i, 0)),
                  pl.BlockSpec((1, n), lambda i: (0, 0)),
                  pl.BlockSpec((tm, D_PLE), lambda i: (i, 0)),
                  pl.BlockSpec((n, n), lambda i: (0, 0), **const),
                  pl.BlockSpec((D_PLE, n), lambda i: (0, 0), **const),
                  pl.BlockSpec((1, n), lambda i: (0, 0))],
        out_specs=pl.BlockSpec((tm, n), lambda i: (i, 0)),
        out_shape=jax.ShapeDtypeStruct((m, n), F32),
        scratch_shapes=[pltpu.VMEM((D_PLE, n), BF16)],
        compiler_params=_cparams(1),
        name="ffn_down_ple",
    )(a, w_down, resid, g_ple.reshape(1, n), p, w_gate, w_proj, g_final.reshape(1, n))


def _attn_a_scores(q, k):
    return lax.dot_general(q, k, (((1,), (1,)), ((), ())), preferred_element_type=F32)


def _attn_a_softmax(s, bias, key_lo):
    nk = s.shape[1]
    s = s * (SCALE * LOG2E) + bias
    row = lax.broadcasted_iota(jnp.int32, (QB_A, nk), 0)
    col = lax.broadcasted_iota(jnp.int32, (QB_A, nk), 1) + key_lo
    first = row < CHUNK
    valid = (first & (col < BAND_A - CHUNK)) | (jnp.logical_not(first) & (col >= CHUNK))
    s = jnp.where(valid, s, NEG_INF)
    mx = jnp.max(s, axis=-1, keepdims=True)
    e = jnp.exp2(s - mx)
    den = jnp.sum(e, axis=-1, keepdims=True)
    return e.astype(BF16), den


def _attn_a_kernel(g_ref, q_ref, k_ref, v_ref, w_ref, o_ref, w_o_ref, bias_s):
    @pl.when(pl.program_id(1) == 0)
    def _():
        for hh in range(HEADS_PER_STEP_A):
            g = jnp.broadcast_to(g_ref[hh] * LOG2E, (QB_A, REL_ROW))
            bias_s[hh] = pltpu.roll(g, 0, axis=1, stride=1, stride_axis=0)[:, :BAND_A]

    n_blocks = q_ref.shape[0] // QB_A
    items = [(m, hh) for m in range(n_blocks) for hh in range(HEADS_PER_STEP_A)]

    def window(m):
        k0 = max(m * QB_A - PAD_A, 0)
        return k0, (m + 1) * QB_A

    def lanes(hh):
        return slice(hh * HEAD_DIM, (hh + 1) * HEAD_DIM)

    def scores(m, hh):
        k0, k1 = window(m)
        return _attn_a_scores(q_ref[m * QB_A:k1, lanes(hh)], k_ref[k0:k1, lanes(hh)])

    def finish(m, hh, e, den):
        k0, k1 = window(m)
        o = _dot(e, v_ref[k0:k1, lanes(hh)])
        o_ref[m * QB_A:k1, lanes(hh)] = (o / den).astype(o_ref.dtype)

    s_queue = [scores(*items[t]) for t in range(QK_AHEAD)]
    pending = None
    for t, (m, hh) in enumerate(items):
        s = s_queue.pop(0)
        if t + QK_AHEAD < len(items):
            s_queue.append(scores(*items[t + QK_AHEAD]))
        k0, k1 = window(m)
        lo = BAND_A - (k1 - k0)
        e, den = _attn_a_softmax(s, bias_s[hh, :, lo:], lo)
        if pending is not None:
            finish(*pending)
        pending = (m, hh, e, den)
    finish(*pending)

    w_o_ref[...] = w_ref[...].astype(BF16)


def _attn_a(z, bias_rows, w_cast):
    b, s, _ = z.shape
    hp = HEADS_PER_STEP_A
    n_groups = N_HEADS_A // hp
    width = hp * HEAD_DIM
    wk, wn = w_cast.shape
    slab = wk // (n_groups * b)
    return pl.pallas_call(
        _attn_a_kernel,
        grid=(n_groups, b),
        in_specs=[pl.BlockSpec((hp, 1, REL_ROW), lambda h, bi: (h, 0, 0)),
                  pl.BlockSpec((None, s, width), lambda h, bi: (bi, 0, h)),
                  pl.BlockSpec((None, s, width), lambda h, bi: (bi, 0, n_groups + h)),
                  pl.BlockSpec((None, s, width), lambda h, bi: (bi, 0, 2 * n_groups + h)),
                  pl.BlockSpec((slab, wn), lambda h, bi: (h * b + bi, 0))],
        out_specs=[pl.BlockSpec((None, s, width), lambda h, bi: (bi, 0, h)),
                   pl.BlockSpec((slab, wn), lambda h, bi: (h * b + bi, 0))],
        out_shape=[jax.ShapeDtypeStruct((b, s, D_A), BF16),
                   jax.ShapeDtypeStruct((wk, wn), BF16)],
        scratch_shapes=[pltpu.VMEM((hp, QB_A, BAND_A), F32)],
        compiler_params=_cparams(2),
        name="attn_band",
    )(bias_rows, z, z, z, w_cast)


def _fox_prep_kernel(u_ref, wft_ref, bf_ref, f_ref, ft_ref):
    seq, d = u_ref.shape
    wft = jnp.concatenate([wft_ref[...], jnp.zeros((LANES - N_HEADS_B, d), F32)], axis=0).astype(BF16)
    fl = lax.dot_general(u_ref[...], wft, (((1,), (1,)), ((), ())), preferred_element_type=F32)
    fl = fl + bf_ref[...]
    x = jnp.minimum(fl, 0.0) - jnp.log1p(jnp.exp(-jnp.abs(fl)))
    row = lax.broadcasted_iota(jnp.int32, x.shape, 0)
    shift = 1
    while shift < seq:
        x = x + jnp.where(row >= shift, pltpu.roll(x, shift, axis=0), 0.0)
        shift *= 2
    x = x * LOG2E
    f_ref[...] = x
    ft_ref[...] = jnp.transpose(x)[:N_HEADS_B, :]


def _fox_prep(u3, w_in_t, b_f):
    b, s, d = u3.shape
    return pl.pallas_call(
        _fox_prep_kernel,
        grid=(b,),
        in_specs=[pl.BlockSpec((None, s, d), lambda bi: (bi, 0, 0)),
                  pl.BlockSpec((N_HEADS_B, d), lambda bi: (D_QKV // N_HEADS_B, 0)),
                  pl.BlockSpec((1, LANES), lambda bi: (0, 0))],
        out_specs=[pl.BlockSpec((None, s, LANES), lambda bi: (bi, 0, 0)),
                   pl.BlockSpec((None, N_HEADS_B, s), lambda bi: (bi, 0, 0))],
        out_shape=[jax.ShapeDtypeStruct((b, s, LANES), F32),
                   jax.ShapeDtypeStruct((b, N_HEADS_B, s), F32)],
        compiler_params=_cparams(1),
        name="fox_prep",
    )(u3, w_in_t, b_f)


def _fox_kernel(q_ref, k_ref, v_ref, fq_ref, fk_ref, w_ref, o_ref, w_o_ref, v_aug):
    seq = q_ref.shape[0]
    n_blocks = seq // TQ_B
    group = pl.program_id(1)
    lane = lax.broadcasted_iota(jnp.int32, (TQ_B, LANES), 1)
    row = lax.broadcasted_iota(jnp.int32, (TQ_B, TQ_B), 0)
    col = lax.broadcasted_iota(jnp.int32, (TQ_B, TQ_B), 1)
    nt = (((1,), (1,)), ((), ()))

    def lanes(hh):
        return slice(hh * HEAD_DIM, (hh + 1) * HEAD_DIM)

    for hh in range(HEADS_PER_STEP_B):
        v_aug[hh, :, :HEAD_DIM] = v_ref[:, lanes(hh)]
        v_aug[hh, :, HEAD_DIM:] = jnp.ones((seq, HEAD_DIM), BF16)

    items = []
    for t in range(n_blocks):
        for hh in range(HEADS_PER_STEP_B):
            items.append((hh, t if hh % 2 == 0 else n_blocks - 1 - t))

    def scores(hh, qi):
        q0, q1 = qi * TQ_B, (qi + 1) * TQ_B
        q = q_ref[q0:q1, lanes(hh)]
        s_diag = lax.dot_general(q, k_ref[q0:q1, lanes(hh)], nt, preferred_element_type=F32)
        s_past = (lax.dot_general(q, k_ref[0:q0, lanes(hh)], nt, preferred_element_type=F32)
                  if qi > 0 else None)
        return s_diag, s_past

    def softmax(hh, qi, s_d, s_p):
        q0, q1 = qi * TQ_B, (qi + 1) * TQ_B
        h = group * HEADS_PER_STEP_B + hh
        fq = jnp.sum(jnp.where(lane == h, fq_ref[q0:q1, :], 0.0), axis=1, keepdims=True)
        s_d = (s_d * (SCALE * LOG2E) + fq) - fk_ref[pl.ds(h, 1), q0:q1]
        s_d = jnp.where(row >= col, s_d, NEG_INF)
        mx = jnp.max(s_d, axis=-1, keepdims=True)
        if qi > 0:
            s_p = (s_p * (SCALE * LOG2E) + fq) - fk_ref[pl.ds(h, 1), 0:q0]
            mx = jnp.maximum(mx, jnp.max(s_p, axis=-1, keepdims=True))
        e_d = jnp.exp2(s_d - mx).astype(BF16)
        e_p = jnp.exp2(s_p - mx).astype(BF16) if qi > 0 else None
        return e_d, e_p

    def finish(hh, qi, e_d, e_p):
        q0, q1 = qi * TQ_B, (qi + 1) * TQ_B
        acc = _dot(e_d, v_aug[hh, q0:q1, :])
        if e_p is not None:
            acc = acc + _dot(e_p, v_aug[hh, 0:q0, :])
        o_ref[q0:q1, lanes(hh)] = (acc[:, :HEAD_DIM] / acc[:, HEAD_DIM:]).astype(o_ref.dtype)

    s_queue = [scores(*items[t]) for t in range(QK_AHEAD)]
    pending = None
    for t, (hh, qi) in enumerate(items):
        s_d, s_p = s_queue.pop(0)
        if t + QK_AHEAD < len(items):
            s_queue.append(scores(*items[t + QK_AHEAD]))
        e_d, e_p = softmax(hh, qi, s_d, s_p)
        if pending is not None:
            finish(*pending)
        pending = (hh, qi, e_d, e_p)
    finish(*pending)

    w_o_ref[...] = w_ref[...].astype(BF16)


def _fox(z, f_col, f_row, w_cast):
    b, s, _ = z.shape
    hp = HEADS_PER_STEP_B
    n_groups = N_HEADS_B // hp
    width = hp * HEAD_DIM
    base = 3 * N_HEADS_A // hp
    wk, wn = w_cast.shape
    slab = wk // (b * n_groups)
    return pl.pallas_call(
        _fox_kernel,
        grid=(b, n_groups),
        in_specs=[pl.BlockSpec((None, s, width), lambda bi, h: (bi, 0, base + h)),
                  pl.BlockSpec((None, s, width), lambda bi, h: (bi, 0, base + n_groups + h)),
                  pl.BlockSpec((None, s, width), lambda bi, h: (bi, 0, base + 2 * n_groups + h)),
                  pl.BlockSpec((None, s, LANES), lambda bi, h: (bi, 0, 0)),
                  pl.BlockSpec((None, N_HEADS_B, s), lambda bi, h: (bi, 0, 0)),
                  pl.BlockSpec((slab, wn), lambda bi, h: (bi * n_groups + h, 0))],
        out_specs=[pl.BlockSpec((None, s, width), lambda bi, h: (bi, 0, h)),
                   pl.BlockSpec((slab, wn), lambda bi, h: (bi * n_groups + h, 0))],
        out_shape=[jax.ShapeDtypeStruct((b, s, D_B), BF16),
                   jax.ShapeDtypeStruct((wk, wn), BF16)],
        scratch_shapes=[pltpu.VMEM((hp, s, 2 * HEAD_DIM), BF16)],
        compiler_params=_cparams(2),
        name="attn_fox",
    )(z, z, z, f_col, f_row, w_cast)


def _rel_bias_rows(rel_bias):
    far = rel_bias[:, 2 * REL_CLIP:]
    n_head = PAD_A - REL_CLIP
    n_tail = REL_ROW - n_head - (2 * REL_CLIP + 1)
    rows = jnp.concatenate([jnp.tile(far, (1, n_head)), jnp.flip(rel_bias, axis=1),
                            jnp.tile(far, (1, n_tail))], axis=1)
    return rows[:, None, :]


def kernel(x, p, g_ffn1, w_ffn1_gu, w_ffn1_down, g_mix, w_in, b_forget, rel_bias, w_out,
           g_ffn2, w_ffn2_gu, w_ffn2_down, g_ple, w_ple_gate, w_ple_proj, g_final):
    b, s, d = x.shape
    m = b * s
    depth = p.shape[0]
    h = x.reshape(m, d)
    for i in range(depth):
        hid, w_down1 = _gateup(h, g_ffn1[i], w_ffn1_gu[i], w_ffn1_down[i], GATEUP_TN_FUSED)
        h, u, w_down2 = _down(hid, w_down1, h, g_mix[i], w_ffn2_down[i])
        w_in_t = jnp.swapaxes(w_in[i], 0, 1)
        z = _proj(u, w_in_t, D_QKV).reshape(b, s, D_QKV)
        b_f = jnp.pad(b_forget[i], (0, LANES - N_HEADS_B)).reshape(1, LANES)
        f_col, f_row = _fox_prep(u.reshape(b, s, d), w_in_t, b_f)
        o_a, w_out_bf = _attn_a(z, _rel_bias_rows(rel_bias[i]), w_out[i])
        o_b, w_gate_bf = _fox(z, f_col, f_row, w_ple_gate[i])
        h, xn = _outproj(o_a.reshape(m, D_A), o_b.reshape(m, D_B), w_out_bf, h, g_ffn2[i])
        hid, = _gateup(xn, None, w_ffn2_gu[i], None, GATEUP_TN)
        h = _down_ple(hid, w_down2, h, g_ple[i], p[i].reshape(m, D_PLE), w_gate_bf, w_ple_proj[i],
                      g_final, i == depth - 1)
    return h.reshape(b, s, d)
```

```python
import functools

import jax
import jax.numpy as jnp
from jax import lax
from jax.experimental import pallas as pl
from jax.experimental.pallas import tpu as pltpu

F32 = jnp.float32
BF16 = jnp.bfloat16

D_MODEL = 2048
CHUNK = 64
N_LEFT_CHUNKS = 8
HEAD_DIM = 128
N_HEADS_A = 8
N_HEADS_B = 8
D_A = N_HEADS_A * HEAD_DIM
D_B = N_HEADS_B * HEAD_DIM
REL_CLIP = 256
D_FF = 5632
D_PLE = 256
EPS = 1e-6
NEG_INF = -1e30
D_QKV = 3 * D_A + 3 * D_B
SCALE = HEAD_DIM ** -0.5
LOG2E = 1.4426950408889634

LANES = 128
VMEM_LIMIT = 56 * 1024 * 1024

QB_A = 2 * CHUNK
BAND_A = (N_LEFT_CHUNKS + 2) * CHUNK
PAD_A = N_LEFT_CHUNKS * CHUNK
REL_ROW = 1024

GATEUP_ROWS = 1024
GATEUP_TN = 256
GATEUP_TN_FUSED = 128
NORM_ROWS = 512
TQ_B = 256
HEADS_PER_STEP_A = 2
HEADS_PER_STEP_B = 4
QK_AHEAD = 2


def _cparams(n_axes):
    return pltpu.CompilerParams(
        dimension_semantics=("arbitrary",) * n_axes,
        vmem_limit_bytes=VMEM_LIMIT,
    )


def _rms(x, g):
    ms = jnp.mean(x * x, axis=-1, keepdims=True)
    return (x * lax.rsqrt(ms + EPS)) * g


def _dot(a, b):
    return jnp.dot(a, b, preferred_element_type=F32)


def _gateup_kernel(n_norm, has_rider, *refs):
    refs = list(refs)
    x_ref, g_ref = (refs.pop(0), refs.pop(0)) if n_norm else (None, None)
    a_ref = None if n_norm else refs.pop(0)
    wg_ref, wu_ref = refs.pop(0), refs.pop(0)
    wc_ref = refs.pop(0) if has_rider else None
    o_ref = refs.pop(0)
    wc_o_ref = refs.pop(0) if has_rider else None
    w_s = refs.pop(0)
    if n_norm:
        a_ref = refs.pop(0)
    m, tn = o_ref.shape

    def matmul_phase():
        w_s[:, :tn] = wg_ref[...].astype(BF16)
        w_s[:, tn:] = wu_ref[...].astype(BF16)

        if has_rider:
            wc_o_ref[...] = wc_ref[...].astype(BF16)

        w = w_s[...]
        for c in range(m // GATEUP_ROWS):
            rows = slice(c * GATEUP_ROWS, (c + 1) * GATEUP_ROWS)
            r = _dot(a_ref[rows, :], w)
            g = r[:, :tn]
            u = r[:, tn:]
            o_ref[rows, :] = ((g * jax.nn.sigmoid(g)) * u).astype(o_ref.dtype)

    if n_norm:
        step = pl.program_id(0)

        @pl.when(step < n_norm)
        def _():
            r0 = pl.multiple_of(step * NORM_ROWS, NORM_ROWS)
            a_ref[pl.ds(r0, NORM_ROWS), :] = _rms(x_ref[...], g_ref[...]).astype(BF16)

        pl.when(step >= n_norm)(matmul_phase)
    else:
        matmul_phase()


def _gateup(a, g_norm, w_gu, w_cast, tn):
    m, k = a.shape
    nj = D_FF // tn
    n_norm = 0 if g_norm is None else m // NORM_ROWS
    has_rider = w_cast is not None

    def col(s):
        return jnp.maximum(s - n_norm, 0)

    operands, in_specs = [], []
    scratch = [pltpu.VMEM((k, 2 * tn), BF16)]
    if n_norm:
        operands += [a, g_norm.reshape(1, k)]
        in_specs += [pl.BlockSpec((NORM_ROWS, k), lambda s: (jnp.minimum(s, n_norm - 1), 0)),
                     pl.BlockSpec((1, k), lambda s: (0, 0))]
        scratch.append(pltpu.VMEM((m, k), BF16))
    else:
        operands.append(a)
        in_specs.append(pl.BlockSpec((m, k), lambda s: (0, 0), pipeline_mode=pl.Buffered(1)))
    operands += [w_gu, w_gu]
    in_specs += [pl.BlockSpec((k, tn), lambda s: (0, col(s))),
                 pl.BlockSpec((k, tn), lambda s: (0, col(s) + nj))]
    out_specs = [pl.BlockSpec((m, tn), lambda s: (0, col(s)))]
    out_shape = [jax.ShapeDtypeStruct((m, D_FF), BF16)]
    if has_rider:
        wk, wn = w_cast.shape
        slab = wk // nj
        operands.append(w_cast)
        in_specs.append(pl.BlockSpec((slab, wn), lambda s: (col(s), 0)))
        out_specs.append(pl.BlockSpec((slab, wn), lambda s: (col(s), 0)))
        out_shape.append(jax.ShapeDtypeStruct((wk, wn), BF16))
    return pl.pallas_call(
        functools.partial(_gateup_kernel, n_norm, has_rider),
        grid=(n_norm + nj,),
        in_specs=in_specs,
        out_specs=out_specs,
        out_shape=out_shape,
        scratch_shapes=scratch,
        compiler_params=_cparams(1),
        name="ffn_gateup",
    )(*operands)


def _proj_kernel(a_ref, wt_ref, o_ref, w_s):
    @pl.when(pl.program_id(1) == 0)
    def _():
        w_s[...] = wt_ref[...].astype(BF16)

    r = lax.dot_general(a_ref[...], w_s[...], (((1,), (1,)), ((), ())), preferred_element_type=F32)
    for hh in range(o_ref.shape[0]):
        o_ref[hh] = r[:, hh * HEAD_DIM:(hh + 1) * HEAD_DIM].astype(o_ref.dtype)


def _proj(a, w_t, n_out, batch, tn=1024):
    m, k = a.shape
    seq = m // batch
    heads_per_tile = tn // HEAD_DIM
    return pl.pallas_call(
        _proj_kernel,
        grid=(n_out // tn, batch),
        in_specs=[pl.BlockSpec((seq, k), lambda j, i: (i, 0)),
                  pl.BlockSpec((tn, k), lambda j, i: (j, 0))],
        out_specs=pl.BlockSpec((None, heads_per_tile, seq, HEAD_DIM), lambda j, i: (i, j, 0, 0)),
        out_shape=jax.ShapeDtypeStruct((batch, n_out // HEAD_DIM, seq, HEAD_DIM), BF16),
        scratch_shapes=[pltpu.VMEM((tn, k), BF16)],
        compiler_params=_cparams(2),
        name="in_proj",
    )(a, w_t)


def _down_kernel(a_ref, w_ref, r_ref, g_ref, wc_ref, h_ref, n_ref, wc_o_ref):
    h = r_ref[...] + 0.5 * _dot(a_ref[...], w_ref[...])
    h_ref[...] = h
    n_ref[...] = _rms(h, g_ref[...]).astype(n_ref.dtype)
    wc_o_ref[...] = wc_ref[...].astype(BF16)


def _down(a, w_bf16, resid, g_next, w_cast, tm=256):
    m, k = a.shape
    n = w_bf16.shape[1]
    steps = m // tm
    wk, wn = w_cast.shape
    slab = wk // steps
    return pl.pallas_call(
        _down_kernel,
        grid=(steps,),
        in_specs=[pl.BlockSpec((tm, k), lambda i: (i, 0)),
                  pl.BlockSpec((k, n), lambda i: (0, 0), pipeline_mode=pl.Buffered(1)),
                  pl.BlockSpec((tm, n), lambda i: (i, 0)),
                  pl.BlockSpec((1, n), lambda i: (0, 0)),
                  pl.BlockSpec((slab, wn), lambda i: (i, 0))],
        out_specs=[pl.BlockSpec((tm, n), lambda i: (i, 0)),
                   pl.BlockSpec((tm, n), lambda i: (i, 0)),
                   pl.BlockSpec((slab, wn), lambda i: (i, 0))],
        out_shape=[jax.ShapeDtypeStruct((m, n), F32), jax.ShapeDtypeStruct((m, n), BF16),
                   jax.ShapeDtypeStruct((wk, wn), BF16)],
        compiler_params=_cparams(1),
        name="ffn_down",
    )(a, w_bf16, resid, g_next.reshape(1, n), w_cast)


def _outproj_kernel(oa_ref, ob_ref, w_ref, r_ref, g_ref, h_ref, n_ref):
    acc = _dot(oa_ref[...], w_ref[:D_A, :]) + _dot(ob_ref[...], w_ref[D_A:, :])
    h = r_ref[...] + acc
    h_ref[...] = h
    n_ref[...] = _rms(h, g_ref[...]).astype(n_ref.dtype)


def _outproj(o_a, o_b, w_out, resid, g_next, tm=512):
    m = o_a.shape[0]
    k, n = w_out.shape
    return pl.pallas_call(
        _outproj_kernel,
        grid=(m // tm,),
        in_specs=[pl.BlockSpec((tm, D_A), lambda i: (i, 0)),
                  pl.BlockSpec((tm, D_B), lambda i: (i, 0)),
                  pl.BlockSpec((k, n), lambda i: (0, 0), pipeline_mode=pl.Buffered(1)),
                  pl.BlockSpec((tm, n), lambda i: (i, 0)),
                  pl.BlockSpec((1, n), lambda i: (0, 0))],
        out_specs=[pl.BlockSpec((tm, n), lambda i: (i, 0)),
                   pl.BlockSpec((tm, n), lambda i: (i, 0))],
        out_shape=[jax.ShapeDtypeStruct((m, n), F32), jax.ShapeDtypeStruct((m, n), BF16)],
        compiler_params=_cparams(1),
        name="out_proj",
    )(o_a, o_b, w_out, resid, g_next.reshape(1, n))


def _down_ple_kernel(close_block, a_ref, w_ref, r_ref, gp_ref, p_ref, wg_ref, wp_ref, gf_ref, o_ref, wp_s):
    @pl.when(pl.program_id(0) == 0)
    def _():
        wp_s[...] = wp_ref[...].astype(BF16)

    h = r_ref[...] + 0.5 * _dot(a_ref[...], w_ref[...])
    hn = _rms(h, gp_ref[...]).astype(BF16)
    gate = jax.nn.sigmoid(_dot(hn, wg_ref[...]))
    emb = _dot(p_ref[...].astype(BF16), wp_s[...])
    h = h + gate * emb
    o_ref[...] = _rms(h, gf_ref[...]) if close_block else h


def _down_ple(a, w_down, resid, g_ple, p, w_gate, w_proj, g_final, close_block, tm=256):
    m, k = a.shape
    n = w_down.shape[1]
    const = dict(pipeline_mode=pl.Buffered(1))
    return pl.pallas_call(
        functools.partial(_down_ple_kernel, close_block),
        grid=(m // tm,),
        in_specs=[pl.BlockSpec((tm, k), lambda i: (i, 0)),
                  pl.BlockSpec((k, n), lambda i: (0, 0), **const),
                  pl.BlockSpec((tm, n), lambda i: (i, 0)),
                  pl.BlockSpec((1, n), lambda i: (0, 0)),
                  pl.BlockSpec((tm, D_PLE), lambda i: (i, 0)),
                  pl.BlockSpec((n, n), lambda i: (0, 0), **const),
                  pl.BlockSpec((D_PLE, n), lambda i: (0, 0), **const),
                  pl.BlockSpec((1, n), lambda i: (0, 0))],
        out_specs=pl.BlockSpec((tm, n), lambda i: (i, 0)),
        out_shape=jax.ShapeDtypeStruct((m, n), F32),
        scratch_shapes=[pltpu.VMEM((D_PLE, n), BF16)],
        compiler_params=_cparams(1),
        name="ffn_down_ple",
    )(a, w_down, resid, g_ple.reshape(1, n), p, w_gate, w_proj, g_final.reshape(1, n))


def _attn_a_scores(q, k):
    return lax.dot_general(q, k, (((1,), (1,)), ((), ())), preferred_element_type=F32)


def _attn_a_softmax(s, bias, key_lo):
    nk = s.shape[1]
    s = s * (SCALE * LOG2E) + bias
    row = lax.broadcasted_iota(jnp.int32, (QB_A, nk), 0)
    col = lax.broadcasted_iota(jnp.int32, (QB_A, nk), 1) + key_lo
    first = row < CHUNK
    valid = (first & (col < BAND_A - CHUNK)) | (jnp.logical_not(first) & (col >= CHUNK))
    s = jnp.where(valid, s, NEG_INF)
    mx = jnp.max(s, axis=-1, keepdims=True)
    e = jnp.exp2(s - mx)
    den = jnp.sum(e, axis=-1, keepdims=True)
    return e.astype(BF16), den


def _attn_a_kernel(g_ref, q_ref, k_ref, v_ref, w_ref, o_ref, w_o_ref, bias_s):
    @pl.when(pl.program_id(1) == 0)
    def _():
        for hh in range(HEADS_PER_STEP_A):
            g = jnp.broadcast_to(g_ref[hh] * LOG2E, (QB_A, REL_ROW))
            bias_s[hh] = pltpu.roll(g, 0, axis=1, stride=1, stride_axis=0)[:, :BAND_A]

    n_blocks = q_ref.shape[1] // QB_A
    items = [(m, hh) for m in range(n_blocks) for hh in range(HEADS_PER_STEP_A)]

    def window(m):
        k0 = max(m * QB_A - PAD_A, 0)
        return k0, (m + 1) * QB_A

    def lanes(hh):
        return slice(hh * HEAD_DIM, (hh + 1) * HEAD_DIM)

    def scores(m, hh):
        k0, k1 = window(m)
        return _attn_a_scores(q_ref[hh, m * QB_A:k1, :], k_ref[hh, k0:k1, :])

    def finish(m, hh, e, den):
        k0, k1 = window(m)
        o = _dot(e, v_ref[hh, k0:k1, :])
        o_ref[m * QB_A:k1, lanes(hh)] = (o / den).astype(o_ref.dtype)

    s_queue = [scores(*items[t]) for t in range(QK_AHEAD)]
    pending = None
    for t, (m, hh) in enumerate(items):
        s = s_queue.pop(0)
        if t + QK_AHEAD < len(items):
            s_queue.append(scores(*items[t + QK_AHEAD]))
        k0, k1 = window(m)
        lo = BAND_A - (k1 - k0)
        e, den = _attn_a_softmax(s, bias_s[hh, :, lo:], lo)
        if pending is not None:
            finish(*pending)
        pending = (m, hh, e, den)
    finish(*pending)

    w_o_ref[...] = w_ref[...].astype(BF16)


def _attn_a(z, bias_rows, w_cast):
    b, _, s, _ = z.shape
    hp = HEADS_PER_STEP_A
    n_groups = N_HEADS_A // hp
    width = hp * HEAD_DIM
    wk, wn = w_cast.shape
    slab = wk // (n_groups * b)
    return pl.pallas_call(
        _attn_a_kernel,
        grid=(n_groups, b),
        in_specs=[pl.BlockSpec((hp, 1, REL_ROW), lambda h, bi: (h, 0, 0)),
                  pl.BlockSpec((None, hp, s, HEAD_DIM), lambda h, bi: (bi, h, 0, 0)),
                  pl.BlockSpec((None, hp, s, HEAD_DIM), lambda h, bi: (bi, n_groups + h, 0, 0)),
                  pl.BlockSpec((None, hp, s, HEAD_DIM), lambda h, bi: (bi, 2 * n_groups + h, 0, 0)),
                  pl.BlockSpec((slab, wn), lambda h, bi: (h * b + bi, 0))],
        out_specs=[pl.BlockSpec((None, s, width), lambda h, bi: (bi, 0, h)),
                   pl.BlockSpec((slab, wn), lambda h, bi: (h * b + bi, 0))],
        out_shape=[jax.ShapeDtypeStruct((b, s, D_A), BF16),
                   jax.ShapeDtypeStruct((wk, wn), BF16)],
        scratch_shapes=[pltpu.VMEM((hp, QB_A, BAND_A), F32)],
        compiler_params=_cparams(2),
        name="attn_band",
    )(bias_rows, z, z, z, w_cast)


def _fox_prep_kernel(u_ref, wft_ref, bf_ref, f_ref, ft_ref):
    seq, d = u_ref.shape
    wft = jnp.concatenate([wft_ref[...], jnp.zeros((LANES - N_HEADS_B, d), F32)], axis=0).astype(BF16)
    fl = lax.dot_general(u_ref[...], wft, (((1,), (1,)), ((), ())), preferred_element_type=F32)
    fl = fl + bf_ref[...]
    x = jnp.minimum(fl, 0.0) - jnp.log1p(jnp.exp(-jnp.abs(fl)))
    row = lax.broadcasted_iota(jnp.int32, x.shape, 0)
    shift = 1
    while shift < seq:
        x = x + jnp.where(row >= shift, pltpu.roll(x, shift, axis=0), 0.0)
        shift *= 2
    x = x * LOG2E
    f_ref[...] = x
    ft_ref[...] = jnp.transpose(x)[:N_HEADS_B, :]


def _fox_prep(u3, w_in_t, b_f):
    b, s, d = u3.shape
    return pl.pallas_call(
        _fox_prep_kernel,
        grid=(b,),
        in_specs=[pl.BlockSpec((None, s, d), lambda bi: (bi, 0, 0)),
                  pl.BlockSpec((N_HEADS_B, d), lambda bi: (D_QKV // N_HEADS_B, 0)),
                  pl.BlockSpec((1, LANES), lambda bi: (0, 0))],
        out_specs=[pl.BlockSpec((None, s, LANES), lambda bi: (bi, 0, 0)),
                   pl.BlockSpec((None, N_HEADS_B, s), lambda bi: (bi, 0, 0))],
        out_shape=[jax.ShapeDtypeStruct((b, s, LANES), F32),
                   jax.ShapeDtypeStruct((b, N_HEADS_B, s), F32)],
        compiler_params=_cparams(1),
        name="fox_prep",
    )(u3, w_in_t, b_f)


def _fox_kernel(q_ref, k_ref, v_ref, fq_ref, fk_ref, w_ref, o_ref, w_o_ref, v_aug):
    seq = q_ref.shape[1]
    n_blocks = seq // TQ_B
    group = pl.program_id(1)
    lane = lax.broadcasted_iota(jnp.int32, (TQ_B, LANES), 1)
    row = lax.broadcasted_iota(jnp.int32, (TQ_B, TQ_B), 0)
    col = lax.broadcasted_iota(jnp.int32, (TQ_B, TQ_B), 1)
    nt = (((1,), (1,)), ((), ()))

    def lanes(hh):
        return slice(hh * HEAD_DIM, (hh + 1) * HEAD_DIM)

    for hh in range(HEADS_PER_STEP_B):
        v_aug[hh, :, :HEAD_DIM] = v_ref[hh]
        v_aug[hh, :, HEAD_DIM:] = jnp.ones((seq, HEAD_DIM), BF16)

    items = []
    for t in range(n_blocks):
        for hh in range(HEADS_PER_STEP_B):
            items.append((hh, t if hh % 2 == 0 else n_blocks - 1 - t))

    def scores(hh, qi):
        q0, q1 = qi * TQ_B, (qi + 1) * TQ_B
        q = q_ref[hh, q0:q1, :]
        s_diag = lax.dot_general(q, k_ref[hh, q0:q1, :], nt, preferred_element_type=F32)
        s_past = (lax.dot_general(q, k_ref[hh, 0:q0, :], nt, preferred_element_type=F32)
                  if qi > 0 else None)
        return s_diag, s_past

    def softmax(hh, qi, s_d, s_p):
        q0, q1 = qi * TQ_B, (qi + 1) * TQ_B
        h = group * HEADS_PER_STEP_B + hh
        fq = jnp.sum(jnp.where(lane == h, fq_ref[q0:q1, :], 0.0), axis=1, keepdims=True)
        s_d = (s_d * (SCALE * LOG2E) + fq) - fk_ref[pl.ds(h, 1), q0:q1]
        s_d = jnp.where(row >= col, s_d, NEG_INF)
        mx = jnp.max(s_d, axis=-1, keepdims=True)
        if qi > 0:
            s_p = (s_p * (SCALE * LOG2E) + fq) - fk_ref[pl.ds(h, 1), 0:q0]
            mx = jnp.maximum(mx, jnp.max(s_p, axis=-1, keepdims=True))
        e_d = jnp.exp2(s_d - mx).astype(BF16)
        e_p = jnp.exp2(s_p - mx).astype(BF16) if qi > 0 else None
        return e_d, e_p

    def finish(hh, qi, e_d, e_p):
        q0, q1 = qi * TQ_B, (qi + 1) * TQ_B
        acc = _dot(e_d, v_aug[hh, q0:q1, :])
        if e_p is not None:
            acc = acc + _dot(e_p, v_aug[hh, 0:q0, :])
        o_ref[q0:q1, lanes(hh)] = (acc[:, :HEAD_DIM] / acc[:, HEAD_DIM:]).astype(o_ref.dtype)

    s_queue = [scores(*items[t]) for t in range(QK_AHEAD)]
    pending = None
    for t, (hh, qi) in enumerate(items):
        s_d, s_p = s_queue.pop(0)
        if t + QK_AHEAD < len(items):
            s_queue.append(scores(*items[t + QK_AHEAD]))
        e_d, e_p = softmax(hh, qi, s_d, s_p)
        if pending is not None:
            finish(*pending)
        pending = (hh, qi, e_d, e_p)
    finish(*pending)

    w_o_ref[...] = w_ref[...].astype(BF16)


def _fox(z, f_col, f_row, w_cast):
    b, _, s, _ = z.shape
    hp = HEADS_PER_STEP_B
    n_groups = N_HEADS_B // hp
    width = hp * HEAD_DIM
    base = 3 * N_HEADS_A // hp
    wk, wn = w_cast.shape
    slab = wk // (b * n_groups)
    return pl.pallas_call(
        _fox_kernel,
        grid=(b, n_groups),
        in_specs=[pl.BlockSpec((None, hp, s, HEAD_DIM), lambda bi, h: (bi, base + h, 0, 0)),
                  pl.BlockSpec((None, hp, s, HEAD_DIM), lambda bi, h: (bi, base + n_groups + h, 0, 0)),
                  pl.BlockSpec((None, hp, s, HEAD_DIM), lambda bi, h: (bi, base + 2 * n_groups + h, 0, 0)),
                  pl.BlockSpec((None, s, LANES), lambda bi, h: (bi, 0, 0)),
                  pl.BlockSpec((None, N_HEADS_B, s), lambda bi, h: (bi, 0, 0)),
                  pl.BlockSpec((slab, wn), lambda bi, h: (bi * n_groups + h, 0))],
        out_specs=[pl.BlockSpec((None, s, width), lambda bi, h: (bi, 0, h)),
                   pl.BlockSpec((slab, wn), lambda bi, h: (bi * n_groups + h, 0))],
        out_shape=[jax.ShapeDtypeStruct((b, s, D_B), BF16),
                   jax.ShapeDtypeStruct((wk, wn), BF16)],
        scratch_shapes=[pltpu.VMEM((hp, s, 2 * HEAD_DIM), BF16)],
        compiler_params=_cparams(2),
        name="attn_fox",
    )(z, z, z, f_col, f_row, w_cast)


def _rel_bias_rows(rel_bias):
    far = rel_bias[:, 2 * REL_CLIP:]
    n_head = PAD_A - REL_CLIP
    n_tail = REL_ROW - n_head - (2 * REL_CLIP + 1)
    rows = jnp.concatenate([jnp.tile(far, (1, n_head)), jnp.flip(rel_bias, axis=1),
                            jnp.tile(far, (1, n_tail))], axis=1)
    return rows[:, None, :]


def kernel(x, p, g_ffn1, w_ffn1_gu, w_ffn1_down, g_mix, w_in, b_forget, rel_bias, w_out,
           g_ffn2, w_ffn2_gu, w_ffn2_down, g_ple, w_ple_gate, w_ple_proj, g_final):
    b, s, d = x.shape
    m = b * s
    depth = p.shape[0]
    h = x.reshape(m, d)
    for i in range(depth):
        hid, w_down1 = _gateup(h, g_ffn1[i], w_ffn1_gu[i], w_ffn1_down[i], GATEUP_TN_FUSED)
        h, u, w_down2 = _down(hid, w_down1, h, g_mix[i], w_ffn2_down[i])
        w_in_t = jnp.swapaxes(w_in[i], 0, 1)
        z = _proj(u, w_in_t, D_QKV, b)
        b_f = jnp.pad(b_forget[i], (0, LANES - N_HEADS_B)).reshape(1, LANES)
        f_col, f_row = _fox_prep(u.reshape(b, s, d), w_in_t, b_f)
        o_a, w_out_bf = _attn_a(z, _rel_bias_rows(rel_bias[i]), w_out[i])
        o_b, w_gate_bf = _fox(z, f_col, f_row, w_ple_gate[i])
        h, xn = _outproj(o_a.reshape(m, D_A), o_b.reshape(m, D_B), w_out_bf, h, g_ffn2[i])
        hid, = _gateup(xn, None, w_ffn2_gu[i], None, GATEUP_TN)
        h = _down_ple(hid, w_down2, h, g_ple[i], p[i].reshape(m, D_PLE), w_gate_bf, w_ple_proj[i],
                      g_final, i == depth - 1)
    return h.reshape(b, s, d)
```

```python
import functools

import jax
import jax.numpy as jnp
from jax import lax
from jax.experimental import pallas as pl
from jax.experimental.pallas import tpu as pltpu

F32 = jnp.float32
BF16 = jnp.bfloat16

D_MODEL = 2048
CHUNK = 64
N_LEFT_CHUNKS = 8
HEAD_DIM = 128
N_HEADS_A = 8
N_HEADS_B = 8
D_A = N_HEADS_A * HEAD_DIM
D_B = N_HEADS_B * HEAD_DIM
REL_CLIP = 256
D_FF = 5632
D_PLE = 256
EPS = 1e-6
NEG_INF = -1e30
D_QKV = 3 * D_A + 3 * D_B
SCALE = HEAD_DIM ** -0.5
LOG2E = 1.4426950408889634

LANES = 128
VMEM_LIMIT = 56 * 1024 * 1024

QB_A = 2 * CHUNK
BAND_A = (N_LEFT_CHUNKS + 2) * CHUNK
PAD_A = N_LEFT_CHUNKS * CHUNK
REL_ROW = 1024

GATEUP_ROWS = 1024
GATEUP_TN = 256
GATEUP_TN_FUSED = 128
RESID_SLOTS = 3
NORM_ROWS = 512
TQ_B = 256
HEADS_PER_STEP_A = 2
HEADS_PER_STEP_B = 4
QK_AHEAD = 2


def _cparams(n_axes):
    return pltpu.CompilerParams(
        dimension_semantics=("arbitrary",) * n_axes,
        vmem_limit_bytes=VMEM_LIMIT,
    )


def _rms(x, g):
    ms = jnp.mean(x * x, axis=-1, keepdims=True)
    return (x * lax.rsqrt(ms + EPS)) * g


def _dot(a, b):
    return jnp.dot(a, b, preferred_element_type=F32)


def _gateup_kernel(n_norm, has_rider, *refs):
    refs = list(refs)
    x_ref, g_ref = (refs.pop(0), refs.pop(0)) if n_norm else (None, None)
    a_ref = None if n_norm else refs.pop(0)
    wg_ref, wu_ref = refs.pop(0), refs.pop(0)
    wc_ref = refs.pop(0) if has_rider else None
    o_ref = refs.pop(0)
    wc_o_ref = refs.pop(0) if has_rider else None
    w_s = refs.pop(0)
    if n_norm:
        a_ref = refs.pop(0)
    m, tn = o_ref.shape

    def matmul_phase():
        w_s[:, :tn] = wg_ref[...].astype(BF16)
        w_s[:, tn:] = wu_ref[...].astype(BF16)

        if has_rider:
            wc_o_ref[...] = wc_ref[...].astype(BF16)

        w = w_s[...]
        for c in range(m // GATEUP_ROWS):
            rows = slice(c * GATEUP_ROWS, (c + 1) * GATEUP_ROWS)
            r = _dot(a_ref[rows, :], w)
            g = r[:, :tn]
            u = r[:, tn:]
            o_ref[rows, :] = ((g * jax.nn.sigmoid(g)) * u).astype(o_ref.dtype)

    if n_norm:
        step = pl.program_id(0)

        @pl.when(step < n_norm)
        def _():
            r0 = pl.multiple_of(step * NORM_ROWS, NORM_ROWS)
            a_ref[pl.ds(r0, NORM_ROWS), :] = _rms(x_ref[...], g_ref[...]).astype(BF16)

        pl.when(step >= n_norm)(matmul_phase)
    else:
        matmul_phase()


def _gateup(a, g_norm, w_gu, w_cast, tn):
    m, k = a.shape
    nj = D_FF // tn
    n_norm = 0 if g_norm is None else m // NORM_ROWS
    has_rider = w_cast is not None

    def col(s):
        return jnp.maximum(s - n_norm, 0)

    operands, in_specs = [], []
    scratch = [pltpu.VMEM((k, 2 * tn), BF16)]
    if n_norm:
        operands += [a, g_norm.reshape(1, k)]
        in_specs += [pl.BlockSpec((NORM_ROWS, k), lambda s: (jnp.minimum(s, n_norm - 1), 0)),
                     pl.BlockSpec((1, k), lambda s: (0, 0))]
        scratch.append(pltpu.VMEM((m, k), BF16))
    else:
        operands.append(a)
        in_specs.append(pl.BlockSpec((m, k), lambda s: (0, 0), pipeline_mode=pl.Buffered(1)))
    operands += [w_gu, w_gu]
    in_specs += [pl.BlockSpec((k, tn), lambda s: (0, col(s))),
                 pl.BlockSpec((k, tn), lambda s: (0, col(s) + nj))]
    out_specs = [pl.BlockSpec((m, tn), lambda s: (0, col(s)))]
    out_shape = [jax.ShapeDtypeStruct((m, D_FF), BF16)]
    if has_rider:
        wk, wn = w_cast.shape
        slab = wk // nj
        operands.append(w_cast)
        in_specs.append(pl.BlockSpec((slab, wn), lambda s: (col(s), 0)))
        out_specs.append(pl.BlockSpec((slab, wn), lambda s: (col(s), 0)))
        out_shape.append(jax.ShapeDtypeStruct((wk, wn), BF16))
    return pl.pallas_call(
        functools.partial(_gateup_kernel, n_norm, has_rider),
        grid=(n_norm + nj,),
        in_specs=in_specs,
        out_specs=out_specs,
        out_shape=out_shape,
        scratch_shapes=scratch,
        compiler_params=_cparams(1),
        name="ffn_gateup",
    )(*operands)


def _proj_kernel(a_ref, wt_ref, o_ref, w_s):
    @pl.when(pl.program_id(1) == 0)
    def _():
        w_s[...] = wt_ref[...].astype(BF16)

    r = lax.dot_general(a_ref[...], w_s[...], (((1,), (1,)), ((), ())), preferred_element_type=F32)
    for hh in range(o_ref.shape[0]):
        o_ref[hh] = r[:, hh * HEAD_DIM:(hh + 1) * HEAD_DIM].astype(o_ref.dtype)


def _proj(a, w_t, n_out, batch, tn=1024):
    m, k = a.shape
    seq = m // batch
    heads_per_tile = tn // HEAD_DIM
    return pl.pallas_call(
        _proj_kernel,
        grid=(n_out // tn, batch),
        in_specs=[pl.BlockSpec((seq, k), lambda j, i: (i, 0)),
                  pl.BlockSpec((tn, k), lambda j, i: (j, 0))],
        out_specs=pl.BlockSpec((None, heads_per_tile, seq, HEAD_DIM), lambda j, i: (i, j, 0, 0)),
        out_shape=jax.ShapeDtypeStruct((batch, n_out // HEAD_DIM, seq, HEAD_DIM), BF16),
        scratch_shapes=[pltpu.VMEM((tn, k), BF16)],
        compiler_params=_cparams(2),
        name="in_proj",
    )(a, w_t)


def _down_kernel(a_ref, w_ref, r_ref, g_ref, wc_ref, h_ref, n_ref, wc_o_ref):
    h = r_ref[...] + 0.5 * _dot(a_ref[...], w_ref[...])
    h_ref[...] = h
    n_ref[...] = _rms(h, g_ref[...]).astype(n_ref.dtype)
    wc_o_ref[...] = wc_ref[...].astype(BF16)


def _down(a, w_bf16, resid, g_next, w_cast, tm=256):
    m, k = a.shape
    n = w_bf16.shape[1]
    steps = m // tm
    wk, wn = w_cast.shape
    slab = wk // steps
    return pl.pallas_call(
        _down_kernel,
        grid=(steps,),
        in_specs=[pl.BlockSpec((tm, k), lambda i: (i, 0)),
                  pl.BlockSpec((k, n), lambda i: (0, 0), pipeline_mode=pl.Buffered(1)),
                  pl.BlockSpec((tm, n), lambda i: (i, 0)),
                  pl.BlockSpec((1, n), lambda i: (0, 0)),
                  pl.BlockSpec((slab, wn), lambda i: (i, 0))],
        out_specs=[pl.BlockSpec((tm, n), lambda i: (i, 0)),
                   pl.BlockSpec((tm, n), lambda i: (i, 0)),
                   pl.BlockSpec((slab, wn), lambda i: (i, 0))],
        out_shape=[jax.ShapeDtypeStruct((m, n), F32), jax.ShapeDtypeStruct((m, n), BF16),
                   jax.ShapeDtypeStruct((wk, wn), BF16)],
        compiler_params=_cparams(1),
        name="ffn_down",
    )(a, w_bf16, resid, g_next.reshape(1, n), w_cast)


def _outproj_kernel(steps, oa_ref, ob_ref, w_ref, r_hbm, g_ref, h_ref, n_ref, r_buf, r_sem):
    i = pl.program_id(0)
    tm = h_ref.shape[0]

    def resid_copy(step):
        slot = step % RESID_SLOTS
        row0 = pl.multiple_of(step * tm, tm)
        return pltpu.make_async_copy(r_hbm.at[pl.ds(row0, tm), :], r_buf.at[slot], r_sem.at[slot])

    @pl.when(i == 0)
    def _():
        for s in range(RESID_SLOTS - 1):
            resid_copy(s).start()

    @pl.when(i + RESID_SLOTS - 1 < steps)
    def _():
        resid_copy(i + RESID_SLOTS - 1).start()

    resid_copy(i).wait()
    acc = _dot(oa_ref[...], w_ref[:D_A, :]) + _dot(ob_ref[...], w_ref[D_A:, :])
    h = r_buf[i % RESID_SLOTS] + acc
    h_ref[...] = h
    n_ref[...] = _rms(h, g_ref[...]).astype(n_ref.dtype)


def _outproj(o_a, o_b, w_out, resid, g_next, tm=512):
    m = o_a.shape[0]
    k, n = w_out.shape
    steps = m // tm
    assert steps >= RESID_SLOTS
    return pl.pallas_call(
        functools.partial(_outproj_kernel, steps),
        grid=(steps,),
        in_specs=[pl.BlockSpec((tm, D_A), lambda i: (i, 0)),
                  pl.BlockSpec((tm, D_B), lambda i: (i, 0)),
                  pl.BlockSpec((k, n), lambda i: (0, 0), pipeline_mode=pl.Buffered(1)),
                  pl.BlockSpec(memory_space=pl.ANY),
                  pl.BlockSpec((1, n), lambda i: (0, 0))],
        out_specs=[pl.BlockSpec((tm, n), lambda i: (i, 0)),
                   pl.BlockSpec((tm, n), lambda i: (i, 0))],
        out_shape=[jax.ShapeDtypeStruct((m, n), F32), jax.ShapeDtypeStruct((m, n), BF16)],
        scratch_shapes=[pltpu.VMEM((RESID_SLOTS, tm, n), F32), pltpu.SemaphoreType.DMA((RESID_SLOTS,))],
        compiler_params=_cparams(1),
        name="out_proj",
    )(o_a, o_b, w_out, resid, g_next.reshape(1, n))


def _down_ple_kernel(close_block, a_ref, w_ref, r_ref, gp_ref, p_ref, wg_ref, wp_ref, gf_ref, o_ref, wp_s):
    @pl.when(pl.program_id(0) == 0)
    def _():
        wp_s[...] = wp_ref[...].astype(BF16)

    h = r_ref[...] + 0.5 * _dot(a_ref[...], w_ref[...])
    hn = _rms(h, gp_ref[...]).astype(BF16)
    gate = jax.nn.sigmoid(_dot(hn, wg_ref[...]))
    emb = _dot(p_ref[...].astype(BF16), wp_s[...])
    h = h + gate * emb
    o_ref[...] = _rms(h, gf_ref[...]) if close_block else h


def _down_ple(a, w_down, resid, g_ple, p, w_gate, w_proj, g_final, close_block, tm=256):
    m, k = a.shape
    n = w_down.shape[1]
    const = dict(pipeline_mode=pl.Buffered(1))
    return pl.pallas_call(
        functools.partial(_down_ple_kernel, close_block),
        grid=(m // tm,),
        in_specs=[pl.BlockSpec((tm, k), lambda i: (i, 0)),
                  pl.BlockSpec((k, n), lambda i: (0, 0), **const),
                  pl.BlockSpec((tm, n), lambda i: (i, 0)),
                  pl.BlockSpec((1, n), lambda i: (0, 0)),
                  pl.BlockSpec((tm, D_PLE), lambda i: (i, 0)),
                  pl.BlockSpec((n, n), lambda i: (0, 0), **const),
                  pl.BlockSpec((D_PLE, n), lambda i: (0, 0), **const),
                  pl.BlockSpec((1, n), lambda i: (0, 0))],
        out_specs=pl.BlockSpec((tm, n), lambda i: (i, 0)),
        out_shape=jax.ShapeDtypeStruct((m, n), F32),
        scratch_shapes=[pltpu.VMEM((D_PLE, n), BF16)],
        compiler_params=_cparams(1),
        name="ffn_down_ple",
    )(a, w_down, resid, g_ple.reshape(1, n), p, w_gate, w_proj, g_final.reshape(1, n))


def _attn_a_scores(q, k):
    return lax.dot_general(q, k, (((1,), (1,)), ((), ())), preferred_element_type=F32)


def _attn_a_softmax(s, bias, key_lo):
    nk = s.shape[1]
    s = s * (SCALE * LOG2E) + bias
    row = lax.broadcasted_iota(jnp.int32, (QB_A, nk), 0)
    col = lax.broadcasted_iota(jnp.int32, (QB_A, nk), 1) + key_lo
    first = row < CHUNK
    valid = (first & (col < BAND_A - CHUNK)) | (jnp.logical_not(first) & (col >= CHUNK))
    s = jnp.where(valid, s, NEG_INF)
    mx = jnp.max(s, axis=-1, keepdims=True)
    e = jnp.exp2(s - mx)
    den = jnp.sum(e, axis=-1, keepdims=True)
    return e.astype(BF16), den


def _attn_a_kernel(g_ref, q_ref, k_ref, v_ref, w_ref, o_ref, w_o_ref, bias_s):
    @pl.when(pl.program_id(1) == 0)
    def _():
        for hh in range(HEADS_PER_STEP_A):
            g = jnp.broadcast_to(g_ref[hh] * LOG2E, (QB_A, REL_ROW))
            bias_s[hh] = pltpu.roll(g, 0, axis=1, stride=1, stride_axis=0)[:, :BAND_A]

    n_blocks = q_ref.shape[1] // QB_A
    items = [(m, hh) for m in range(n_blocks) for hh in range(HEADS_PER_STEP_A)]

    def window(m):
        k0 = max(m * QB_A - PAD_A, 0)
        return k0, (m + 1) * QB_A

    def lanes(hh):
        return slice(hh * HEAD_DIM, (hh + 1) * HEAD_DIM)

    def scores(m, hh):
        k0, k1 = window(m)
        return _attn_a_scores(q_ref[hh, m * QB_A:k1, :], k_ref[hh, k0:k1, :])

    def finish(m, hh, e, den):
        k0, k1 = window(m)
        o = _dot(e, v_ref[hh, k0:k1, :])
        o_ref[m * QB_A:k1, lanes(hh)] = (o / den).astype(o_ref.dtype)

    s_queue = [scores(*items[t]) for t in range(QK_AHEAD)]
    pending = None
    for t, (m, hh) in enumerate(items):
        s = s_queue.pop(0)
        if t + QK_AHEAD < len(items):
            s_queue.append(scores(*items[t + QK_AHEAD]))
        k0, k1 = window(m)
        lo = BAND_A - (k1 - k0)
        e, den = _attn_a_softmax(s, bias_s[hh, :, lo:], lo)
        if pending is not None:
            finish(*pending)
        pending = (m, hh, e, den)
    finish(*pending)

    w_o_ref[...] = w_ref[...].astype(BF16)


def _attn_a(z, bias_rows, w_cast):
    b, _, s, _ = z.shape
    hp = HEADS_PER_STEP_A
    n_groups = N_HEADS_A // hp
    width = hp * HEAD_DIM
    wk, wn = w_cast.shape
    slab = wk // (n_groups * b)
    return pl.pallas_call(
        _attn_a_kernel,
        grid=(n_groups, b),
        in_specs=[pl.BlockSpec((hp, 1, REL_ROW), lambda h, bi: (h, 0, 0)),
                  pl.BlockSpec((None, hp, s, HEAD_DIM), lambda h, bi: (bi, h, 0, 0)),
                  pl.BlockSpec((None, hp, s, HEAD_DIM), lambda h, bi: (bi, n_groups + h, 0, 0)),
                  pl.BlockSpec((None, hp, s, HEAD_DIM), lambda h, bi: (bi, 2 * n_groups + h, 0, 0)),
                  pl.BlockSpec((slab, wn), lambda h, bi: (h * b + bi, 0))],
        out_specs=[pl.BlockSpec((None, s, width), lambda h, bi: (bi, 0, h)),
                   pl.BlockSpec((slab, wn), lambda h, bi: (h * b + bi, 0))],
        out_shape=[jax.ShapeDtypeStruct((b, s, D_A), BF16),
                   jax.ShapeDtypeStruct((wk, wn), BF16)],
        scratch_shapes=[pltpu.VMEM((hp, QB_A, BAND_A), F32)],
        compiler_params=_cparams(2),
        name="attn_band",
    )(bias_rows, z, z, z, w_cast)


def _fox_prep_kernel(u_ref, wft_ref, bf_ref, f_ref, ft_ref):
    seq, d = u_ref.shape
    wft = jnp.concatenate([wft_ref[...], jnp.zeros((LANES - N_HEADS_B, d), F32)], axis=0).astype(BF16)
    fl = lax.dot_general(u_ref[...], wft, (((1,), (1,)), ((), ())), preferred_element_type=F32)
    fl = fl + bf_ref[...]
    x = jnp.minimum(fl, 0.0) - jnp.log1p(jnp.exp(-jnp.abs(fl)))
    row = lax.broadcasted_iota(jnp.int32, x.shape, 0)
    shift = 1
    while shift < seq:
        x = x + jnp.where(row >= shift, pltpu.roll(x, shift, axis=0), 0.0)
        shift *= 2
    x = x * LOG2E
    f_ref[...] = x
    ft_ref[...] = jnp.transpose(x)[:N_HEADS_B, :]


def _fox_prep(u3, w_in_t, b_f):
    b, s, d = u3.shape
    return pl.pallas_call(
        _fox_prep_kernel,
        grid=(b,),
        in_specs=[pl.BlockSpec((None, s, d), lambda bi: (bi, 0, 0)),
                  pl.BlockSpec((N_HEADS_B, d), lambda bi: (D_QKV // N_HEADS_B, 0)),
                  pl.BlockSpec((1, LANES), lambda bi: (0, 0))],
        out_specs=[pl.BlockSpec((None, s, LANES), lambda bi: (bi, 0, 0)),
                   pl.BlockSpec((None, N_HEADS_B, s), lambda bi: (bi, 0, 0))],
        out_shape=[jax.ShapeDtypeStruct((b, s, LANES), F32),
                   jax.ShapeDtypeStruct((b, N_HEADS_B, s), F32)],
        compiler_params=_cparams(1),
        name="fox_prep",
    )(u3, w_in_t, b_f)


def _fox_kernel(q_ref, k_ref, v_ref, fq_ref, fk_ref, w_ref, o_ref, w_o_ref, v_aug):
    seq = q_ref.shape[1]
    n_blocks = seq // TQ_B
    group = pl.program_id(1)
    lane = lax.broadcasted_iota(jnp.int32, (TQ_B, LANES), 1)
    row = lax.broadcasted_iota(jnp.int32, (TQ_B, TQ_B), 0)
    col = lax.broadcasted_iota(jnp.int32, (TQ_B, TQ_B), 1)
    nt = (((1,), (1,)), ((), ()))

    def lanes(hh):
        return slice(hh * HEAD_DIM, (hh + 1) * HEAD_DIM)

    for hh in range(HEADS_PER_STEP_B):
        v_aug[hh, :, :HEAD_DIM] = v_ref[hh]
        v_aug[hh, :, HEAD_DIM:] = jnp.ones((seq, HEAD_DIM), BF16)

    items = []
    for t in range(n_blocks):
        for hh in range(HEADS_PER_STEP_B):
            items.append((hh, t if hh % 2 == 0 else n_blocks - 1 - t))

    def scores(hh, qi):
        q0, q1 = qi * TQ_B, (qi + 1) * TQ_B
        q = q_ref[hh, q0:q1, :]
        s_diag = lax.dot_general(q, k_ref[hh, q0:q1, :], nt, preferred_element_type=F32)
        s_past = (lax.dot_general(q, k_ref[hh, 0:q0, :], nt, preferred_element_type=F32)
                  if qi > 0 else None)
        return s_diag, s_past

    def softmax(hh, qi, s_d, s_p):
        q0, q1 = qi * TQ_B, (qi + 1) * TQ_B
        h = group * HEADS_PER_STEP_B + hh
        fq = jnp.sum(jnp.where(lane == h, fq_ref[q0:q1, :], 0.0), axis=1, keepdims=True)
        s_d = (s_d * (SCALE * LOG2E) + fq) - fk_ref[pl.ds(h, 1), q0:q1]
        s_d = jnp.where(row >= col, s_d, NEG_INF)
        mx = jnp.max(s_d, axis=-1, keepdims=True)
        if qi > 0:
            s_p = (s_p * (SCALE * LOG2E) + fq) - fk_ref[pl.ds(h, 1), 0:q0]
            mx = jnp.maximum(mx, jnp.max(s_p, axis=-1, keepdims=True))
        e_d = jnp.exp2(s_d - mx).astype(BF16)
        e_p = jnp.exp2(s_p - mx).astype(BF16) if qi > 0 else None
        return e_d, e_p

    def finish(hh, qi, e_d, e_p):
        q0, q1 = qi * TQ_B, (qi + 1) * TQ_B
        acc = _dot(e_d, v_aug[hh, q0:q1, :])
        if e_p is not None:
            acc = acc + _dot(e_p, v_aug[hh, 0:q0, :])
        o_ref[q0:q1, lanes(hh)] = (acc[:, :HEAD_DIM] / acc[:, HEAD_DIM:]).astype(o_ref.dtype)

    s_queue = [scores(*items[t]) for t in range(QK_AHEAD)]
    pending = None
    for t, (hh, qi) in enumerate(items):
        s_d, s_p = s_queue.pop(0)
        if t + QK_AHEAD < len(items):
            s_queue.append(scores(*items[t + QK_AHEAD]))
        e_d, e_p = softmax(hh, qi, s_d, s_p)
        if pending is not None:
            finish(*pending)
        pending = (hh, qi, e_d, e_p)
    finish(*pending)

    w_o_ref[...] = w_ref[...].astype(BF16)


def _fox(z, f_col, f_row, w_cast):
    b, _, s, _ = z.shape
    hp = HEADS_PER_STEP_B
    n_groups = N_HEADS_B // hp
    width = hp * HEAD_DIM
    base = 3 * N_HEADS_A // hp
    wk, wn = w_cast.shape
    slab = wk // (b * n_groups)
    return pl.pallas_call(
        _fox_kernel,
        grid=(b, n_groups),
        in_specs=[pl.BlockSpec((None, hp, s, HEAD_DIM), lambda bi, h: (bi, base + h, 0, 0)),
                  pl.BlockSpec((None, hp, s, HEAD_DIM), lambda bi, h: (bi, base + n_groups + h, 0, 0)),
                  pl.BlockSpec((None, hp, s, HEAD_DIM), lambda bi, h: (bi, base + 2 * n_groups + h, 0, 0)),
                  pl.BlockSpec((None, s, LANES), lambda bi, h: (bi, 0, 0)),
                  pl.BlockSpec((None, N_HEADS_B, s), lambda bi, h: (bi, 0, 0)),
                  pl.BlockSpec((slab, wn), lambda bi, h: (bi * n_groups + h, 0))],
        out_specs=[pl.BlockSpec((None, s, width), lambda bi, h: (bi, 0, h)),
                   pl.BlockSpec((slab, wn), lambda bi, h: (bi * n_groups + h, 0))],
        out_shape=[jax.ShapeDtypeStruct((b, s, D_B), BF16),
                   jax.ShapeDtypeStruct((wk, wn), BF16)],
        scratch_shapes=[pltpu.VMEM((hp, s, 2 * HEAD_DIM), BF16)],
        compiler_params=_cparams(2),
        name="attn_fox",
    )(z, z, z, f_col, f_row, w_cast)


def _rel_bias_rows(rel_bias):
    far = rel_bias[:, 2 * REL_CLIP:]
    n_head = PAD_A - REL_CLIP
    n_tail = REL_ROW - n_head - (2 * REL_CLIP + 1)
    rows = jnp.concatenate([jnp.tile(far, (1, n_head)), jnp.flip(rel_bias, axis=1),
                            jnp.tile(far, (1, n_tail))], axis=1)
    return rows[:, None, :]


def kernel(x, p, g_ffn1, w_ffn1_gu, w_ffn1_down, g_mix, w_in, b_forget, rel_bias, w_out,
           g_ffn2, w_ffn2_gu, w_ffn2_down, g_ple, w_ple_gate, w_ple_proj, g_final):
    b, s, d = x.shape
    m = b * s
    depth = p.shape[0]
    h = x.reshape(m, d)
    for i in range(depth):
        hid, w_down1 = _gateup(h, g_ffn1[i], w_ffn1_gu[i], w_ffn1_down[i], GATEUP_TN_FUSED)
        h, u, w_down2 = _down(hid, w_down1, h, g_mix[i], w_ffn2_down[i])
        w_in_t = jnp.swapaxes(w_in[i], 0, 1)
        z = _proj(u, w_in_t, D_QKV, b)
        b_f = jnp.pad(b_forget[i], (0, LANES - N_HEADS_B)).reshape(1, LANES)
        f_col, f_row = _fox_prep(u.reshape(b, s, d), w_in_t, b_f)
        o_a, w_out_bf = _attn_a(z, _rel_bias_rows(rel_bias[i]), w_out[i])
        o_b, w_gate_bf = _fox(z, f_col, f_row, w_ple_gate[i])
        h, xn = _outproj(o_a.reshape(m, D_A), o_b.reshape(m, D_B), w_out_bf, h, g_ffn2[i])
        hid, = _gateup(xn, None, w_ffn2_gu[i], None, GATEUP_TN)
        h = _down_ple(hid, w_down2, h, g_ple[i], p[i].reshape(m, D_PLE), w_gate_bf, w_ple_proj[i],
                      g_final, i == depth - 1)
    return h.reshape(b, s, d)
```

```python
import functools

import jax
import jax.numpy as jnp
from jax import lax
from jax.experimental import pallas as pl
from jax.experimental.pallas import tpu as pltpu

F32 = jnp.float32
BF16 = jnp.bfloat16

D_MODEL = 2048
CHUNK = 64
N_LEFT_CHUNKS = 8
HEAD_DIM = 128
N_HEADS_A = 8
N_HEADS_B = 8
D_A = N_HEADS_A * HEAD_DIM
D_B = N_HEADS_B * HEAD_DIM
REL_CLIP = 256
D_FF = 5632
D_PLE = 256
EPS = 1e-6
NEG_INF = -1e30
D_QKV = 3 * D_A + 3 * D_B
SCALE = HEAD_DIM ** -0.5
LOG2E = 1.4426950408889634

LANES = 128
VMEM_LIMIT = 56 * 1024 * 1024

QB_A = 2 * CHUNK
BAND_A = (N_LEFT_CHUNKS + 2) * CHUNK
PAD_A = N_LEFT_CHUNKS * CHUNK
REL_ROW = 1024

GATEUP_ROWS = 1024
GATEUP_TN = 256
GATEUP_TN_FUSED = 128
NORM_ROWS = 512
TQ_B = 256
HEADS_PER_STEP_A = 2
HEADS_PER_STEP_B = 4
QK_AHEAD = 2


def _cparams(n_axes):
    return pltpu.CompilerParams(
        dimension_semantics=("arbitrary",) * n_axes,
        vmem_limit_bytes=VMEM_LIMIT,
    )


def _rms(x, g):
    ms = jnp.mean(x * x, axis=-1, keepdims=True)
    return (x * lax.rsqrt(ms + EPS)) * g


def _dot(a, b):
    return jnp.dot(a, b, preferred_element_type=F32)


def _gateup_kernel(n_norm, has_rider, *refs):
    refs = list(refs)
    x_ref, g_ref = (refs.pop(0), refs.pop(0)) if n_norm else (None, None)
    a_ref = None if n_norm else refs.pop(0)
    wg_ref, wu_ref = refs.pop(0), refs.pop(0)
    wc_ref = refs.pop(0) if has_rider else None
    o_ref = refs.pop(0)
    wc_o_ref = refs.pop(0) if has_rider else None
    w_s = refs.pop(0)
    if n_norm:
        a_ref = refs.pop(0)
    m, tn = o_ref.shape

    def matmul_phase():
        w_s[:, :tn] = wg_ref[...].astype(BF16)
        w_s[:, tn:] = wu_ref[...].astype(BF16)

        if has_rider:
            wc_o_ref[...] = wc_ref[...].astype(BF16)

        w = w_s[...]
        for c in range(m // GATEUP_ROWS):
            rows = slice(c * GATEUP_ROWS, (c + 1) * GATEUP_ROWS)
            r = _dot(a_ref[rows, :], w)
            g = r[:, :tn]
            u = r[:, tn:]
            o_ref[rows, :] = ((g * jax.nn.sigmoid(g)) * u).astype(o_ref.dtype)

    if n_norm:
        step = pl.program_id(0)

        @pl.when(step < n_norm)
        def _():
            r0 = pl.multiple_of(step * NORM_ROWS, NORM_ROWS)
            a_ref[pl.ds(r0, NORM_ROWS), :] = _rms(x_ref[...], g_ref[...]).astype(BF16)

        pl.when(step >= n_norm)(matmul_phase)
    else:
        matmul_phase()


def _gateup(a, g_norm, w_gu, w_cast, tn):
    m, k = a.shape
    nj = D_FF // tn
    n_norm = 0 if g_norm is None else m // NORM_ROWS
    has_rider = w_cast is not None

    def col(s):
        return jnp.maximum(s - n_norm, 0)

    operands, in_specs = [], []
    scratch = [pltpu.VMEM((k, 2 * tn), BF16)]
    if n_norm:
        operands += [a, g_norm.reshape(1, k)]
        in_specs += [pl.BlockSpec((NORM_ROWS, k), lambda s: (jnp.minimum(s, n_norm - 1), 0)),
                     pl.BlockSpec((1, k), lambda s: (0, 0))]
        scratch.append(pltpu.VMEM((m, k), BF16))
    else:
        operands.append(a)
        in_specs.append(pl.BlockSpec((m, k), lambda s: (0, 0), pipeline_mode=pl.Buffered(1)))
    operands += [w_gu, w_gu]
    in_specs += [pl.BlockSpec((k, tn), lambda s: (0, col(s))),
                 pl.BlockSpec((k, tn), lambda s: (0, col(s) + nj))]
    out_specs = [pl.BlockSpec((None, m, tn), lambda s: (col(s), 0, 0))]
    out_shape = [jax.ShapeDtypeStruct((nj, m, tn), BF16)]
    if has_rider:
        wk, wn = w_cast.shape
        slab = wk // nj
        operands.append(w_cast)
        in_specs.append(pl.BlockSpec((slab, wn), lambda s: (col(s), 0)))
        out_specs.append(pl.BlockSpec((slab, wn), lambda s: (col(s), 0)))
        out_shape.append(jax.ShapeDtypeStruct((wk, wn), BF16))
    return pl.pallas_call(
        functools.partial(_gateup_kernel, n_norm, has_rider),
        grid=(n_norm + nj,),
        in_specs=in_specs,
        out_specs=out_specs,
        out_shape=out_shape,
        scratch_shapes=scratch,
        compiler_params=_cparams(1),
        name="ffn_gateup",
    )(*operands)


def _proj_kernel(a_ref, wt_ref, o_ref, w_s):
    @pl.when(pl.program_id(1) == 0)
    def _():
        w_s[...] = wt_ref[...].astype(BF16)

    r = lax.dot_general(a_ref[...], w_s[...], (((1,), (1,)), ((), ())), preferred_element_type=F32)
    for hh in range(o_ref.shape[0]):
        o_ref[hh] = r[:, hh * HEAD_DIM:(hh + 1) * HEAD_DIM].astype(o_ref.dtype)


def _proj(a, w_t, n_out, batch, tn=1024):
    m, k = a.shape
    seq = m // batch
    heads_per_tile = tn // HEAD_DIM
    return pl.pallas_call(
        _proj_kernel,
        grid=(n_out // tn, batch),
        in_specs=[pl.BlockSpec((seq, k), lambda j, i: (i, 0)),
                  pl.BlockSpec((tn, k), lambda j, i: (j, 0))],
        out_specs=pl.BlockSpec((None, heads_per_tile, seq, HEAD_DIM), lambda j, i: (i, j, 0, 0)),
        out_shape=jax.ShapeDtypeStruct((batch, n_out // HEAD_DIM, seq, HEAD_DIM), BF16),
        scratch_shapes=[pltpu.VMEM((tn, k), BF16)],
        compiler_params=_cparams(2),
        name="in_proj",
    )(a, w_t)


def _row_tile(a_ref):
    return jnp.concatenate([a_ref[j] for j in range(a_ref.shape[0])], axis=1)


def _down_kernel(a_ref, w_ref, r_ref, g_ref, wc_ref, h_ref, n_ref, wc_o_ref):
    h = r_ref[...] + 0.5 * _dot(_row_tile(a_ref), w_ref[...])
    h_ref[...] = h
    n_ref[...] = _rms(h, g_ref[...]).astype(n_ref.dtype)
    wc_o_ref[...] = wc_ref[...].astype(BF16)


def _down(a, w_bf16, resid, g_next, w_cast, tm=256):
    tiles, m, tn = a.shape
    k, n = w_bf16.shape
    steps = m // tm
    wk, wn = w_cast.shape
    slab = wk // steps
    return pl.pallas_call(
        _down_kernel,
        grid=(steps,),
        in_specs=[pl.BlockSpec((tiles, tm, tn), lambda i: (0, i, 0)),
                  pl.BlockSpec((k, n), lambda i: (0, 0), pipeline_mode=pl.Buffered(1)),
                  pl.BlockSpec((tm, n), lambda i: (i, 0)),
                  pl.BlockSpec((1, n), lambda i: (0, 0)),
                  pl.BlockSpec((slab, wn), lambda i: (i, 0))],
        out_specs=[pl.BlockSpec((tm, n), lambda i: (i, 0)),
                   pl.BlockSpec((tm, n), lambda i: (i, 0)),
                   pl.BlockSpec((slab, wn), lambda i: (i, 0))],
        out_shape=[jax.ShapeDtypeStruct((m, n), F32), jax.ShapeDtypeStruct((m, n), BF16),
                   jax.ShapeDtypeStruct((wk, wn), BF16)],
        compiler_params=_cparams(1),
        name="ffn_down",
    )(a, w_bf16, resid, g_next.reshape(1, n), w_cast)


def _outproj_kernel(oa_ref, ob_ref, w_ref, r_ref, g_ref, h_ref, n_ref):
    acc = _dot(oa_ref[...], w_ref[:D_A, :]) + _dot(ob_ref[...], w_ref[D_A:, :])
    h = r_ref[...] + acc
    h_ref[...] = h
    n_ref[...] = _rms(h, g_ref[...]).astype(n_ref.dtype)


def _outproj(o_a, o_b, w_out, resid, g_next, tm=512):
    m = o_a.shape[0]
    k, n = w_out.shape
    return pl.pallas_call(
        _outproj_kernel,
        grid=(m // tm,),
        in_specs=[pl.BlockSpec((tm, D_A), lambda i: (i, 0)),
                  pl.BlockSpec((tm, D_B), lambda i: (i, 0)),
                  pl.BlockSpec((k, n), lambda i: (0, 0), pipeline_mode=pl.Buffered(1)),
                  pl.BlockSpec((tm, n), lambda i: (i, 0)),
                  pl.BlockSpec((1, n), lambda i: (0, 0))],
        out_specs=[pl.BlockSpec((tm, n), lambda i: (i, 0)),
                   pl.BlockSpec((tm, n), lambda i: (i, 0))],
        out_shape=[jax.ShapeDtypeStruct((m, n), F32), jax.ShapeDtypeStruct((m, n), BF16)],
        compiler_params=_cparams(1),
        name="out_proj",
    )(o_a, o_b, w_out, resid, g_next.reshape(1, n))


def _down_ple_kernel(close_block, a_ref, w_ref, r_ref, gp_ref, p_ref, wg_ref, wp_ref, gf_ref, o_ref, wp_s):
    @pl.when(pl.program_id(0) == 0)
    def _():
        wp_s[...] = wp_ref[...].astype(BF16)

    h = r_ref[...] + 0.5 * _dot(_row_tile(a_ref), w_ref[...])
    hn =_rms(h, gp_ref[...]).astype(BF16)
    gate = jax.nn.sigmoid(_dot(hn, wg_ref[...]))
    emb = _dot(p_ref[...].astype(BF16), wp_s[...])
    h = h + gate * emb
    o_ref[...] = _rms(h, gf_ref[...]) if close_block else h


def _down_ple(a, w_down, resid, g_ple, p, w_gate, w_proj, g_final, close_block, tm=256):
    tiles, m, tn = a.shape
    k, n = w_down.shape
    const = dict(pipeline_mode=pl.Buffered(1))
    return pl.pallas_call(
        functools.partial(_down_ple_kernel, close_block),
        grid=(m // tm,),
        in_specs=[pl.BlockSpec((tiles, tm, tn), lambda i: (0, i, 0)),
                  pl.BlockSpec((k, n), lambda i: (0, 0), **const),
                  pl.BlockSpec((tm, n), lambda i: (i, 0)),
                  pl.BlockSpec((1, n), lambda i: (0, 0)),
                  pl.BlockSpec((tm, D_PLE), lambda i: (i, 0)),
                  pl.BlockSpec((n, n), lambda i: (0, 0), **const),
                  pl.BlockSpec((D_PLE, n), lambda i: (0, 0), **const),
                  pl.BlockSpec((1, n), lambda i: (0, 0))],
        out_specs=pl.BlockSpec((tm, n), lambda i: (i, 0)),
        out_shape=jax.ShapeDtypeStruct((m, n), F32),
        scratch_shapes=[pltpu.VMEM((D_PLE, n), BF16)],
        compiler_params=_cparams(1),
        name="ffn_down_ple",
    )(a, w_down, resid, g_ple.reshape(1, n), p, w_gate, w_proj, g_final.reshape(1, n))


def _attn_a_scores(q, k):
    return lax.dot_general(q, k, (((1,), (1,)), ((), ())), preferred_element_type=F32)


def _attn_a_softmax(s, bias, key_lo):
    nk = s.shape[1]
    s = s * (SCALE * LOG2E) + bias
    row = lax.broadcasted_iota(jnp.int32, (QB_A, nk), 0)
    col = lax.broadcasted_iota(jnp.int32, (QB_A, nk), 1) + key_lo
    first = row < CHUNK
    valid = (first & (col < BAND_A - CHUNK)) | (jnp.logical_not(first) & (col >= CHUNK))
    s = jnp.where(valid, s, NEG_INF)
    mx = jnp.max(s, axis=-1, keepdims=True)
    e = jnp.exp2(s - mx)
    den = jnp.sum(e, axis=-1, keepdims=True)
    return e.astype(BF16), den


def _attn_a_kernel(g_ref, q_ref, k_ref, v_ref, w_ref, o_ref, w_o_ref, bias_s):
    @pl.when(pl.program_id(1) == 0)
    def _():
        for hh in range(HEADS_PER_STEP_A):
            g = jnp.broadcast_to(g_ref[hh] * LOG2E, (QB_A, REL_ROW))
            bias_s[hh] = pltpu.roll(g, 0, axis=1, stride=1, stride_axis=0)[:, :BAND_A]

    n_blocks = q_ref.shape[1] // QB_A
    items = [(m, hh) for m in range(n_blocks) for hh in range(HEADS_PER_STEP_A)]

    def window(m):
        k0 = max(m * QB_A - PAD_A, 0)
        return k0, (m + 1) * QB_A

    def lanes(hh):
        return slice(hh * HEAD_DIM, (hh + 1) * HEAD_DIM)

    def scores(m, hh):
        k0, k1 = window(m)
        return _attn_a_scores(q_ref[hh, m * QB_A:k1, :], k_ref[hh, k0:k1, :])

    def finish(m, hh, e, den):
        k0, k1 = window(m)
        o = _dot(e, v_ref[hh, k0:k1, :])
        o_ref[m * QB_A:k1, lanes(hh)] = (o / den).astype(o_ref.dtype)

    s_queue = [scores(*items[t]) for t in range(QK_AHEAD)]
    pending = None
    for t, (m, hh) in enumerate(items):
        s = s_queue.pop(0)
        if t + QK_AHEAD < len(items):
            s_queue.append(scores(*items[t + QK_AHEAD]))
        k0, k1 = window(m)
        lo = BAND_A - (k1 - k0)
        e, den = _attn_a_softmax(s, bias_s[hh, :, lo:], lo)
        if pending is not None:
            finish(*pending)
        pending = (m, hh, e, den)
    finish(*pending)

    w_o_ref[...] = w_ref[...].astype(BF16)


def _attn_a(z, bias_rows, w_cast):
    b, _, s, _ = z.shape
    hp = HEADS_PER_STEP_A
    n_groups = N_HEADS_A // hp
    width = hp * HEAD_DIM
    wk, wn = w_cast.shape
    slab = wk // (n_groups * b)
    return pl.pallas_call(
        _attn_a_kernel,
        grid=(n_groups, b),
        in_specs=[pl.BlockSpec((hp, 1, REL_ROW), lambda h, bi: (h, 0, 0)),
                  pl.BlockSpec((None, hp, s, HEAD_DIM), lambda h, bi: (bi, h, 0, 0)),
                  pl.BlockSpec((None, hp, s, HEAD_DIM), lambda h, bi: (bi, n_groups + h, 0, 0)),
                  pl.BlockSpec((None, hp, s, HEAD_DIM), lambda h, bi: (bi, 2 * n_groups + h, 0, 0)),
                  pl.BlockSpec((slab, wn), lambda h, bi: (h * b + bi, 0))],
        out_specs=[pl.BlockSpec((None, s, width), lambda h, bi: (bi, 0, h)),
                   pl.BlockSpec((slab, wn), lambda h, bi: (h * b + bi, 0))],
        out_shape=[jax.ShapeDtypeStruct((b, s, D_A), BF16),
                   jax.ShapeDtypeStruct((wk, wn), BF16)],
        scratch_shapes=[pltpu.VMEM((hp, QB_A, BAND_A), F32)],
        compiler_params=_cparams(2),
        name="attn_band",
    )(bias_rows, z, z, z, w_cast)


def _fox_prep_kernel(u_ref, wft_ref, bf_ref, f_ref, ft_ref):
    seq, d = u_ref.shape
    wft = jnp.concatenate([wft_ref[...], jnp.zeros((LANES - N_HEADS_B, d), F32)], axis=0).astype(BF16)
    fl = lax.dot_general(u_ref[...], wft, (((1,), (1,)), ((), ())), preferred_element_type=F32)
    fl = fl + bf_ref[...]
    x = jnp.minimum(fl, 0.0) - jnp.log1p(jnp.exp(-jnp.abs(fl)))
    row = lax.broadcasted_iota(jnp.int32, x.shape, 0)
    shift = 1
    while shift < seq:
        x = x + jnp.where(row >= shift, pltpu.roll(x, shift, axis=0), 0.0)
        shift *= 2
    x = x * LOG2E
    f_ref[...] = x
    ft_ref[...] = jnp.transpose(x)[:N_HEADS_B, :]


def _fox_prep(u3, w_in_t, b_f):
    b, s, d = u3.shape
    return pl.pallas_call(
        _fox_prep_kernel,
        grid=(b,),
        in_specs=[pl.BlockSpec((None, s, d), lambda bi: (bi, 0, 0)),
                  pl.BlockSpec((N_HEADS_B, d), lambda bi: (D_QKV // N_HEADS_B, 0)),
                  pl.BlockSpec((1, LANES), lambda bi: (0, 0))],
        out_specs=[pl.BlockSpec((None, s, LANES), lambda bi: (bi, 0, 0)),
                   pl.BlockSpec((None, N_HEADS_B, s), lambda bi: (bi, 0, 0))],
        out_shape=[jax.ShapeDtypeStruct((b, s, LANES), F32),
                   jax.ShapeDtypeStruct((b, N_HEADS_B, s), F32)],
        compiler_params=_cparams(1),
        name="fox_prep",
    )(u3, w_in_t, b_f)


def _fox_kernel(q_ref, k_ref, v_ref, fq_ref, fk_ref, w_ref, o_ref, w_o_ref, v_aug):
    seq = q_ref.shape[1]
    n_blocks = seq // TQ_B
    group = pl.program_id(1)
    lane = lax.broadcasted_iota(jnp.int32, (TQ_B, LANES), 1)
    row = lax.broadcasted_iota(jnp.int32, (TQ_B, TQ_B), 0)
    col = lax.broadcasted_iota(jnp.int32, (TQ_B, TQ_B), 1)
    nt = (((1,), (1,)), ((), ()))

    def lanes(hh):
        return slice(hh * HEAD_DIM, (hh + 1) * HEAD_DIM)

    for hh in range(HEADS_PER_STEP_B):
        v_aug[hh, :, :HEAD_DIM] = v_ref[hh]
        v_aug[hh, :, HEAD_DIM:] = jnp.ones((seq, HEAD_DIM), BF16)

    items = []
    for t in range(n_blocks):
        for hh in range(HEADS_PER_STEP_B):
            items.append((hh, t if hh % 2 == 0 else n_blocks - 1 - t))

    def scores(hh, qi):
        q0, q1 = qi * TQ_B, (qi + 1) * TQ_B
        q = q_ref[hh, q0:q1, :]
        s_diag = lax.dot_general(q, k_ref[hh, q0:q1, :], nt, preferred_element_type=F32)
        s_past = (lax.dot_general(q, k_ref[hh, 0:q0, :], nt, preferred_element_type=F32)
                  if qi > 0 else None)
        return s_diag, s_past

    def softmax(hh, qi, s_d, s_p):
        q0, q1 = qi * TQ_B, (qi + 1) * TQ_B
        h = group * HEADS_PER_STEP_B + hh
        fq = jnp.sum(jnp.where(lane == h, fq_ref[q0:q1, :], 0.0), axis=1, keepdims=True)
        s_d = (s_d * (SCALE * LOG2E) + fq) - fk_ref[pl.ds(h, 1), q0:q1]
        s_d = jnp.where(row >= col, s_d, NEG_INF)
        mx = jnp.max(s_d, axis=-1, keepdims=True)
        if qi > 0:
            s_p = (s_p * (SCALE * LOG2E) + fq) - fk_ref[pl.ds(h, 1), 0:q0]
            mx = jnp.maximum(mx, jnp.max(s_p, axis=-1, keepdims=True))
        e_d = jnp.exp2(s_d - mx).astype(BF16)
        e_p = jnp.exp2(s_p - mx).astype(BF16) if qi > 0 else None
        return e_d, e_p

    def finish(hh, qi, e_d, e_p):
        q0, q1 = qi * TQ_B, (qi + 1) * TQ_B
        acc = _dot(e_d, v_aug[hh, q0:q1, :])
        if e_p is not None:
            acc = acc + _dot(e_p, v_aug[hh, 0:q0, :])
        o_ref[q0:q1, lanes(hh)] = (acc[:, :HEAD_DIM] / acc[:, HEAD_DIM:]).astype(o_ref.dtype)

    s_queue = [scores(*items[t]) for t in range(QK_AHEAD)]
    pending = None
    for t, (hh, qi) in enumerate(items):
        s_d, s_p = s_queue.pop(0)
        if t + QK_AHEAD < len(items):
            s_queue.append(scores(*items[t + QK_AHEAD]))
        e_d, e_p = softmax(hh, qi, s_d, s_p)
        if pending is not None:
            finish(*pending)
        pending = (hh, qi, e_d, e_p)
    finish(*pending)

    w_o_ref[...] = w_ref[...].astype(BF16)


def _fox(z, f_col, f_row, w_cast):
    b, _, s, _ = z.shape
    hp = HEADS_PER_STEP_B
    n_groups = N_HEADS_B // hp
    width = hp * HEAD_DIM
    base = 3 * N_HEADS_A // hp
    wk, wn = w_cast.shape
    slab = wk // (b * n_groups)
    return pl.pallas_call(
        _fox_kernel,
        grid=(b, n_groups),
        in_specs=[pl.BlockSpec((None, hp, s, HEAD_DIM), lambda bi, h: (bi, base + h, 0, 0)),
                  pl.BlockSpec((None, hp, s, HEAD_DIM), lambda bi, h: (bi, base + n_groups + h, 0, 0)),
                  pl.BlockSpec((None, hp, s, HEAD_DIM), lambda bi, h: (bi, base + 2 * n_groups + h, 0, 0)),
                  pl.BlockSpec((None, s, LANES), lambda bi, h: (bi, 0, 0)),
                  pl.BlockSpec((None, N_HEADS_B, s), lambda bi, h: (bi, 0, 0)),
                  pl.BlockSpec((slab, wn), lambda bi, h: (bi * n_groups + h, 0))],
        out_specs=[pl.BlockSpec((None, s, width), lambda bi, h: (bi, 0, h)),
                   pl.BlockSpec((slab, wn), lambda bi, h: (bi * n_groups + h, 0))],
        out_shape=[jax.ShapeDtypeStruct((b, s, D_B), BF16),
                   jax.ShapeDtypeStruct((wk, wn), BF16)],
        scratch_shapes=[pltpu.VMEM((hp, s, 2 * HEAD_DIM), BF16)],
        compiler_params=_cparams(2),
        name="attn_fox",
    )(z, z, z, f_col, f_row, w_cast)


def _rel_bias_rows(rel_bias):
    far = rel_bias[:, 2 * REL_CLIP:]
    n_head = PAD_A - REL_CLIP
    n_tail = REL_ROW - n_head - (2 * REL_CLIP + 1)
    rows = jnp.concatenate([jnp.tile(far, (1, n_head)), jnp.flip(rel_bias, axis=1),
                            jnp.tile(far, (1, n_tail))], axis=1)
    return rows[:, None, :]


def kernel(x, p, g_ffn1, w_ffn1_gu, w_ffn1_down, g_mix, w_in, b_forget, rel_bias, w_out,
           g_ffn2, w_ffn2_gu, w_ffn2_down, g_ple, w_ple_gate, w_ple_proj, g_final):
    b, s, d = x.shape
    m = b * s
    depth = p.shape[0]
    h = x.reshape(m, d)
    for i in range(depth):
        hid, w_down1 = _gateup(h, g_ffn1[i], w_ffn1_gu[i], w_ffn1_down[i], GATEUP_TN_FUSED)
        h, u, w_down2 = _down(hid, w_down1, h, g_mix[i], w_ffn2_down[i])
        w_in_t = jnp.swapaxes(w_in[i], 0, 1)
        z = _proj(u, w_in_t, D_QKV, b)
        b_f = jnp.pad(b_forget[i], (0, LANES - N_HEADS_B)).reshape(1, LANES)
        f_col, f_row = _fox_prep(u.reshape(b, s, d), w_in_t, b_f)
        o_a, w_out_bf = _attn_a(z, _rel_bias_rows(rel_bias[i]), w_out[i])
        o_b, w_gate_bf = _fox(z, f_col, f_row, w_ple_gate[i])
        h, xn = _outproj(o_a.reshape(m, D_A), o_b.reshape(m, D_B), w_out_bf, h, g_ffn2[i])
        hid, = _gateup(xn, None, w_ffn2_gu[i], None, GATEUP_TN)
        h = _down_ple(hid, w_down2, h, g_ple[i], p[i].reshape(m, D_PLE), w_gate_bf, w_ple_proj[i],
                      g_final, i == depth - 1)
    return h.reshape(b, s, d)
```

```python
import functools

import jax
import jax.numpy as jnp
from jax import lax
from jax.experimental import pallas as pl
from jax.experimental.pallas import tpu as pltpu

F32 = jnp.float32
BF16 = jnp.bfloat16

D_MODEL = 2048
CHUNK = 64
N_LEFT_CHUNKS = 8
HEAD_DIM = 128
N_HEADS_A = 8
N_HEADS_B = 8
D_A = N_HEADS_A * HEAD_DIM
D_B = N_HEADS_B * HEAD_DIM
REL_CLIP = 256
D_FF = 5632
D_PLE = 256
EPS = 1e-6
NEG_INF = -1e30
D_QKV = 3 * D_A + 3 * D_B
SCALE = HEAD_DIM ** -0.5
LOG2E = 1.4426950408889634

LANES = 128
VMEM_LIMIT = 56 * 1024 * 1024

QB_A = 2 * CHUNK
BAND_A = (N_LEFT_CHUNKS + 2) * CHUNK
PAD_A = N_LEFT_CHUNKS * CHUNK
REL_ROW = 1024

GATEUP_ROWS = 1024
GATEUP_TN = 256
GATEUP_TN_FUSED = 128
NORM_ROWS = 512
TQ_B = 256
HEADS_PER_STEP_A = 2
HEADS_PER_STEP_B = 4
QK_AHEAD = 2


def _cparams(n_axes):
    return pltpu.CompilerParams(
        dimension_semantics=("arbitrary",) * n_axes,
        vmem_limit_bytes=VMEM_LIMIT,
    )


def _rms(x, g):
    ms = jnp.mean(x * x, axis=-1, keepdims=True)
    return (x * lax.rsqrt(ms + EPS)) * g


def _dot(a, b):
    return jnp.dot(a, b, preferred_element_type=F32)


def _gateup_kernel(n_norm, has_rider, *refs):
    refs = list(refs)
    x_ref, g_ref = (refs.pop(0), refs.pop(0)) if n_norm else (None, None)
    a_ref = None if n_norm else refs.pop(0)
    wg_ref, wu_ref = refs.pop(0), refs.pop(0)
    wc_ref = refs.pop(0) if has_rider else None
    o_ref = refs.pop(0)
    wc_o_ref = refs.pop(0) if has_rider else None
    w_s = refs.pop(0)
    if n_norm:
        a_ref = refs.pop(0)
    m, tn = o_ref.shape

    def matmul_phase():
        w_s[:, :tn] = wg_ref[...].astype(BF16)
        w_s[:, tn:] = wu_ref[...].astype(BF16)

        if has_rider:
            wc_o_ref[...] = wc_ref[...].astype(BF16)

        w = w_s[...]
        for c in range(m // GATEUP_ROWS):
            rows = slice(c * GATEUP_ROWS, (c + 1) * GATEUP_ROWS)
            r = _dot(a_ref[rows, :], w)
            g = r[:, :tn]
            u = r[:, tn:]
            o_ref[rows, :] = ((g * jax.nn.sigmoid(g)) * u).astype(o_ref.dtype)

    if n_norm:
        step = pl.program_id(0)

        @pl.when(step < n_norm)
        def _():
            r0 = pl.multiple_of(step * NORM_ROWS, NORM_ROWS)
            a_ref[pl.ds(r0, NORM_ROWS), :] = _rms(x_ref[...], g_ref[...]).astype(BF16)

        pl.when(step >= n_norm)(matmul_phase)
    else:
        matmul_phase()


def _gateup(a, g_norm, w_gu, w_cast, tn):
    m, k = a.shape
    nj = D_FF // tn
    n_norm = 0 if g_norm is None else m // NORM_ROWS
    has_rider = w_cast is not None

    def col(s):
        return jnp.maximum(s - n_norm, 0)

    operands, in_specs = [], []
    scratch = [pltpu.VMEM((k, 2 * tn), BF16)]
    if n_norm:
        operands += [a, g_norm.reshape(1, k)]
        in_specs += [pl.BlockSpec((NORM_ROWS, k), lambda s: (jnp.minimum(s, n_norm - 1), 0)),
                     pl.BlockSpec((1, k), lambda s: (0, 0))]
        scratch.append(pltpu.VMEM((m, k), BF16))
    else:
        operands.append(a)
        in_specs.append(pl.BlockSpec((m, k), lambda s: (0, 0), pipeline_mode=pl.Buffered(1)))
    operands += [w_gu, w_gu]
    in_specs += [pl.BlockSpec((k, tn), lambda s: (0, col(s))),
                 pl.BlockSpec((k, tn), lambda s: (0, col(s) + nj))]
    out_specs = [pl.BlockSpec((m, tn), lambda s: (0, col(s)))]
    out_shape = [jax.ShapeDtypeStruct((m, D_FF), BF16)]
    if has_rider:
        wk, wn = w_cast.shape
        slab = wk // nj
        operands.append(w_cast)
        in_specs.append(pl.BlockSpec((slab, wn), lambda s: (col(s), 0)))
        out_specs.append(pl.BlockSpec((slab, wn), lambda s: (col(s), 0)))
        out_shape.append(jax.ShapeDtypeStruct((wk, wn), BF16))
    return pl.pallas_call(
        functools.partial(_gateup_kernel, n_norm, has_rider),
        grid=(n_norm + nj,),
        in_specs=in_specs,
        out_specs=out_specs,
        out_shape=out_shape,
        scratch_shapes=scratch,
        compiler_params=_cparams(1),
        name="ffn_gateup",
    )(*operands)


def _proj_kernel(a_ref, wt_ref, o_ref, w_s):
    @pl.when(pl.program_id(1) == 0)
    def _():
        w_s[...] = wt_ref[...].astype(BF16)

    r = lax.dot_general(a_ref[...], w_s[...], (((1,), (1,)), ((), ())), preferred_element_type=F32)
    for hh in range(o_ref.shape[0]):
        o_ref[hh] = r[:, hh * HEAD_DIM:(hh + 1) * HEAD_DIM].astype(o_ref.dtype)


def _proj(a, w_t, n_out, batch, tn=1024):
    m, k = a.shape
    seq = m // batch
    heads_per_tile = tn // HEAD_DIM
    return pl.pallas_call(
        _proj_kernel,
        grid=(n_out // tn, batch),
        in_specs=[pl.BlockSpec((seq, k), lambda j, i: (i, 0)),
                  pl.BlockSpec((tn, k), lambda j, i: (j, 0))],
        out_specs=pl.BlockSpec((None, heads_per_tile, seq, HEAD_DIM), lambda j, i: (i, j, 0, 0)),
        out_shape=jax.ShapeDtypeStruct((batch, n_out // HEAD_DIM, seq, HEAD_DIM), BF16),
        scratch_shapes=[pltpu.VMEM((tn, k), BF16)],
        compiler_params=_cparams(2),
        name="in_proj",
    )(a, w_t)


def _down_kernel(a_ref, w_ref, r_ref, g_ref, wc_ref, h_ref, n_ref, wc_o_ref):
    h = r_ref[...] + 0.5 * _dot(a_ref[...], w_ref[...])
    h_ref[...] = h
    n_ref[...] = _rms(h, g_ref[...]).astype(n_ref.dtype)
    wc_o_ref[...] = wc_ref[...].astype(BF16)


def _down(a, w_bf16, resid, g_next, w_cast, tm=256):
    m, k = a.shape
    n = w_bf16.shape[1]
    steps = m // tm
    wk, wn = w_cast.shape
    slab = wk // steps
    return pl.pallas_call(
        _down_kernel,
        grid=(steps,),
        in_specs=[pl.BlockSpec((tm, k), lambda i: (i, 0)),
                  pl.BlockSpec((k, n), lambda i: (0, 0), pipeline_mode=pl.Buffered(1)),
                  pl.BlockSpec((tm, n), lambda i: (i, 0)),
                  pl.BlockSpec((1, n), lambda i: (0, 0)),
                  pl.BlockSpec((slab, wn), lambda i: (i, 0))],
        out_specs=[pl.BlockSpec((tm, n), lambda i: (i, 0)),
                   pl.BlockSpec((tm, n), lambda i: (i, 0)),
                   pl.BlockSpec((slab, wn), lambda i: (i, 0))],
        out_shape=[jax.ShapeDtypeStruct((m, n), F32), jax.ShapeDtypeStruct((m, n), BF16),
                   jax.ShapeDtypeStruct((wk, wn), BF16)],
        compiler_params=_cparams(1),
        name="ffn_down",
    )(a, w_bf16, resid, g_next.reshape(1, n), w_cast)


def _outproj_kernel(o_ref, w_ref, r_ref, g_ref, h_ref, n_ref):
    h = r_ref[...] + _dot(o_ref[...], w_ref[...])
    h_ref[...] = h
    n_ref[...] = _rms(h, g_ref[...]).astype(n_ref.dtype)


def _outproj(o, w_out, resid, g_next, tm=512):
    m = o.shape[0]
    k, n = w_out.shape
    return pl.pallas_call(
        _outproj_kernel,
        grid=(m // tm,),
        in_specs=[pl.BlockSpec((tm, k), lambda i: (i, 0)),
                  pl.BlockSpec((k, n), lambda i: (0, 0), pipeline_mode=pl.Buffered(1)),
                  pl.BlockSpec((tm, n), lambda i: (i, 0)),
                  pl.BlockSpec((1, n), lambda i: (0, 0))],
        out_specs=[pl.BlockSpec((tm, n), lambda i: (i, 0)),
                   pl.BlockSpec((tm, n), lambda i: (i, 0))],
        out_shape=[jax.ShapeDtypeStruct((m, n), F32), jax.ShapeDtypeStruct((m, n), BF16)],
        compiler_params=_cparams(1),
        name="out_proj",
    )(o, w_out, resid, g_next.reshape(1, n))


def _down_ple_kernel(close_block, a_ref, w_ref, r_ref, gp_ref, p_ref, wg_ref, wp_ref, gf_ref, o_ref, wp_s):
    @pl.when(pl.program_id(0) == 0)
    def _():
        wp_s[...] = wp_ref[...].astype(BF16)

    h = r_ref[...] + 0.5 * _dot(a_ref[...], w_ref[...])
    hn = _rms(h, gp_ref[...]).astype(BF16)
    gate = jax.nn.sigmoid(_dot(hn, wg_ref[...]))
    emb = _dot(p_ref[...].astype(BF16), wp_s[...])
    h = h + gate * emb
    o_ref[...] = _rms(h, gf_ref[...]) if close_block else h


def _down_ple(a, w_down, resid, g_ple, p, w_gate, w_proj, g_final, close_block, tm=256):
    m, k = a.shape
    n = w_down.shape[1]
    const = dict(pipeline_mode=pl.Buffered(1))
    return pl.pallas_call(
        functools.partial(_down_ple_kernel, close_block),
        grid=(m // tm,),
        in_specs=[pl.BlockSpec((tm, k), lambda i: (i, 0)),
                  pl.BlockSpec((k, n), lambda i: (0, 0), **const),
                  pl.BlockSpec((tm, n), lambda i: (i, 0)),
                  pl.BlockSpec((1, n), lambda i: (0, 0)),
                  pl.BlockSpec((tm, D_PLE), lambda i: (i, 0)),
                  pl.BlockSpec((n, n), lambda i: (0, 0), **const),
                  pl.BlockSpec((D_PLE, n), lambda i: (0, 0), **const),
                  pl.BlockSpec((1, n), lambda i: (0, 0))],
        out_specs=pl.BlockSpec((tm, n), lambda i: (i, 0)),
        out_shape=jax.ShapeDtypeStruct((m, n), F32),
        scratch_shapes=[pltpu.VMEM((D_PLE, n), BF16)],
        compiler_params=_cparams(1),
        name="ffn_down_ple",
    )(a, w_down, resid, g_ple.reshape(1, n), p, w_gate, w_proj, g_final.reshape(1, n))


def _attn_a_scores(q, k):
    return lax.dot_general(q, k, (((1,), (1,)), ((), ())), preferred_element_type=F32)


def _attn_a_softmax(s, bias, key_lo):
    nk = s.shape[1]
    s = s * (SCALE * LOG2E) + bias
    row = lax.broadcasted_iota(jnp.int32, (QB_A, nk), 0)
    col = lax.broadcasted_iota(jnp.int32, (QB_A, nk), 1) + key_lo
    first = row < CHUNK
    valid = (first & (col < BAND_A - CHUNK)) | (jnp.logical_not(first) & (col >= CHUNK))
    s = jnp.where(valid, s, NEG_INF)
    mx = jnp.max(s, axis=-1, keepdims=True)
    e = jnp.exp2(s - mx)
    den = jnp.sum(e, axis=-1, keepdims=True)
    return e.astype(BF16), den


def _attn_a_kernel(g_ref, q_ref, k_ref, v_ref, w_ref, o_ref, w_o_ref, bias_s):
    @pl.when(pl.program_id(1) == 0)
    def _():
        for hh in range(HEADS_PER_STEP_A):
            g = jnp.broadcast_to(g_ref[hh] * LOG2E, (QB_A, REL_ROW))
            bias_s[hh] = pltpu.roll(g, 0, axis=1, stride=1, stride_axis=0)[:, :BAND_A]

    n_blocks = q_ref.shape[1] // QB_A
    items = [(m, hh) for m in range(n_blocks) for hh in range(HEADS_PER_STEP_A)]

    def window(m):
        k0 = max(m * QB_A - PAD_A, 0)
        return k0, (m + 1) * QB_A

    def lanes(hh):
        return slice(hh * HEAD_DIM, (hh + 1) * HEAD_DIM)

    def scores(m, hh):
        k0, k1 = window(m)
        return _attn_a_scores(q_ref[hh, m * QB_A:k1, :], k_ref[hh, k0:k1, :])

    def finish(m, hh, e, den):
        k0, k1 = window(m)
        o = _dot(e, v_ref[hh, k0:k1, :])
        o_ref[m * QB_A:k1, lanes(hh)] = (o / den).astype(o_ref.dtype)

    s_queue = [scores(*items[t]) for t in range(QK_AHEAD)]
    pending = None
    for t, (m, hh) in enumerate(items):
        s = s_queue.pop(0)
        if t + QK_AHEAD < len(items):
            s_queue.append(scores(*items[t + QK_AHEAD]))
        k0, k1 = window(m)
        lo = BAND_A - (k1 - k0)
        e, den = _attn_a_softmax(s, bias_s[hh, :, lo:], lo)
        if pending is not None:
            finish(*pending)
        pending = (m, hh, e, den)
    finish(*pending)

    w_o_ref[...] = w_ref[...].astype(BF16)


def _attn_a(z, bias_rows, w_cast):
    b, _, s, _ = z.shape
    hp = HEADS_PER_STEP_A
    n_groups = N_HEADS_A // hp
    width = hp * HEAD_DIM
    wk, wn = w_cast.shape
    slab = wk // (n_groups * b)
    return pl.pallas_call(
        _attn_a_kernel,
        grid=(n_groups, b),
        in_specs=[pl.BlockSpec((hp, 1, REL_ROW), lambda h, bi: (h, 0, 0)),
                  pl.BlockSpec((None, hp, s, HEAD_DIM), lambda h, bi: (bi, h, 0, 0)),
                  pl.BlockSpec((None, hp, s, HEAD_DIM), lambda h, bi: (bi, n_groups + h, 0, 0)),
                  pl.BlockSpec((None, hp, s, HEAD_DIM), lambda h, bi: (bi, 2 * n_groups + h, 0, 0)),
                  pl.BlockSpec((slab, wn), lambda h, bi: (h * b + bi, 0))],
        out_specs=[pl.BlockSpec((None, s, width), lambda h, bi: (bi, 0, h)),
                   pl.BlockSpec((slab, wn), lambda h, bi: (h * b + bi, 0))],
        out_shape=[jax.ShapeDtypeStruct((b, s, D_A + D_B), BF16),
                   jax.ShapeDtypeStruct((wk, wn), BF16)],
        scratch_shapes=[pltpu.VMEM((hp, QB_A, BAND_A), F32)],
        compiler_params=_cparams(2),
        name="attn_band",
    )(bias_rows, z, z, z, w_cast)


def _fox_prep_kernel(u_ref, wft_ref, bf_ref, f_ref, ft_ref):
    seq, d = u_ref.shape
    wft = jnp.concatenate([wft_ref[...], jnp.zeros((LANES - N_HEADS_B, d), F32)], axis=0).astype(BF16)
    fl = lax.dot_general(u_ref[...], wft, (((1,), (1,)), ((), ())), preferred_element_type=F32)
    fl = fl + bf_ref[...]
    x = jnp.minimum(fl, 0.0) - jnp.log1p(jnp.exp(-jnp.abs(fl)))
    row = lax.broadcasted_iota(jnp.int32, x.shape, 0)
    shift = 1
    while shift < seq:
        x = x + jnp.where(row >= shift, pltpu.roll(x, shift, axis=0), 0.0)
        shift *= 2
    x = x * LOG2E
    f_ref[...] = x
    ft_ref[...] = jnp.transpose(x)[:N_HEADS_B, :]


def _fox_prep(u3, w_in_t, b_f):
    b, s, d = u3.shape
    return pl.pallas_call(
        _fox_prep_kernel,
        grid=(b,),
        in_specs=[pl.BlockSpec((None, s, d), lambda bi: (bi, 0, 0)),
                  pl.BlockSpec((N_HEADS_B, d), lambda bi: (D_QKV // N_HEADS_B, 0)),
                  pl.BlockSpec((1, LANES), lambda bi: (0, 0))],
        out_specs=[pl.BlockSpec((None, s, LANES), lambda bi: (bi, 0, 0)),
                   pl.BlockSpec((None, N_HEADS_B, s), lambda bi: (bi, 0, 0))],
        out_shape=[jax.ShapeDtypeStruct((b, s, LANES), F32),
                   jax.ShapeDtypeStruct((b, N_HEADS_B, s), F32)],
        compiler_params=_cparams(1),
        name="fox_prep",
    )(u3, w_in_t, b_f)


def _fox_kernel(q_ref, k_ref, v_ref, fq_ref, fk_ref, w_ref, o_prev_ref, o_ref, w_o_ref, v_aug):
    del o_prev_ref
    seq = q_ref.shape[1]
    n_blocks = seq // TQ_B
    group = pl.program_id(1)
    lane = lax.broadcasted_iota(jnp.int32, (TQ_B, LANES), 1)
    row = lax.broadcasted_iota(jnp.int32, (TQ_B, TQ_B), 0)
    col = lax.broadcasted_iota(jnp.int32, (TQ_B, TQ_B), 1)
    nt = (((1,), (1,)), ((), ()))

    def lanes(hh):
        return slice(hh * HEAD_DIM, (hh + 1) * HEAD_DIM)

    for hh in range(HEADS_PER_STEP_B):
        v_aug[hh, :, :HEAD_DIM] = v_ref[hh]
        v_aug[hh, :, HEAD_DIM:] = jnp.ones((seq, HEAD_DIM), BF16)

    items = []
    for t in range(n_blocks):
        for hh in range(HEADS_PER_STEP_B):
            items.append((hh, t if hh % 2 == 0 else n_blocks - 1 - t))

    def scores(hh, qi):
        q0, q1 = qi * TQ_B, (qi + 1) * TQ_B
        q = q_ref[hh, q0:q1, :]
        s_diag = lax.dot_general(q, k_ref[hh, q0:q1, :], nt, preferred_element_type=F32)
        s_past = (lax.dot_general(q, k_ref[hh, 0:q0, :], nt, preferred_element_type=F32)
                  if qi > 0 else None)
        return s_diag, s_past

    def softmax(hh, qi, s_d, s_p):
        q0, q1 = qi * TQ_B, (qi + 1) * TQ_B
        h = group * HEADS_PER_STEP_B + hh
        fq = jnp.sum(jnp.where(lane == h, fq_ref[q0:q1, :], 0.0), axis=1, keepdims=True)
        s_d = (s_d * (SCALE * LOG2E) + fq) - fk_ref[pl.ds(h, 1), q0:q1]
        s_d = jnp.where(row >= col, s_d, NEG_INF)
        mx = jnp.max(s_d, axis=-1, keepdims=True)
        if qi > 0:
            s_p = (s_p * (SCALE * LOG2E) + fq) - fk_ref[pl.ds(h, 1), 0:q0]
            mx = jnp.maximum(mx, jnp.max(s_p, axis=-1, keepdims=True))
        e_d = jnp.exp2(s_d - mx).astype(BF16)
        e_p = jnp.exp2(s_p - mx).astype(BF16) if qi > 0 else None
        return e_d, e_p

    def finish(hh, qi, e_d, e_p):
        q0, q1 = qi * TQ_B, (qi + 1) * TQ_B
        acc = _dot(e_d, v_aug[hh, q0:q1, :])
        if e_p is not None:
            acc = acc + _dot(e_p, v_aug[hh, 0:q0, :])
        o_ref[q0:q1, lanes(hh)] = (acc[:, :HEAD_DIM] / acc[:, HEAD_DIM:]).astype(o_ref.dtype)

    s_queue = [scores(*items[t]) for t in range(QK_AHEAD)]
    pending = None
    for t, (hh, qi) in enumerate(items):
        s_d, s_p = s_queue.pop(0)
        if t + QK_AHEAD < len(items):
            s_queue.append(scores(*items[t + QK_AHEAD]))
        e_d, e_p = softmax(hh, qi, s_d, s_p)
        if pending is not None:
            finish(*pending)
        pending = (hh, qi, e_d, e_p)
    finish(*pending)

    w_o_ref[...] = w_ref[...].astype(BF16)


def _fox(z, f_col, f_row, w_cast, o_prev):
    b, _, s, _ = z.shape
    hp = HEADS_PER_STEP_B
    n_groups = N_HEADS_B // hp
    width = hp * HEAD_DIM
    base = 3 * N_HEADS_A // hp
    wk, wn = w_cast.shape
    slab = wk // (b * n_groups)
    return pl.pallas_call(
        _fox_kernel,
        grid=(b, n_groups),
        in_specs=[pl.BlockSpec((None, hp, s, HEAD_DIM), lambda bi, h: (bi, base + h, 0, 0)),
                  pl.BlockSpec((None, hp, s, HEAD_DIM), lambda bi, h: (bi, base + n_groups + h, 0, 0)),
                  pl.BlockSpec((None, hp, s, HEAD_DIM), lambda bi, h: (bi, base + 2 * n_groups + h, 0, 0)),
                  pl.BlockSpec((None, s, LANES), lambda bi, h: (bi, 0, 0)),
                  pl.BlockSpec((None, N_HEADS_B, s), lambda bi, h: (bi, 0, 0)),
                  pl.BlockSpec((slab, wn), lambda bi, h: (bi * n_groups + h, 0)),
                  pl.BlockSpec(memory_space=pl.ANY)],
        out_specs=[pl.BlockSpec((None, s, width), lambda bi, h: (bi, 0, D_A // width + h)),
                   pl.BlockSpec((slab, wn), lambda bi, h: (bi * n_groups + h, 0))],
        out_shape=[jax.ShapeDtypeStruct(o_prev.shape, BF16),
                   jax.ShapeDtypeStruct((wk, wn), BF16)],
        input_output_aliases={6: 0},
        scratch_shapes=[pltpu.VMEM((hp, s, 2 * HEAD_DIM), BF16)],
        compiler_params=_cparams(2),
        name="attn_fox",
    )(z, z, z, f_col, f_row, w_cast, o_prev)


def _rel_bias_rows(rel_bias):
    far = rel_bias[:, 2 * REL_CLIP:]
    n_head = PAD_A - REL_CLIP
    n_tail = REL_ROW - n_head - (2 * REL_CLIP + 1)
    rows = jnp.concatenate([jnp.tile(far, (1, n_head)), jnp.flip(rel_bias, axis=1),
                            jnp.tile(far, (1, n_tail))], axis=1)
    return rows[:, None, :]


def kernel(x, p, g_ffn1, w_ffn1_gu, w_ffn1_down, g_mix, w_in, b_forget, rel_bias, w_out,
           g_ffn2, w_ffn2_gu, w_ffn2_down, g_ple, w_ple_gate, w_ple_proj, g_final):
    b, s, d = x.shape
    m = b * s
    depth = p.shape[0]
    h = x.reshape(m, d)
    for i in range(depth):
        hid, w_down1 = _gateup(h, g_ffn1[i], w_ffn1_gu[i], w_ffn1_down[i], GATEUP_TN_FUSED)
        h, u, w_down2 = _down(hid, w_down1, h, g_mix[i], w_ffn2_down[i])
        w_in_t = jnp.swapaxes(w_in[i], 0, 1)
        z = _proj(u, w_in_t, D_QKV, b)
        b_f = jnp.pad(b_forget[i], (0, LANES - N_HEADS_B)).reshape(1, LANES)
        f_col, f_row = _fox_prep(u.reshape(b, s, d), w_in_t, b_f)
        o_a, w_out_bf = _attn_a(z, _rel_bias_rows(rel_bias[i]), w_out[i])
        o, w_gate_bf = _fox(z, f_col, f_row, w_ple_gate[i], o_a)
        h, xn = _outproj(o.reshape(m, D_A + D_B), w_out_bf, h, g_ffn2[i])
        hid, = _gateup(xn, None, w_ffn2_gu[i], None, GATEUP_TN)
        h = _down_ple(hid, w_down2, h, g_ple[i], p[i].reshape(m, D_PLE), w_gate_bf, w_ple_proj[i],
                      g_final, i == depth - 1)
    return h.reshape(b, s, d)
```

```python
import functools

import jax
import jax.numpy as jnp
from jax import lax
from jax.experimental import pallas as pl
from jax.experimental.pallas import tpu as pltpu

F32 = jnp.float32
BF16 = jnp.bfloat16

D_MODEL = 2048
CHUNK = 64
N_LEFT_CHUNKS = 8
HEAD_DIM = 128
N_HEADS_A = 8
N_HEADS_B = 8
D_A = N_HEADS_A * HEAD_DIM
D_B = N_HEADS_B * HEAD_DIM
REL_CLIP = 256
D_FF = 5632
D_PLE = 256
EPS = 1e-6
NEG_INF = -1e30
D_QKV = 3 * D_A + 3 * D_B
SCALE = HEAD_DIM ** -0.5
LOG2E = 1.4426950408889634

LANES = 128
VMEM_LIMIT = 56 * 1024 * 1024

QB_A = 2 * CHUNK
BAND_A = (N_LEFT_CHUNKS + 2) * CHUNK
PAD_A = N_LEFT_CHUNKS * CHUNK
REL_ROW = 1024

GATEUP_ROWS = 1024
GATEUP_TN = 256
GATEUP_TN_FUSED = 128
NORM_ROWS = 512
TQ_B = 256
HEADS_PER_STEP_A = 2
HEADS_PER_STEP_B = 4
QK_AHEAD = 2


def _cparams(n_axes):
    return pltpu.CompilerParams(
        dimension_semantics=("arbitrary",) * n_axes,
        vmem_limit_bytes=VMEM_LIMIT,
    )


def _rms(x, g):
    ms = jnp.mean(x * x, axis=-1, keepdims=True)
    return (x * lax.rsqrt(ms + EPS)) * g


def _dot(a, b):
    return jnp.dot(a, b, preferred_element_type=F32)


def _gateup_kernel(n_norm, has_rider, *refs):
    refs = list(refs)
    x_ref, g_ref = (refs.pop(0), refs.pop(0)) if n_norm else (None, None)
    a_ref = None if n_norm else refs.pop(0)
    wg_ref, wu_ref = refs.pop(0), refs.pop(0)
    wc_ref = refs.pop(0) if has_rider else None
    o_ref = refs.pop(0)
    wc_o_ref = refs.pop(0) if has_rider else None
    w_s = refs.pop(0)
    if n_norm:
        a_ref = refs.pop(0)
    m, tn = o_ref.shape

    def matmul_phase():
        w_s[:, :tn] = wg_ref[...].astype(BF16)
        w_s[:, tn:] = wu_ref[...].astype(BF16)

        if has_rider:
            wc_o_ref[...] = wc_ref[...].astype(BF16)

        w = w_s[...]
        for c in range(m // GATEUP_ROWS):
            rows = slice(c * GATEUP_ROWS, (c + 1) * GATEUP_ROWS)
            r = _dot(a_ref[rows, :], w)
            g = r[:, :tn]
            u = r[:, tn:]
            o_ref[rows, :] = ((g * jax.nn.sigmoid(g)) * u).astype(o_ref.dtype)

    if n_norm:
        step = pl.program_id(0)

        @pl.when(step < n_norm)
        def _():
            r0 = pl.multiple_of(step * NORM_ROWS, NORM_ROWS)
            a_ref[pl.ds(r0, NORM_ROWS), :] = _rms(x_ref[...], g_ref[...]).astype(BF16)

        pl.when(step >= n_norm)(matmul_phase)
    else:
        matmul_phase()


def _gateup(a, g_norm, w_gu, w_cast, tn):
    m, k = a.shape
    nj = D_FF // tn
    n_norm = 0 if g_norm is None else m // NORM_ROWS
    has_rider = w_cast is not None

    def col(s):
        return jnp.maximum(s - n_norm, 0)

    operands, in_specs = [], []
    scratch = [pltpu.VMEM((k, 2 * tn), BF16)]
    if n_norm:
        operands += [a, g_norm.reshape(1, k)]
        in_specs += [pl.BlockSpec((NORM_ROWS, k), lambda s: (jnp.minimum(s, n_norm - 1), 0)),
                     pl.BlockSpec((1, k), lambda s: (0, 0))]
        scratch.append(pltpu.VMEM((m, k), BF16))
    else:
        operands.append(a)
        in_specs.append(pl.BlockSpec((m, k), lambda s: (0, 0), pipeline_mode=pl.Buffered(1)))
    operands += [w_gu, w_gu]
    in_specs += [pl.BlockSpec((k, tn), lambda s: (0, col(s))),
                 pl.BlockSpec((k, tn), lambda s: (0, col(s) + nj))]
    out_specs = [pl.BlockSpec((m, tn), lambda s: (0, col(s)))]
    out_shape = [jax.ShapeDtypeStruct((m, D_FF), BF16)]
    if has_rider:
        wk, wn = w_cast.shape
        slab = wk // nj
        operands.append(w_cast)
        in_specs.append(pl.BlockSpec((slab, wn), lambda s: (col(s), 0)))
        out_specs.append(pl.BlockSpec((slab, wn), lambda s: (col(s), 0)))
        out_shape.append(jax.ShapeDtypeStruct((wk, wn), BF16))
    return pl.pallas_call(
        functools.partial(_gateup_kernel, n_norm, has_rider),
        grid=(n_norm + nj,),
        in_specs=in_specs,
        out_specs=out_specs,
        out_shape=out_shape,
        scratch_shapes=scratch,
        compiler_params=_cparams(1),
        name="ffn_gateup",
    )(*operands)


def _proj_kernel(a_ref, wt_ref, o_ref, w_s):
    @pl.when(pl.program_id(1) == 0)
    def _():
        w_s[...] = wt_ref[...].astype(BF16)

    r = lax.dot_general(a_ref[...], w_s[...], (((1,), (1,)), ((), ())), preferred_element_type=F32)
    for hh in range(o_ref.shape[0]):
        o_ref[hh] = r[:, hh * HEAD_DIM:(hh + 1) * HEAD_DIM].astype(o_ref.dtype)


def _proj(a, w_t, n_out, batch, tn=1024):
    m, k = a.shape
    seq = m // batch
    heads_per_tile = tn // HEAD_DIM
    return pl.pallas_call(
        _proj_kernel,
        grid=(n_out // tn, batch),
        in_specs=[pl.BlockSpec((seq, k), lambda j, i: (i, 0)),
                  pl.BlockSpec((tn, k), lambda j, i: (j, 0))],
        out_specs=pl.BlockSpec((None, heads_per_tile, seq, HEAD_DIM), lambda j, i: (i, j, 0, 0)),
        out_shape=jax.ShapeDtypeStruct((batch, n_out // HEAD_DIM, seq, HEAD_DIM), BF16),
        scratch_shapes=[pltpu.VMEM((tn, k), BF16)],
        compiler_params=_cparams(2),
        name="in_proj",
    )(a, w_t)


def _down_kernel(a_ref, w_ref, r_ref, g_ref, wc_ref, h_ref, n_ref, wc_o_ref):
    h = r_ref[...] + 0.5 * _dot(a_ref[...], w_ref[...])
    h_ref[...] = h
    n_ref[...] = _rms(h, g_ref[...]).astype(n_ref.dtype)
    wc_o_ref[...] = wc_ref[...].astype(BF16)


def _down(a, w_bf16, resid, g_next, w_cast, tm=256):
    m, k = a.shape
    n = w_bf16.shape[1]
    steps = m // tm
    wk, wn = w_cast.shape
    slab = wk // steps
    return pl.pallas_call(
        _down_kernel,
        grid=(steps,),
        in_specs=[pl.BlockSpec((tm, k), lambda i: (i, 0)),
                  pl.BlockSpec((k, n), lambda i: (0, 0), pipeline_mode=pl.Buffered(1)),
                  pl.BlockSpec((tm, n), lambda i: (i, 0)),
                  pl.BlockSpec((1, n), lambda i: (0, 0)),
                  pl.BlockSpec((slab, wn), lambda i: (i, 0))],
        out_specs=[pl.BlockSpec((tm, n), lambda i: (i, 0)),
                   pl.BlockSpec((tm, n), lambda i: (i, 0)),
                   pl.BlockSpec((slab, wn), lambda i: (i, 0))],
        out_shape=[jax.ShapeDtypeStruct((m, n), F32), jax.ShapeDtypeStruct((m, n), BF16),
                   jax.ShapeDtypeStruct((wk, wn), BF16)],
        compiler_params=_cparams(1),
        name="ffn_down",
    )(a, w_bf16, resid, g_next.reshape(1, n), w_cast)


def _outproj_kernel(oa_ref, ob_ref, w_ref, r_ref, g_ref, h_ref, n_ref):
    acc = _dot(oa_ref[...], w_ref[:D_A, :])
    width = ob_ref.shape[2]
    for grp in range(ob_ref.shape[0]):
        acc = acc + _dot(ob_ref[grp], w_ref[D_A + grp * width:D_A + (grp + 1) * width, :])
    h = r_ref[...] + acc
    h_ref[...] = h
    n_ref[...] = _rms(h, g_ref[...]).astype(n_ref.dtype)


def _outproj(o_a, o_b, w_out, resid, g_next, tm=512):
    m = o_a.shape[0]
    k, n = w_out.shape
    _, groups, s, width = o_b.shape
    per_seq = s // tm
    return pl.pallas_call(
        _outproj_kernel,
        grid=(m // tm,),
        in_specs=[pl.BlockSpec((tm, D_A), lambda i: (i, 0)),
                  pl.BlockSpec((None, groups, tm, width), lambda i: (i // per_seq, 0, i % per_seq, 0)),
                  pl.BlockSpec((k, n), lambda i: (0, 0), pipeline_mode=pl.Buffered(1)),
                  pl.BlockSpec((tm, n), lambda i: (i, 0)),
                  pl.BlockSpec((1, n), lambda i: (0, 0))],
        out_specs=[pl.BlockSpec((tm, n), lambda i: (i, 0)),
                   pl.BlockSpec((tm, n), lambda i: (i, 0))],
        out_shape=[jax.ShapeDtypeStruct((m, n), F32), jax.ShapeDtypeStruct((m, n), BF16)],
        compiler_params=_cparams(1),
        name="out_proj",
    )(o_a, o_b, w_out, resid, g_next.reshape(1, n))


def _down_ple_kernel(close_block, a_ref, w_ref, r_ref, gp_ref, p_ref, wg_ref, wp_ref, gf_ref, o_ref, wp_s):
    @pl.when(pl.program_id(0) == 0)
    def _():
        wp_s[...] = wp_ref[...].astype(BF16)

    h = r_ref[...] + 0.5 * _dot(a_ref[...], w_ref[...])
    hn = _rms(h, gp_ref[...]).astype(BF16)
    gate = jax.nn.sigmoid(_dot(hn, wg_ref[...]))
    emb = _dot(p_ref[...].astype(BF16), wp_s[...])
    h = h + gate * emb
    o_ref[...] = _rms(h, gf_ref[...]) if close_block else h


def _down_ple(a, w_down, resid, g_ple, p, w_gate, w_proj, g_final, close_block, tm=256):
    m, k = a.shape
    n = w_down.shape[1]
    const = dict(pipeline_mode=pl.Buffered(1))
    return pl.pallas_call(
        functools.partial(_down_ple_kernel, close_block),
        grid=(m // tm,),
        in_specs=[pl.BlockSpec((tm, k), lambda i: (i, 0)),
                  pl.BlockSpec((k, n), lambda i: (0, 0), **const),
                  pl.BlockSpec((tm, n), lambda i: (i, 0)),
                  pl.BlockSpec((1, n), lambda i: (0, 0)),
                  pl.BlockSpec((tm, D_PLE), lambda i: (i, 0)),
                  pl.BlockSpec((n, n), lambda i: (0, 0), **const),
                  pl.BlockSpec((D_PLE, n), lambda i: (0, 0), **const),
                  pl.BlockSpec((1, n), lambda i: (0, 0))],
        out_specs=pl.BlockSpec((tm, n), lambda i: (i, 0)),
        out_shape=jax.ShapeDtypeStruct((m, n), F32),
        scratch_shapes=[pltpu.VMEM((D_PLE, n), BF16)],
        compiler_params=_cparams(1),
        name="ffn_down_ple",
    )(a, w_down, resid, g_ple.reshape(1, n), p, w_gate, w_proj, g_final.reshape(1, n))


def _attn_a_scores(q, k):
    return lax.dot_general(q, k, (((1,), (1,)), ((), ())), preferred_element_type=F32)


def _attn_a_softmax(s, bias, key_lo):
    nk = s.shape[1]
    s = s * (SCALE * LOG2E) + bias
    row = lax.broadcasted_iota(jnp.int32, (QB_A, nk), 0)
    col = lax.broadcasted_iota(jnp.int32, (QB_A, nk), 1) + key_lo
    first = row < CHUNK
    valid = (first & (col < BAND_A - CHUNK)) | (jnp.logical_not(first) & (col >= CHUNK))
    s = jnp.where(valid, s, NEG_INF)
    mx = jnp.max(s, axis=-1, keepdims=True)
    e = jnp.exp2(s - mx)
    den = jnp.sum(e, axis=-1, keepdims=True)
    return e.astype(BF16), den


def _attn_a_kernel(g_ref, q_ref, k_ref, v_ref, w_ref, o_ref, w_o_ref, bias_s):
    @pl.when(pl.program_id(1) == 0)
    def _():
        for hh in range(HEADS_PER_STEP_A):
            g = jnp.broadcast_to(g_ref[hh] * LOG2E, (QB_A, REL_ROW))
            bias_s[hh] = pltpu.roll(g, 0, axis=1, stride=1, stride_axis=0)[:, :BAND_A]

    n_blocks = q_ref.shape[1] // QB_A
    items = [(m, hh) for m in range(n_blocks) for hh in range(HEADS_PER_STEP_A)]

    def window(m):
        k0 = max(m * QB_A - PAD_A, 0)
        return k0, (m + 1) * QB_A

    def lanes(hh):
        return slice(hh * HEAD_DIM, (hh + 1) * HEAD_DIM)

    def scores(m, hh):
        k0, k1 = window(m)
        return _attn_a_scores(q_ref[hh, m * QB_A:k1, :], k_ref[hh, k0:k1, :])

    def finish(m, hh, e, den):
        k0, k1 = window(m)
        o = _dot(e, v_ref[hh, k0:k1, :])
        o_ref[m * QB_A:k1, lanes(hh)] = (o / den).astype(o_ref.dtype)

    s_queue = [scores(*items[t]) for t in range(QK_AHEAD)]
    pending = None
    for t, (m, hh) in enumerate(items):
        s = s_queue.pop(0)
        if t + QK_AHEAD < len(items):
            s_queue.append(scores(*items[t + QK_AHEAD]))
        k0, k1 = window(m)
        lo = BAND_A - (k1 - k0)
        e, den = _attn_a_softmax(s, bias_s[hh, :, lo:], lo)
        if pending is not None:
            finish(*pending)
        pending = (m, hh, e, den)
    finish(*pending)

    w_o_ref[...] = w_ref[...].astype(BF16)


def _attn_a(z, bias_rows, w_cast):
    b, _, s, _ = z.shape
    hp = HEADS_PER_STEP_A
    n_groups = N_HEADS_A // hp
    width = hp * HEAD_DIM
    wk, wn = w_cast.shape
    slab = wk // (n_groups * b)
    return pl.pallas_call(
        _attn_a_kernel,
        grid=(n_groups, b),
        in_specs=[pl.BlockSpec((hp, 1, REL_ROW), lambda h, bi: (h, 0, 0)),
                  pl.BlockSpec((None, hp, s, HEAD_DIM), lambda h, bi: (bi, h, 0, 0)),
                  pl.BlockSpec((None, hp, s, HEAD_DIM), lambda h, bi: (bi, n_groups + h, 0, 0)),
                  pl.BlockSpec((None, hp, s, HEAD_DIM), lambda h, bi: (bi, 2 * n_groups + h, 0, 0)),
                  pl.BlockSpec((slab, wn), lambda h, bi: (h * b + bi, 0))],
        out_specs=[pl.BlockSpec((None, s, width), lambda h, bi: (bi, 0, h)),
                   pl.BlockSpec((slab, wn), lambda h, bi: (h * b + bi, 0))],
        out_shape=[jax.ShapeDtypeStruct((b, s, D_A), BF16),
                   jax.ShapeDtypeStruct((wk, wn), BF16)],
        scratch_shapes=[pltpu.VMEM((hp, QB_A, BAND_A), F32)],
        compiler_params=_cparams(2),
        name="attn_band",
    )(bias_rows, z, z, z, w_cast)


def _fox_prep_kernel(u_ref, wft_ref, bf_ref, f_ref, ft_ref):
    seq, d = u_ref.shape
    wft = jnp.concatenate([wft_ref[...], jnp.zeros((LANES - N_HEADS_B, d), F32)], axis=0).astype(BF16)
    fl = lax.dot_general(u_ref[...], wft, (((1,), (1,)), ((), ())), preferred_element_type=F32)
    fl = fl + bf_ref[...]
    x = jnp.minimum(fl, 0.0) - jnp.log1p(jnp.exp(-jnp.abs(fl)))
    row = lax.broadcasted_iota(jnp.int32, x.shape, 0)
    shift = 1
    while shift < seq:
        x = x + jnp.where(row >= shift, pltpu.roll(x, shift, axis=0), 0.0)
        shift *= 2
    x = x * LOG2E
    f_ref[...] = x
    ft_ref[...] = jnp.transpose(x)[:N_HEADS_B, :]


def _fox_prep(u3, w_in_t, b_f):
    b, s, d = u3.shape
    return pl.pallas_call(
        _fox_prep_kernel,
        grid=(b,),
        in_specs=[pl.BlockSpec((None, s, d), lambda bi: (bi, 0, 0)),
                  pl.BlockSpec((N_HEADS_B, d), lambda bi: (D_QKV // N_HEADS_B, 0)),
                  pl.BlockSpec((1, LANES), lambda bi: (0, 0))],
        out_specs=[pl.BlockSpec((None, s, LANES), lambda bi: (bi, 0, 0)),
                   pl.BlockSpec((None, N_HEADS_B, s), lambda bi: (bi, 0, 0))],
        out_shape=[jax.ShapeDtypeStruct((b, s, LANES), F32),
                   jax.ShapeDtypeStruct((b, N_HEADS_B, s), F32)],
        compiler_params=_cparams(1),
        name="fox_prep",
    )(u3, w_in_t, b_f)


def _fox_kernel(q_ref, k_ref, v_ref, fq_ref, fk_ref, w_ref, o_ref, w_o_ref, v_aug):
    seq = q_ref.shape[1]
    n_blocks = seq // TQ_B
    group = pl.program_id(1)
    lane = lax.broadcasted_iota(jnp.int32, (TQ_B, LANES), 1)
    row = lax.broadcasted_iota(jnp.int32, (TQ_B, TQ_B), 0)
    col = lax.broadcasted_iota(jnp.int32, (TQ_B, TQ_B), 1)
    nt = (((1,), (1,)), ((), ()))

    def lanes(hh):
        return slice(hh * HEAD_DIM, (hh + 1) * HEAD_DIM)

    for hh in range(HEADS_PER_STEP_B):
        v_aug[hh, :, :HEAD_DIM] = v_ref[hh]
        v_aug[hh, :, HEAD_DIM:] = jnp.ones((seq, HEAD_DIM), BF16)

    items = []
    for t in range(n_blocks):
        for hh in range(HEADS_PER_STEP_B):
            items.append((hh, t if hh % 2 == 0 else n_blocks - 1 - t))

    def scores(hh, qi):
        q0, q1 = qi * TQ_B, (qi + 1) * TQ_B
        q = q_ref[hh, q0:q1, :]
        s_diag = lax.dot_general(q, k_ref[hh, q0:q1, :], nt, preferred_element_type=F32)
        s_past = (lax.dot_general(q, k_ref[hh, 0:q0, :], nt, preferred_element_type=F32)
                  if qi > 0 else None)
        return s_diag, s_past

    def softmax(hh, qi, s_d, s_p):
        q0, q1 = qi * TQ_B, (qi + 1) * TQ_B
        h = group * HEADS_PER_STEP_B + hh
        fq = jnp.sum(jnp.where(lane == h, fq_ref[q0:q1, :], 0.0), axis=1, keepdims=True)
        s_d = (s_d * (SCALE * LOG2E) + fq) - fk_ref[pl.ds(h, 1), q0:q1]
        s_d = jnp.where(row >= col, s_d, NEG_INF)
        mx = jnp.max(s_d, axis=-1, keepdims=True)
        if qi > 0:
            s_p = (s_p * (SCALE * LOG2E) + fq) - fk_ref[pl.ds(h, 1), 0:q0]
            mx = jnp.maximum(mx, jnp.max(s_p, axis=-1, keepdims=True))
        e_d = jnp.exp2(s_d - mx).astype(BF16)
        e_p = jnp.exp2(s_p - mx).astype(BF16) if qi > 0 else None
        return e_d, e_p

    def finish(hh, qi, e_d, e_p):
        q0, q1 = qi * TQ_B, (qi + 1) * TQ_B
        acc = _dot(e_d, v_aug[hh, q0:q1, :])
        if e_p is not None:
            acc = acc + _dot(e_p, v_aug[hh, 0:q0, :])
        o_ref[q0:q1, lanes(hh)] = (acc[:, :HEAD_DIM] / acc[:, HEAD_DIM:]).astype(o_ref.dtype)

    s_queue = [scores(*items[t]) for t in range(QK_AHEAD)]
    pending = None
    for t, (hh, qi) in enumerate(items):
        s_d, s_p = s_queue.pop(0)
        if t + QK_AHEAD < len(items):
            s_queue.append(scores(*items[t + QK_AHEAD]))
        e_d, e_p = softmax(hh, qi, s_d, s_p)
        if pending is not None:
            finish(*pending)
        pending = (hh, qi, e_d, e_p)
    finish(*pending)

    w_o_ref[...] = w_ref[...].astype(BF16)


def _fox(z, f_col, f_row, w_cast):
    b, _, s, _ = z.shape
    hp = HEADS_PER_STEP_B
    n_groups = N_HEADS_B // hp
    width = hp * HEAD_DIM
    base = 3 * N_HEADS_A // hp
    wk, wn = w_cast.shape
    slab = wk // (b * n_groups)
    return pl.pallas_call(
        _fox_kernel,
        grid=(b, n_groups),
        in_specs=[pl.BlockSpec((None, hp, s, HEAD_DIM), lambda bi, h: (bi, base + h, 0, 0)),
                  pl.BlockSpec((None, hp, s, HEAD_DIM), lambda bi, h: (bi, base + n_groups + h, 0, 0)),
                  pl.BlockSpec((None, hp, s, HEAD_DIM), lambda bi, h: (bi, base + 2 * n_groups + h, 0, 0)),
                  pl.BlockSpec((None, s, LANES), lambda bi, h: (bi, 0, 0)),
                  pl.BlockSpec((None, N_HEADS_B, s), lambda bi, h: (bi, 0, 0)),
                  pl.BlockSpec((slab, wn), lambda bi, h: (bi * n_groups + h, 0))],
        out_specs=[pl.BlockSpec((None, None, s, width), lambda bi, h: (bi, h, 0, 0)),
                   pl.BlockSpec((slab, wn), lambda bi, h: (bi * n_groups + h, 0))],
        out_shape=[jax.ShapeDtypeStruct((b, n_groups, s, width), BF16),
                   jax.ShapeDtypeStruct((wk, wn), BF16)],
        scratch_shapes=[pltpu.VMEM((hp, s, 2 * HEAD_DIM), BF16)],
        compiler_params=_cparams(2),
        name="attn_fox",
    )(z, z, z, f_col, f_row, w_cast)


def _rel_bias_rows(rel_bias):
    far = rel_bias[:, 2 * REL_CLIP:]
    n_head = PAD_A - REL_CLIP
    n_tail = REL_ROW - n_head - (2 * REL_CLIP + 1)
    rows = jnp.concatenate([jnp.tile(far, (1, n_head)), jnp.flip(rel_bias, axis=1),
                            jnp.tile(far, (1, n_tail))], axis=1)
    return rows[:, None, :]


def kernel(x, p, g_ffn1, w_ffn1_gu, w_ffn1_down, g_mix, w_in, b_forget, rel_bias, w_out,
           g_ffn2, w_ffn2_gu, w_ffn2_down, g_ple, w_ple_gate, w_ple_proj, g_final):
    b, s, d = x.shape
    m = b * s
    depth = p.shape[0]
    h = x.reshape(m, d)
    for i in range(depth):
        hid, w_down1 = _gateup(h, g_ffn1[i], w_ffn1_gu[i], w_ffn1_down[i], GATEUP_TN_FUSED)
        h, u, w_down2 = _down(hid, w_down1, h, g_mix[i], w_ffn2_down[i])
        w_in_t = jnp.swapaxes(w_in[i], 0, 1)
        z = _proj(u, w_in_t, D_QKV, b)
        b_f = jnp.pad(b_forget[i], (0, LANES - N_HEADS_B)).reshape(1, LANES)
        f_col, f_row = _fox_prep(u.reshape(b, s, d), w_in_t, b_f)
        o_a, w_out_bf = _attn_a(z, _rel_bias_rows(rel_bias[i]), w_out[i])
        o_b, w_gate_bf = _fox(z, f_col, f_row, w_ple_gate[i])
        h, xn = _outproj(o_a.reshape(m, D_A), o_b, w_out_bf, h, g_ffn2[i])
        hid, = _gateup(xn, None, w_ffn2_gu[i], None, GATEUP_TN)
        h = _down_ple(hid, w_down2, h, g_ple[i], p[i].reshape(m, D_PLE), w_gate_bf, w_ple_proj[i],
                      g_final, i == depth - 1)
    return h.reshape(b, s, d)
```
